```python
import math
import jax
import jax.numpy as jnp
from jax import lax
import numpy as np

D_MODEL = 2048
BATCH = 2
SEQ = 4096
DEPTH = 2

GRID_W = 64
CTX_LEN = 256

SSM_WIDTH = 1024
SSM_GROUP = 16
SSM_GROUPS = SSM_WIDTH // SSM_GROUP
SSM_STATE = 64
SSM_DT_MIN = 0.001
SSM_DT_MAX = 0.1

WIN_HEADS = 16
WIN_KV_HEADS = 2
WIN_HEAD_DIM = 64
WINDOW = 128

DIFF_HEADS = 8
DIFF_QK_DIM = 64
DIFF_V_DIM = 2 * DIFF_QK_DIM

ATTN_BLOCK = 128
ROPE_BASE = 10000.0

N_EXPERTS = 32
TOP_K = 4
D_EXPERT = D_MODEL
SWIGLU_LIMIT = 7.0
SWIGLU_ALPHA = 1.702
EXPERT_BLOCK = 128

NORM_EPS = 1e-6
SUBLN_EPS = 1e-5
NEG_INF = -1e30

WIN_Q_WIDTH = WIN_HEADS * WIN_HEAD_DIM
WIN_KV_WIDTH = WIN_KV_HEADS * WIN_HEAD_DIM
DIFF_QK_WIDTH = DIFF_HEADS * 2 * DIFF_QK_DIM
DIFF_V_WIDTH = DIFF_HEADS * DIFF_V_DIM
IN_SIZES = (SSM_WIDTH, WIN_Q_WIDTH, WIN_KV_WIDTH, WIN_KV_WIDTH, DIFF_QK_WIDTH, DIFF_QK_WIDTH, DIFF_V_WIDTH, D_MODEL, D_MODEL, D_MODEL)
IN_WIDTH = SSM_WIDTH + WIN_Q_WIDTH + 2 * WIN_KV_WIDTH + 2 * DIFF_QK_WIDTH + DIFF_V_WIDTH + 3 * D_MODEL

kernel_name = 'hybrid_s5_swa_diffattn_moe_dit'


def rmsnorm(x, gain, eps=NORM_EPS):
    xf = x.astype(jnp.float32)
    y = xf * lax.rsqrt(jnp.mean(xf * xf, axis=-1, keepdims=True) + eps)
    return (y * gain.astype(jnp.float32)).astype(x.dtype)


def axial_rope_tables(n_tokens, head_dim, dtype):
    rows = n_tokens // GRID_W
    r, col = jnp.meshgrid(jnp.arange(rows, dtype=jnp.float32), jnp.arange(GRID_W, dtype=jnp.float32), indexing='ij')
    half = head_dim // 2
    inv_freq = ROPE_BASE ** (-jnp.arange(0, half, 2, dtype=jnp.float32) / half)
    ang_r = r.reshape(-1)[:, None] * inv_freq[None, :]
    ang_c = col.reshape(-1)[:, None] * inv_freq[None, :]
    return (jnp.cos(ang_r).astype(dtype), jnp.sin(ang_r).astype(dtype), jnp.cos(ang_c).astype(dtype), jnp.sin(ang_c).astype(dtype))


def _rotate(v, cos, sin):
    v1, v2 = jnp.split(v, 2, axis=-1)
    cos = cos[None, :, None, :]
    sin = sin[None, :, None, :]
    return jnp.concatenate([v1 * cos - v2 * sin, v2 * cos + v1 * sin], axis=-1)


def apply_axial_rope(x, tables):
    cos_r, sin_r, cos_c, sin_c = tables
    half = x.shape[-1] // 2
    return jnp.concatenate([_rotate(x[..., :half], cos_r, sin_r), _rotate(x[..., half:], cos_c, sin_c)], axis=-1)


def window_band_mask(n_lat):
    nb = n_lat // ATTN_BLOCK
    base = jnp.arange(nb)[:, None, None] * ATTN_BLOCK
    qpos = base + jnp.arange(ATTN_BLOCK)[None, :, None]
    kpos = base - ATTN_BLOCK + jnp.arange(3 * ATTN_BLOCK)[None, None, :]
    return (jnp.abs(qpos - kpos) <= WINDOW) & (kpos >= 0) & (kpos < n_lat)


def s5_discretize(lam_re, lam_im, log_dt, b_re, b_im):
    lam_re = jnp.minimum(lam_re.astype(jnp.float32), -1e-4)
    lam_im = lam_im.astype(jnp.float32)
    dt = jnp.exp(log_dt.astype(jnp.float32))[:, None]
    mag = jnp.exp(lam_re * dt)
    a_re = mag * jnp.cos(lam_im * dt)
    a_im = mag * jnp.sin(lam_im * dt)
    den = lam_re * lam_re + lam_im * lam_im
    k_re = ((a_re - 1.0) * lam_re + a_im * lam_im) / den
    k_im = (a_im * lam_re - (a_re - 1.0) * lam_im) / den
    b_re = b_re.astype(jnp.float32)
    b_im = b_im.astype(jnp.float32)
    bb_re = k_re[..., None] * b_re - k_im[..., None] * b_im
    bb_im = k_re[..., None] * b_im + k_im[..., None] * b_re
    return a_re, a_im, bb_re, bb_im


def _linear_recurrence_combine(e1, e2):
    a1r, a1i, b1r, b1i = e1
    a2r, a2i, b2r, b2i = e2
    return (a2r * a1r - a2i * a1i, a2r * a1i + a2i * a1r, a2r * b1r - a2i * b1i + b2r, a2r * b1i + a2i * b1r + b2i)


def s5_states(u, a_re, a_im, bb_re, bb_im, s0, reverse):
    bu_re = jnp.einsum('blgh,gph->blgp', u, bb_re)
    bu_im = jnp.einsum('blgh,gph->blgp', u, bb_im)
    if s0 is not None:
        s0_re, s0_im = s0
        first = -1 if reverse else 0
        bu_re = bu_re.at[:, first].add(a_re * s0_re - a_im * s0_im)
        bu_im = bu_im.at[:, first].add(a_re * s0_im + a_im * s0_re)
    n = u.shape[1]
    a_re_seq = jnp.broadcast_to(a_re, (1, n) + a_re.shape)
    a_im_seq = jnp.broadcast_to(a_im, (1, n) + a_im.shape)
    _, _, s_re, s_im = lax.associative_scan(_linear_recurrence_combine, (a_re_seq, a_im_seq, bu_re, bu_im), reverse=reverse, axis=1)
    last = 0 if reverse else -1
    return s_re, s_im, (s_re[:, last], s_im[:, last])


def s5_readout(s_re, s_im, c_re, c_im):
    return jnp.einsum('blgp,ghp->blgh', s_re, c_re) - jnp.einsum('blgp,ghp->blgh', s_im, c_im)


def s5_glu(y, w_glu, b_glu):
    y = jax.nn.gelu(y)
    return y * jax.nn.sigmoid(y @ w_glu + b_glu)


def sink_softmax(s, sink):
    m = jnp.maximum(jnp.max(s, axis=-1, keepdims=True), sink)
    e = jnp.exp(s - m)
    return e / (jnp.sum(e, axis=-1, keepdims=True) + jnp.exp(sink - m))


def window_sink_attention(q, k, v, k_ctx, v_ctx, sink, band_mask):
    bsz, n_lat, n_heads, dh = q.shape
    n_kv = k.shape[2]
    grp = n_heads // n_kv
    nb = n_lat // ATTN_BLOCK
    scale = dh ** -0.5
    qb = q.reshape(bsz, nb, ATTN_BLOCK, n_kv, grp, dh)

    def band(t):
        tp = jnp.pad(t, ((0, 0), (ATTN_BLOCK, ATTN_BLOCK), (0, 0), (0, 0))).reshape(bsz, nb + 2, ATTN_BLOCK, n_kv, dh)
        return jnp.concatenate([tp[:, :-2], tp[:, 1:-1], tp[:, 2:]], axis=2)

    kb, vb = band(k), band(v)
    s_loc = jnp.einsum('bnqhgd,bnkhd->bnhgqk', qb, kb).astype(jnp.float32) * scale
    s_loc = jnp.where(band_mask[None, :, None, None], s_loc, NEG_INF)
    s_ctx = jnp.einsum('bnqhgd,bkhd->bnhgqk', qb, k_ctx).astype(jnp.float32) * scale
    p = sink_softmax(jnp.concatenate([s_loc, s_ctx], axis=-1), sink.astype(jnp.float32).reshape(1, 1, n_kv, grp, 1, 1))
    p = p.astype(v.dtype)
    n_loc = 3 * ATTN_BLOCK
    o = jnp.einsum('bnhgqk,bnkhd->bnqhgd', p[..., :n_loc], vb) + jnp.einsum('bnhgqk,bkhd->bnqhgd', p[..., n_loc:], v_ctx)
    return o.reshape(bsz, n_lat, n_heads * dh)


def context_sink_attention(q, k, v, sink):
    bsz, n_ctx, n_heads, dh = q.shape
    n_kv = k.shape[2]
    grp = n_heads // n_kv
    qg = q.reshape(bsz, n_ctx, n_kv, grp, dh)
    s = jnp.einsum('bqhgd,bkhd->bhgqk', qg, k).astype(jnp.float32) * (dh ** -0.5)
    p = sink_softmax(s, sink.astype(jnp.float32).reshape(1, n_kv, grp, 1, 1)).astype(v.dtype)
    return jnp.einsum('bhgqk,bkhd->bqhgd', p, v).reshape(bsz, n_ctx, n_heads * dh)


def diff_attention(q1, q2, k1, k2, v, lam):
    bsz, n_q, n_heads, dq = q1.shape
    nb = n_q // ATTN_BLOCK
    scale = dq ** -0.5

    def to_blocks(t):
        return jnp.moveaxis(t.reshape(bsz, nb, ATTN_BLOCK, n_heads, t.shape[-1]), 1, 0)

    def one_block(qs):
        q1b, q2b = qs
        s1 = jnp.einsum('bqhd,bkhd->bhqk', q1b, k1).astype(jnp.float32) * scale
        s2 = jnp.einsum('bqhd,bkhd->bhqk', q2b, k2).astype(jnp.float32) * scale
        a = jax.nn.softmax(s1, axis=-1) - lam * jax.nn.softmax(s2, axis=-1)
        return jnp.einsum('bhqk,bkhd->bqhd', a.astype(v.dtype), v)

    o = lax.map(one_block, (to_blocks(q1), to_blocks(q2)))
    return jnp.moveaxis(o, 0, 1).reshape(bsz, n_q, n_heads, v.shape[-1])


def diff_post(o, subln_gain, lambda_init):
    o = rmsnorm(o, subln_gain, SUBLN_EPS) * (1.0 - lambda_init)
    return o.reshape(o.shape[0], o.shape[1], -1)


def clamped_swiglu(x_glu, x_lin):
    x_glu = jnp.minimum(x_glu, SWIGLU_LIMIT)
    x_lin = jnp.clip(x_lin, -SWIGLU_LIMIT, SWIGLU_LIMIT)
    return x_glu * jax.nn.sigmoid(SWIGLU_ALPHA * x_glu) * (x_lin + 1.0)


def moe_ffn(h, w_router, b_router, w1, b1, w2, b2):
    n_tok, d = h.shape
    n_exp = w1.shape[0]
    d_ff = w2.shape[1]
    logits = jnp.dot(h, w_router).astype(jnp.float32) + b_router.astype(jnp.float32)
    top_val, top_idx = lax.top_k(logits, TOP_K)
    gates = jax.nn.softmax(top_val, axis=-1)
    n_assign = n_tok * TOP_K
    flat_e = top_idx.reshape(-1)
    flat_t = jnp.arange(n_assign, dtype=jnp.int32) // TOP_K
    flat_g = gates.reshape(-1)
    order = jnp.argsort(flat_e)
    se, st, sg = flat_e[order], flat_t[order], flat_g[order]
    counts = jnp.zeros((n_exp,), jnp.int32).at[flat_e].add(1)
    padded = (counts + EXPERT_BLOCK - 1) // EXPERT_BLOCK * EXPERT_BLOCK
    start = jnp.cumsum(counts) - counts
    pend = jnp.cumsum(padded)
    pstart = pend - padded
    dest = pstart[se] + jnp.arange(n_assign, dtype=jnp.int32) - start[se]
    n_blocks = -(-n_assign // EXPERT_BLOCK) + n_exp
    n_rows = n_blocks * EXPERT_BLOCK
    row_tok = jnp.zeros((n_rows,), jnp.int32).at[dest].set(st)
    row_gate = jnp.zeros((n_rows,), jnp.float32).at[dest].set(sg)
    block_exp = jnp.minimum(jnp.searchsorted(pend, jnp.arange(n_blocks, dtype=jnp.int32) * EXPERT_BLOCK, side='right'), n_exp - 1)
    xb = h[row_tok].reshape(n_blocks, EXPERT_BLOCK, d)

    def expert_block(args):
        xe, e = args
        hid = xe @ w1[e] + b1[e]
        act = clamped_swiglu(hid[..., :d_ff], hid[..., d_ff:])
        return act @ w2[e] + b2[e]

    yb = lax.map(expert_block, (xb, block_exp)).reshape(n_rows, d)
    return jax.ops.segment_sum(yb * row_gate[:, None].astype(yb.dtype), row_tok, num_segments=n_tok)


def gated_merge(o_ssm, o_win, o_diff, g_ssm, g_win, g_diff, p):
    m = (jax.nn.sigmoid(g_ssm) * (o_ssm @ p['w_branch_ssm'])
         + jax.nn.sigmoid(g_win) * (o_win @ p['w_branch_win'])
         + jax.nn.sigmoid(g_diff) * (o_diff @ p['w_branch_diff']))
    return m @ p['w_out']


def hybrid_mixer(hx, hc, p, rope_win, rope_diff, band_mask, lambda_init, with_ctx):
    bsz, n_lat, _ = hx.shape
    n_ctx = hc.shape[1]
    splits = [int(s) for s in np.cumsum(IN_SIZES)[:-1]]
    (u_x, qw_x, kw_x, vw_x, qd_x, kd_x, vd_x, gs_x, gw_x, gd_x) = jnp.split(hx @ p['w_in'], splits, axis=-1)
    (u_c, qw_c, kw_c, vw_c, qd_c, kd_c, vd_c, gs_c, gw_c, gd_c) = jnp.split(hc @ p['w_in'], splits, axis=-1)

    u_x = u_x.astype(jnp.float32).reshape(bsz, n_lat, SSM_GROUPS, SSM_GROUP)
    u_c = u_c.astype(jnp.float32).reshape(bsz, n_ctx, SSM_GROUPS, SSM_GROUP)
    d_skip = p['ssm_d'].astype(jnp.float32).reshape(SSM_GROUPS, SSM_GROUP)
    y_x = d_skip * u_x
    y_c = d_skip * u_c if with_ctx else None
    for direction, reverse in ((0, False), (1, True)):
        a_re, a_im, bb_re, bb_im = s5_discretize(p['ssm_lambda_re'][direction], p['ssm_lambda_im'][direction], p['ssm_log_dt'][direction], p['ssm_b_re'][direction], p['ssm_b_im'][direction])
        c_re = p['ssm_c_re'][direction].astype(jnp.float32)
        c_im = p['ssm_c_im'][direction].astype(jnp.float32)
        sc_re, sc_im, final_c = s5_states(u_c, a_re, a_im, bb_re, bb_im, None, reverse)
        sx_re, sx_im, _ = s5_states(u_x, a_re, a_im, bb_re, bb_im, final_c, reverse)
        y_x = y_x + s5_readout(sx_re, sx_im, c_re, c_im)
        if with_ctx:
            y_c = y_c + s5_readout(sc_re, sc_im, c_re, c_im)
    o_ssm_x = s5_glu(y_x.reshape(bsz, n_lat, SSM_WIDTH).astype(hx.dtype), p['w_glu'], p['b_glu'])

    q_w = apply_axial_rope(qw_x.reshape(bsz, n_lat, WIN_HEADS, WIN_HEAD_DIM), rope_win)
    k_w = apply_axial_rope(kw_x.reshape(bsz, n_lat, WIN_KV_HEADS, WIN_HEAD_DIM), rope_win)
    v_w = vw_x.reshape(bsz, n_lat, WIN_KV_HEADS, WIN_HEAD_DIM)
    k_wc = kw_c.reshape(bsz, n_ctx, WIN_KV_HEADS, WIN_HEAD_DIM)
    v_wc = vw_c.reshape(bsz, n_ctx, WIN_KV_HEADS, WIN_HEAD_DIM)
    o_win_x = window_sink_attention(q_w, k_w, v_w, k_wc, v_wc, p['win_sink'], band_mask)

    qd = qd_x.reshape(bsz, n_lat, DIFF_HEADS, 2, DIFF_QK_DIM)
    kd = kd_x.reshape(bsz, n_lat, DIFF_HEADS, 2, DIFF_QK_DIM)
    qdc = qd_c.reshape(bsz, n_ctx, DIFF_HEADS, 2, DIFF_QK_DIM)
    kdc = kd_c.reshape(bsz, n_ctx, DIFF_HEADS, 2, DIFF_QK_DIM)
    v_d = vd_x.reshape(bsz, n_lat, DIFF_HEADS, DIFF_V_DIM)
    v_dc = vd_c.reshape(bsz, n_ctx, DIFF_HEADS, DIFF_V_DIM)
    lam_p = p['diff_lambda'].astype(jnp.float32)
    lam = jnp.exp(jnp.sum(lam_p[0] * lam_p[1])) - jnp.exp(jnp.sum(lam_p[2] * lam_p[3])) + lambda_init
    k1_all = jnp.concatenate([apply_axial_rope(kd[:, :, :, 0], rope_diff), kdc[:, :, :, 0]], axis=1)
    k2_all = jnp.concatenate([apply_axial_rope(kd[:, :, :, 1], rope_diff), kdc[:, :, :, 1]], axis=1)
    v_all = jnp.concatenate([v_d, v_dc], axis=1)
    o_diff_x = diff_attention(apply_axial_rope(qd[:, :, :, 0], rope_diff), apply_axial_rope(qd[:, :, :, 1], rope_diff), k1_all, k2_all, v_all, lam)
    o_diff_x = diff_post(o_diff_x, p['diff_subln'], lambda_init)

    out_x = gated_merge(o_ssm_x, o_win_x, o_diff_x, gs_x, gw_x, gd_x, p)
    if not with_ctx:
        return out_x, None

    o_ssm_c = s5_glu(y_c.reshape(bsz, n_ctx, SSM_WIDTH).astype(hc.dtype), p['w_glu'], p['b_glu'])
    o_win_c = context_sink_attention(qw_c.reshape(bsz, n_ctx, WIN_HEADS, WIN_HEAD_DIM), k_wc, v_wc, p['win_sink'])
    o_diff_c = diff_post(diff_attention(qdc[:, :, :, 0], qdc[:, :, :, 1], kdc[:, :, :, 0], kdc[:, :, :, 1], v_dc, lam), p['diff_subln'], lambda_init)
    out_c = gated_merge(o_ssm_c, o_win_c, o_diff_c, gs_c, gw_c, gd_c, p)
    return out_x, out_c


def hybrid_layer(x, ctx, c, c_ctx, p, rope_win, rope_diff, band_mask, lambda_init, with_ctx):
    bsz, n_lat, d = x.shape
    n_ctx = ctx.shape[1]
    mod_x = jax.nn.silu(c) @ p['w_mod'] + p['b_mod']
    mod_c = jax.nn.silu(c_ctx) @ p['w_mod'] + p['b_mod']
    sh1, sc1, g1, sh2, sc2, g2 = jnp.split(mod_x[:, None, :], 6, axis=-1)
    sh1c, sc1c, g1c, sh2c, sc2c, g2c = jnp.split(mod_c, 6, axis=-1)
    hx = rmsnorm(x, p['g_mix']) * (1.0 + sc1) + sh1
    hc = rmsnorm(ctx, p['g_mix']) * (1.0 + sc1c) + sh1c
    mx, mc = hybrid_mixer(hx, hc, p, rope_win, rope_diff, band_mask, lambda_init, with_ctx)
    x = x + g1 * mx
    hx = rmsnorm(x, p['g_ffn']) * (1.0 + sc2) + sh2
    if with_ctx:
        ctx = ctx + g1c * mc
        hc = rmsnorm(ctx, p['g_ffn']) * (1.0 + sc2c) + sh2c
        tokens = jnp.concatenate([hx.reshape(-1, d), hc.reshape(-1, d)], axis=0)
        f = moe_ffn(tokens, p['w_router'], p['b_router'], p['w_exp1'], p['b_exp1'], p['w_exp2'], p['b_exp2'])
        x = x + g2 * f[: bsz * n_lat].reshape(bsz, n_lat, d)
        ctx = ctx + g2c * f[bsz * n_lat:].reshape(bsz, n_ctx, d)
    else:
        f = moe_ffn(hx.reshape(-1, d), p['w_router'], p['b_router'], p['w_exp1'], p['b_exp1'], p['w_exp2'], p['b_exp2'])
        x = x + g2 * f.reshape(bsz, n_lat, d)
    return x, ctx


def setup_inputs(seed: int = 0) -> dict:
    key = jax.random.key(seed)
    ks = jax.random.split(key, 33)
    f32 = jnp.float32

    def nrm(i, shape, scale):
        return scale * jax.random.normal(ks[i], shape, f32)

    D = D_MODEL
    F = D_EXPERT
    lam_im_base = math.pi * jnp.arange(SSM_STATE, dtype=f32)
    return {
        'x': nrm(0, (BATCH, SEQ, D), 1.0),
        'c': nrm(1, (BATCH, D), 1.0),
        'ctx': nrm(2, (BATCH, CTX_LEN, D), 1.0),
        'c_ctx': nrm(3, (D,), 1.0),
        'w_mod': nrm(4, (DEPTH, D, 6 * D), 0.5 * D ** -0.5),
        'b_mod': nrm(5, (DEPTH, 6 * D), 0.02),
        'g_mix': 1.0 + nrm(6, (DEPTH, D), 0.02),
        'g_ffn': 1.0 + nrm(7, (DEPTH, D), 0.02),
        'w_in': nrm(8, (DEPTH, D, IN_WIDTH), D ** -0.5),
        'ssm_lambda_re': -0.5 + nrm(9, (DEPTH, 2, SSM_GROUPS, SSM_STATE), 0.01),
        'ssm_lambda_im': lam_im_base + nrm(10, (DEPTH, 2, SSM_GROUPS, SSM_STATE), 0.01),
        'ssm_log_dt': jax.random.uniform(ks[11], (DEPTH, 2, SSM_GROUPS), f32, math.log(SSM_DT_MIN), math.log(SSM_DT_MAX)),
        'ssm_b_re': nrm(12, (DEPTH, 2, SSM_GROUPS, SSM_STATE, SSM_GROUP), (2 * SSM_GROUP) ** -0.5),
        'ssm_b_im': nrm(13, (DEPTH, 2, SSM_GROUPS, SSM_STATE, SSM_GROUP), (2 * SSM_GROUP) ** -0.5),
        'ssm_c_re': nrm(14, (DEPTH, 2, SSM_GROUPS, SSM_GROUP, SSM_STATE), SSM_STATE ** -0.5),
        'ssm_c_im': nrm(15, (DEPTH, 2, SSM_GROUPS, SSM_GROUP, SSM_STATE), SSM_STATE ** -0.5),
        'ssm_d': nrm(16, (DEPTH, SSM_WIDTH), 1.0),
        'w_glu': nrm(17, (DEPTH, SSM_WIDTH, SSM_WIDTH), SSM_WIDTH ** -0.5),
        'b_glu': nrm(18, (DEPTH, SSM_WIDTH), 0.02),
        'win_sink': nrm(19, (DEPTH, WIN_HEADS), 0.5),
        'diff_lambda': nrm(20, (DEPTH, 4, DIFF_QK_DIM), 0.1),
        'diff_subln': 1.0 + nrm(21, (DEPTH, DIFF_V_DIM), 0.02),
        'w_branch_ssm': nrm(22, (DEPTH, SSM_WIDTH, D), SSM_WIDTH ** -0.5),
        'w_branch_win': nrm(23, (DEPTH, WIN_Q_WIDTH, D), WIN_Q_WIDTH ** -0.5),
        'w_branch_diff': nrm(24, (DEPTH, DIFF_V_WIDTH, D), DIFF_V_WIDTH ** -0.5),
        'w_out': nrm(25, (DEPTH, D, D), D ** -0.5),
        'w_router': nrm(26, (DEPTH, D, N_EXPERTS), D ** -0.5),
        'b_router': nrm(27, (DEPTH, N_EXPERTS), 0.01),
        'w_exp1': nrm(28, (DEPTH, N_EXPERTS, D, 2 * F), D ** -0.5),
        'b_exp1': nrm(29, (DEPTH, N_EXPERTS, 2 * F), 0.02),
        'w_exp2': nrm(30, (DEPTH, N_EXPERTS, F, D), F ** -0.5),
        'b_exp2': nrm(31, (DEPTH, N_EXPERTS, D), 0.02),
        'g_final': 1.0 + nrm(32, (D,), 0.02),
    }


def reference(x, c, ctx, c_ctx, w_mod, b_mod, g_mix, g_ffn, w_in, ssm_lambda_re, ssm_lambda_im, ssm_log_dt, ssm_b_re, ssm_b_im, ssm_c_re, ssm_c_im, ssm_d, w_glu, b_glu, win_sink, diff_lambda, diff_subln, w_branch_ssm, w_branch_win, w_branch_diff, w_out, w_router, b_router, w_exp1, b_exp1, w_exp2, b_exp2, g_final):
    n_lat = x.shape[1]
    rope_win = axial_rope_tables(n_lat, WIN_HEAD_DIM, x.dtype)
    rope_diff = axial_rope_tables(n_lat, DIFF_QK_DIM, x.dtype)
    band_mask = window_band_mask(n_lat)
    for i in range(DEPTH):
        p = {
            'w_mod': w_mod[i], 'b_mod': b_mod[i], 'g_mix': g_mix[i], 'g_ffn': g_ffn[i], 'w_in': w_in[i],
            'ssm_lambda_re': ssm_lambda_re[i], 'ssm_lambda_im': ssm_lambda_im[i], 'ssm_log_dt': ssm_log_dt[i],
            'ssm_b_re': ssm_b_re[i], 'ssm_b_im': ssm_b_im[i], 'ssm_c_re': ssm_c_re[i], 'ssm_c_im': ssm_c_im[i],
            'ssm_d': ssm_d[i], 'w_glu': w_glu[i], 'b_glu': b_glu[i], 'win_sink': win_sink[i],
            'diff_lambda': diff_lambda[i], 'diff_subln': diff_subln[i],
            'w_branch_ssm': w_branch_ssm[i], 'w_branch_win': w_branch_win[i], 'w_branch_diff': w_branch_diff[i],
            'w_out': w_out[i], 'w_router': w_router[i], 'b_router': b_router[i],
            'w_exp1': w_exp1[i], 'b_exp1': b_exp1[i], 'w_exp2': w_exp2[i], 'b_exp2': b_exp2[i],
        }
        lambda_init = 0.8 - 0.6 * math.exp(-0.3 * i)
        x, ctx = hybrid_layer(x, ctx, c, c_ctx, p, rope_win, rope_diff, band_mask, lambda_init, i < DEPTH - 1)
    return rmsnorm(x, g_final)
```

```python
import functools
import math

import jax
import jax.numpy as jnp
import numpy as np
from jax import lax
from jax.experimental import pallas as pl
from jax.experimental.pallas import tpu as pltpu

F32 = jnp.float32
BF16 = jnp.bfloat16

GRID_W = 64
SSM_GROUP = 16
WIN_KV_HEADS = 2
WIN_HEAD_DIM = 64
WINDOW = 128
DIFF_QK_DIM = 64
DIFF_V_DIM = 2 * DIFF_QK_DIM
ATTN_BLOCK = 128
ROPE_BASE = 10000.0
TOP_K = 4
SWIGLU_LIMIT = 7.0
SWIGLU_ALPHA = 1.702
NORM_EPS = 1e-6
SUBLN_EPS = 1e-5
NEG_INF = -1e30

LANE = 128
VMEM_LIMIT_BYTES = 56 * 1024 * 1024
S5_CHUNK = 32
MOE_ROWS = 256
ROW_TILE = 512


def _cparams(*sem):
    return pltpu.CompilerParams(dimension_semantics=sem, vmem_limit_bytes=VMEM_LIMIT_BYTES)


def _dot(a, b):
    return jnp.dot(a, b, preferred_element_type=F32)


def _dot_nt(a, b):
    return lax.dot_general(a, b, (((1,), (1,)), ((), ())), preferred_element_type=F32)


def _sigmoid(x):
    return 1.0 / (1.0 + jnp.exp(-x))


def _row_tile(n_rows, cap):
    best = 16
    for t in range(16, cap + 1, 16):
        if n_rows % t == 0:
            best = t
    return best


def _norm_mod_kernel(x_ref, g_ref, sc_ref, sh_ref, o_ref):
    x = x_ref[...]
    y = x * lax.rsqrt(jnp.mean(x * x, axis=-1, keepdims=True) + NORM_EPS) * g_ref[...]
    o_ref[...] = (y * (1.0 + sc_ref[0]) + sh_ref[0]).astype(o_ref.dtype)


def _norm_router_kernel(x_ref, g_ref, sc_ref, sh_ref, wr_ref, br_ref, o_ref, lg_ref):
    x = x_ref[...]
    y = x * lax.rsqrt(jnp.mean(x * x, axis=-1, keepdims=True) + NORM_EPS) * g_ref[...]
    h = y * (1.0 + sc_ref[0]) + sh_ref[0]
    o_ref[...] = h.astype(o_ref.dtype)
    lg_ref[...] = jnp.dot(h, wr_ref[...], preferred_element_type=F32, precision=lax.Precision.HIGHEST) + br_ref[...]


def _stream_tile(n_rows, rows_lat, seq):
    tm = math.gcd(ROW_TILE, seq)
    return math.gcd(tm, n_rows - rows_lat) if n_rows > rows_lat else tm


def _stream_index(i, tm, rows_lat, seq, n_batch):
    r = i * tm
    return jnp.where(r >= rows_lat, n_batch, r // seq)


def norm_mod(x, gain, sc, sh, n_rows, rows_lat, seq, out_dtype, router=None):
    d = x.shape[1]
    tm = _stream_tile(n_rows, rows_lat, seq)
    n_batch = sc.shape[0] - 1
    idx = functools.partial(_stream_index, tm=tm, rows_lat=rows_lat, seq=seq, n_batch=n_batch)
    in_specs = [
        pl.BlockSpec((tm, d), lambda i: (i, 0)),
        pl.BlockSpec((1, d), lambda i: (0, 0)),
        pl.BlockSpec((1, 1, d), lambda i: (idx(i), 0, 0)),
        pl.BlockSpec((1, 1, d), lambda i: (idx(i), 0, 0)),
    ]
    if router is None:
        return pl.pallas_call(
            _norm_mod_kernel,
            grid=(n_rows // tm,),
            in_specs=in_specs,
            out_specs=pl.BlockSpec((tm, d), lambda i: (i, 0)),
            out_shape=jax.ShapeDtypeStruct((n_rows, d), out_dtype),
            compiler_params=_cparams("parallel"),
            name="norm_mod",
        )(x, gain.reshape(1, d), sc, sh)
    w_router, b_router = router
    n_exp = w_router.shape[1]
    return pl.pallas_call(
        _norm_router_kernel,
        grid=(n_rows // tm,),
        in_specs=in_specs + [
            pl.BlockSpec((d, n_exp), lambda i: (0, 0)),
            pl.BlockSpec((1, n_exp), lambda i: (0, 0)),
        ],
        out_specs=[pl.BlockSpec((tm, d), lambda i: (i, 0)), pl.BlockSpec((tm, n_exp), lambda i: (i, 0))],
        out_shape=[jax.ShapeDtypeStruct((n_rows, d), out_dtype), jax.ShapeDtypeStruct((n_rows, n_exp), F32)],
        compiler_params=_cparams("parallel"),
        name="norm_router",
    )(x, gain.reshape(1, d), sc, sh, w_router, b_router.reshape(1, n_exp))


def _mm_kernel(x_ref, w_ref, o_ref, *, pre):
    x = x_ref[...]
    if pre == "silu":
        x = x * _sigmoid(x)
    o_ref[...] = _dot(x.astype(BF16), w_ref[...].astype(BF16)).astype(o_ref.dtype)


def matmul(x, w_stack, layer, n_rows, tm, tn, out_dtype, pre=None, name="matmul"):
    k = x.shape[1]
    n = w_stack.shape[2]
    return pl.pallas_call(
        functools.partial(_mm_kernel, pre=pre),
        grid=(n_rows // tm, n // tn),
        in_specs=[
            pl.BlockSpec((tm, k), lambda i, j: (i, 0)),
            pl.BlockSpec((None, k, tn), lambda i, j: (layer, 0, j)),
        ],
        out_specs=pl.BlockSpec((tm, tn), lambda i, j: (i, j)),
        out_shape=jax.ShapeDtypeStruct((n_rows, n), out_dtype),
        compiler_params=_cparams("parallel", "arbitrary"),
        name=name,
    )(x, w_stack)


def _prep_kernel(p_ref, cos_ref, sin_ref, o_ref, *, rope, scale):
    lane = lax.broadcasted_iota(jnp.int32, (1, LANE), 1)
    first_half = (lane % 32) < 16
    for c in range(p_ref.shape[1] // LANE):
        x = p_ref[:, c * LANE:(c + 1) * LANE]
        if rope:
            partner = jnp.where(first_half, pltpu.roll(x, LANE - 16, axis=1), pltpu.roll(x, 16, axis=1))
            x = x * cos_ref[...] + partner * sin_ref[...]
        if scale != 1.0:
            x = x * scale
        o_ref[:, c * LANE:(c + 1) * LANE] = x.astype(o_ref.dtype)


def attention_operand(proj, row0, n_rows, col0, width, rope, scale, tables, seq):
    cos_t, sin_t = tables
    tm = math.gcd(256, n_rows, seq)
    bw = math.gcd(width, col0, 1024)
    assert row0 % tm == 0 and bw % LANE == 0
    tiles_per_seq = seq // tm
    return pl.pallas_call(
        functools.partial(_prep_kernel, rope=rope, scale=scale),
        grid=(n_rows // tm, width // bw),
        in_specs=[
            pl.BlockSpec((tm, bw), lambda i, j: (row0 // tm + i, col0 // bw + j)),
            pl.BlockSpec((tm, LANE), lambda i, j: (i % tiles_per_seq, 0)),
            pl.BlockSpec((tm, LANE), lambda i, j: (i % tiles_per_seq, 0)),
        ],
        out_specs=pl.BlockSpec((tm, bw), lambda i, j: (i, j)),
        out_shape=jax.ShapeDtypeStruct((n_rows, width), BF16),
        compiler_params=_cparams("parallel", "parallel"),
        name="attention_operand",
    )(proj, cos_t, sin_t)


def rope_tables(seq, head_dim):
    assert head_dim == 64
    rows = seq // GRID_W
    r, col = jnp.meshgrid(jnp.arange(rows, dtype=F32), jnp.arange(GRID_W, dtype=F32), indexing="ij")
    half = head_dim // 2
    inv_freq = ROPE_BASE ** (-jnp.arange(0, half, 2, dtype=F32) / half)
    ang_r = r.reshape(-1)[:, None] * inv_freq[None, :]
    ang_c = col.reshape(-1)[:, None] * inv_freq[None, :]
    cos = jnp.concatenate([jnp.cos(ang_r), jnp.cos(ang_r), jnp.cos(ang_c), jnp.cos(ang_c)], axis=-1)
    sin = jnp.concatenate([-jnp.sin(ang_r), jnp.sin(ang_r), -jnp.sin(ang_c), jnp.sin(ang_c)], axis=-1)
    return jnp.tile(cos, (1, LANE // head_dim)), jnp.tile(sin, (1, LANE // head_dim))


def _win_attn_kernel(sink_ref, q_ref, *refs, n_heads, band, seq):
    if band:
        kp_ref, kc_ref, kn_ref, vp_ref, vc_ref, vn_ref, kx_ref, vx_ref, o_ref = refs
    else:
        kx_ref, vx_ref, o_ref = refs
    blk = q_ref.shape[0]
    dh = WIN_HEAD_DIM
    grp = n_heads // WIN_KV_HEADS
    lane = lax.broadcasted_iota(jnp.int32, (1, LANE), 1)
    kx = kx_ref[...]
    vx = vx_ref[...]
    if band:
        n = pl.program_id(1)
        k_loc = jnp.concatenate([kp_ref[...], kc_ref[...], kn_ref[...]], axis=0)
        v_loc = jnp.concatenate([vp_ref[...], vc_ref[...], vn_ref[...]], axis=0)
        qpos = n * blk + lax.broadcasted_iota(jnp.int32, (blk, 3 * blk), 0)
        kpos = (n - 1) * blk + lax.broadcasted_iota(jnp.int32, (blk, 3 * blk), 1)
        mask = (jnp.abs(qpos - kpos) <= WINDOW) & (kpos >= 0) & (kpos < seq)
    outs = []
    for h in range(n_heads):
        kvh = h // grp
        c = (h * dh) // LANE
        qc = q_ref[:, c * LANE:(c + 1) * LANE]
        q_off = (h * dh) % LANE
        if q_off != kvh * dh:
            qc = jnp.concatenate([qc[:, dh:], qc[:, :dh]], axis=1)
        keep = (lane >= kvh * dh) & (lane < (kvh + 1) * dh)
        qh = jnp.where(keep, qc, jnp.zeros_like(qc))
        sink = sink_ref[h]
        s_x = _dot_nt(qh, kx)
        m = jnp.maximum(jnp.max(s_x, axis=-1, keepdims=True), sink)
        if band:
            s_l = jnp.where(mask, _dot_nt(qh, k_loc), NEG_INF)
            m = jnp.maximum(m, jnp.max(s_l, axis=-1, keepdims=True))
            e_l = jnp.exp(s_l - m)
        e_x = jnp.exp(s_x - m)
        den = jnp.sum(e_x, axis=-1, keepdims=True) + jnp.exp(sink - m)
        if band:
            den = den + jnp.sum(e_l, axis=-1, keepdims=True)
        inv = 1.0 / den
        o = _dot((e_x * inv).astype(BF16), vx)
        if band:
            o = o + _dot((e_l * inv).astype(BF16), v_loc)
        outs.append(o[:, kvh * dh:(kvh + 1) * dh])
    o_ref[...] = jnp.concatenate(outs, axis=1).astype(o_ref.dtype)


def window_attention(sink, q, k, v, k_ctx, v_ctx, n_batch, seq_q, n_ctx, n_heads, band):
    blk = ATTN_BLOCK
    nb = seq_q // blk
    qw = n_heads * WIN_HEAD_DIM
    kvw = WIN_KV_HEADS * WIN_HEAD_DIM
    in_specs = [
        pl.BlockSpec(memory_space=pltpu.SMEM),
        pl.BlockSpec((blk, qw), lambda b, n: (b * nb + n, 0)),
    ]
    args = [sink, q]
    if band:
        band_specs = [
            pl.BlockSpec((blk, kvw), lambda b, n: (b * nb + jnp.maximum(n - 1, 0), 0)),
            pl.BlockSpec((blk, kvw), lambda b, n: (b * nb + n, 0)),
            pl.BlockSpec((blk, kvw), lambda b, n: (b * nb + jnp.minimum(n + 1, nb - 1), 0)),
        ]
        in_specs += band_specs + band_specs
        args += [k, k, k, v, v, v]
    in_specs += [pl.BlockSpec((n_ctx, kvw), lambda b, n: (b, 0))] * 2
    args += [k_ctx, v_ctx]
    return pl.pallas_call(
        functools.partial(_win_attn_kernel, n_heads=n_heads, band=band, seq=seq_q),
        grid=(n_batch, nb),
        in_specs=in_specs,
        out_specs=pl.BlockSpec((blk, qw), lambda b, n: (b * nb + n, 0)),
        out_shape=jax.ShapeDtypeStruct((n_batch * seq_q, qw), BF16),
        compiler_params=_cparams("parallel", "parallel"),
        name="window_attention" if band else "context_window_attention",
    )(*args)


def _diff_attn_kernel(lam_ref, q_ref, gain_ref, *refs, with_lat, post_scale):
    if with_lat:
        kl_ref, vl_ref, kx_ref, vx_ref, o_ref = refs
    else:
        kx_ref, vx_ref, o_ref = refs
    lam = lam_ref[0]
    q = q_ref[...]
    lane = lax.broadcasted_iota(jnp.int32, (1, LANE), 1)
    zero = jnp.zeros_like(q)
    q1 = jnp.where(lane < DIFF_QK_DIM, q, zero)
    q2 = jnp.where(lane >= DIFF_QK_DIM, q, zero)
    kx = kx_ref[...]

    def softmax_parts(qm):
        s_x = _dot_nt(qm, kx)
        m = jnp.max(s_x, axis=-1, keepdims=True)
        if with_lat:
            s_l = _dot_nt(qm, kl_ref[...])
            m = jnp.maximum(m, jnp.max(s_l, axis=-1, keepdims=True))
            e_l = jnp.exp(s_l - m)
        else:
            e_l = None
        e_x = jnp.exp(s_x - m)
        den = jnp.sum(e_x, axis=-1, keepdims=True)
        if with_lat:
            den = den + jnp.sum(e_l, axis=-1, keepdims=True)
        return e_l, e_x, 1.0 / den

    e1l, e1x, inv1 = softmax_parts(q1)
    e2l, e2x, inv2 = softmax_parts(q2)
    w2 = lam * inv2
    o = _dot((e1x * inv1 - e2x * w2).astype(BF16), vx_ref[...])
    if with_lat:
        o = o + _dot((e1l * inv1 - e2l * w2).astype(BF16), vl_ref[...])
    o = o * lax.rsqrt(jnp.mean(o * o, axis=-1, keepdims=True) + SUBLN_EPS) * gain_ref[...]
    o_ref[...] = (o * post_scale).astype(o_ref.dtype)


def diff_attention(lam, gain, q, k_lat, v_lat, k_ctx, v_ctx, n_batch, seq_q, seq_lat, n_ctx, n_heads, post_scale):
    tq = min(256, seq_q)
    nq = seq_q // tq
    with_lat = k_lat is not None
    in_specs = [
        pl.BlockSpec(memory_space=pltpu.SMEM),
        pl.BlockSpec((tq, LANE), lambda b, h, i: (b * nq + i, h)),
        pl.BlockSpec((1, LANE), lambda b, h, i: (0, 0)),
    ]
    args = [lam, q, gain]
    if with_lat:
        in_specs += [pl.BlockSpec((seq_lat, LANE), lambda b, h, i: (b, h))] * 2
        args += [k_lat, v_lat]
    in_specs += [pl.BlockSpec((n_ctx, LANE), lambda b, h, i: (b, h))] * 2
    args += [k_ctx, v_ctx]
    return pl.pallas_call(
        functools.partial(_diff_attn_kernel, with_lat=with_lat, post_scale=post_scale),
        grid=(n_batch, n_heads, nq),
        in_specs=in_specs,
        out_specs=pl.BlockSpec((tq, LANE), lambda b, h, i: (b * nq + i, h)),
        out_shape=jax.ShapeDtypeStruct((n_batch * seq_q, n_heads * LANE), BF16),
        compiler_params=_cparams("parallel", "parallel", "arbitrary"),
        name="diff_attention" if with_lat else "context_diff_attention",
    )(*args)


def s5_matrices(lam_re, lam_im, log_dt, b_re, b_im, c_re, c_im, d_skip, t_chunk):
    n_dir, n_g, n_p = lam_re.shape
    n_h = b_re.shape[-1]
    lam_re = jnp.minimum(lam_re.astype(F32), -1e-4)
    lam_im = lam_im.astype(F32)
    dt = jnp.exp(log_dt.astype(F32))[..., None]
    mag = jnp.exp(lam_re * dt)
    a_re = mag * jnp.cos(lam_im * dt)
    a_im = mag * jnp.sin(lam_im * dt)
    den = lam_re * lam_re + lam_im * lam_im
    k_re = ((a_re - 1.0) * lam_re + a_im * lam_im) / den
    k_im = (a_im * lam_re - (a_re - 1.0) * lam_im) / den
    b_re = b_re.astype(F32)
    b_im = b_im.astype(F32)
    bb_re = k_re[..., None] * b_re - k_im[..., None] * b_im
    bb_im = k_re[..., None] * b_im + k_im[..., None] * b_re
    c_re = c_re.astype(F32)
    c_im = c_im.astype(F32)
    t = jnp.arange(t_chunk + 1, dtype=F32)
    pmag = jnp.exp(lam_re[..., None] * dt[..., None] * t)
    pang = lam_im[..., None] * dt[..., None] * t
    pw_re = pmag * jnp.cos(pang)
    pw_im = pmag * jnp.sin(pang)
    hi = lax.Precision.HIGHEST
    ca_re = c_re[..., None] * pw_re[:, :, None] - c_im[..., None] * pw_im[:, :, None]
    ca_im = c_re[..., None] * pw_im[:, :, None] + c_im[..., None] * pw_re[:, :, None]
    kern = (jnp.einsum("dghpt,dgpk->dgthk", ca_re[..., :t_chunk], bb_re, precision=hi)
            - jnp.einsum("dghpt,dgpk->dgthk", ca_im[..., :t_chunk], bb_im, precision=hi))
    j = jnp.arange(t_chunk)[:, None]
    l = jnp.arange(t_chunk)[None, :]
    lag_f = jnp.clip(l - j, 0, t_chunk - 1)
    lag_b = jnp.clip(j - l, 0, t_chunk - 1)
    kf = jnp.where((l >= j)[None, :, :, None, None], kern[0][:, lag_f], 0.0)
    kb = jnp.where((j >= l)[None, :, :, None, None], kern[1][:, lag_b], 0.0)
    m = jnp.transpose(kf + kb, (0, 1, 4, 2, 3)).reshape(n_g, t_chunk * n_h, t_chunk * n_h)
    jj = jnp.arange(t_chunk)
    pf_re, pf_im = pw_re[0][..., t_chunk - 1 - jj], pw_im[0][..., t_chunk - 1 - jj]
    pb_re, pb_im = pw_re[1][..., jj], pw_im[1][..., jj]

    def p_block(pr, pi, br, bi):
        re = pr[..., None] * br[:, :, None, :] - pi[..., None] * bi[:, :, None, :]
        im = pr[..., None] * bi[:, :, None, :] + pi[..., None] * br[:, :, None, :]
        to = lambda z: jnp.transpose(z, (0, 2, 3, 1)).reshape(n_g, t_chunk * n_h, n_p)
        return to(re), to(im)

    pfr, pfi = p_block(pf_re, pf_im, bb_re[0], bb_im[0])
    pbr, pbi = p_block(pb_re, pb_im, bb_re[1], bb_im[1])
    p_mat = jnp.concatenate([pfr, pfi, pbr, pbi], axis=-1)
    ll = jnp.arange(t_chunk)

    def q_block(d, idx):
        re = ca_re[d][..., idx]
        im = ca_im[d][..., idx]
        to = lambda z: jnp.transpose(z, (0, 2, 3, 1)).reshape(n_g, n_p, t_chunk * n_h)
        return to(re), -to(im)

    qfr, qfi = q_block(0, ll + 1)
    qbr, qbi = q_block(1, t_chunk - ll)
    q_mat = jnp.concatenate([qfr, qfi, qbr, qbi], axis=1)
    d_tile = jnp.tile(d_skip.astype(F32).reshape(n_g, 1, n_h), (1, 1, t_chunk))
    at_re, at_im = pw_re[..., t_chunk], pw_im[..., t_chunk]
    at_mul = jnp.concatenate([at_re, at_re], axis=-1)
    at_swp = jnp.concatenate([-at_im, at_im], axis=-1)
    return m.astype(BF16), p_mat.astype(BF16), q_mat.astype(BF16), d_tile, at_mul, at_swp


def _s5_local_kernel(x_ref, p_ref, e_ref):
    e_ref[...] = _dot(x_ref[...].astype(BF16), p_ref[...])


def _s5_out_kernel(x_ref, s_ref, m_ref, q_ref, d_ref, y_ref):
    x = x_ref[...]
    y_ref[...] = _dot(x.astype(BF16), m_ref[...]) + _dot(s_ref[...].astype(BF16), q_ref[...]) + x * d_ref[...]


def _s5_scan_kernel(ef_ref, eb_ref, mul_ref, swp_ref, sf_ref, sb_ref, *, n_batch, lat_chunks, ctx_chunks):
    half = ef_ref.shape[2] // 2
    mul_f, mul_b = mul_ref[0], mul_ref[1]
    swp_f, swp_b = swp_ref[0], swp_ref[1]
    zero = jnp.zeros(ef_ref.shape[1:], F32)

    def step(c_f, c_b, carry):
        s_f, s_b = carry
        sf_ref[c_f] = s_f
        sb_ref[c_b] = s_b
        s_f = s_f * mul_f + pltpu.roll(s_f, half, axis=1) * swp_f + ef_ref[c_f]
        s_b = s_b * mul_b + pltpu.roll(s_b, half, axis=1) * swp_b + eb_ref[c_b]
        return s_f, s_b

    for b in range(n_batch):
        ctx0 = n_batch * lat_chunks + b * ctx_chunks
        lat0 = b * lat_chunks
        carry = lax.fori_loop(0, ctx_chunks, lambda i, cr: step(ctx0 + i, ctx0 + ctx_chunks - 1 - i, cr), (zero, zero))
        lax.fori_loop(0, lat_chunks, lambda i, cr: step(lat0 + i, lat0 + lat_chunks - 1 - i, cr), carry)


def s5_mixer(proj, mats, n_rows, n_batch, seq, n_ctx):
    m_mat, p_mat, q_mat, d_tile, at_mul, at_swp = mats
    n_g, th, _ = m_mat.shape
    n_h = SSM_GROUP
    t_chunk = th // n_h
    p4 = p_mat.shape[2]
    n_p2 = p4 // 2
    width = n_g * n_h
    n_chunks = n_rows // t_chunk
    u = proj[:, :width]
    x = u.reshape(n_chunks, t_chunk, n_g, n_h).transpose(2, 0, 1, 3).reshape(n_g, n_chunks, th)
    e = pl.pallas_call(
        _s5_local_kernel,
        grid=(n_g,),
        in_specs=[pl.BlockSpec((None, n_chunks, th), lambda g: (g, 0, 0)),
                  pl.BlockSpec((None, th, p4), lambda g: (g, 0, 0))],
        out_specs=pl.BlockSpec((n_chunks, p4), lambda g: (0, g)),
        out_shape=jax.ShapeDtypeStruct((n_chunks, n_g * p4), F32),
        compiler_params=_cparams("parallel"),
        name="s5_local_state",
    )(x, p_mat)
    e = e.reshape(n_chunks, n_g, 2, n_p2)
    gb = 16
    spec = pl.BlockSpec((n_chunks, gb, n_p2), lambda g: (0, g, 0))
    tab = pl.BlockSpec((2, gb, n_p2), lambda g: (0, g, 0))
    s_f, s_b = pl.pallas_call(
        functools.partial(_s5_scan_kernel, n_batch=n_batch, lat_chunks=seq // t_chunk, ctx_chunks=n_ctx // t_chunk),
        grid=(n_g // gb,),
        in_specs=[spec, spec, tab, tab],
        out_specs=[spec, spec],
        out_shape=[jax.ShapeDtypeStruct((n_chunks, n_g, n_p2), F32)] * 2,
        compiler_params=_cparams("parallel"),
        name="s5_chunk_scan",
    )(e[:, :, 0], e[:, :, 1], at_mul, at_swp)
    s_in = jnp.stack([s_f, s_b], axis=2).reshape(n_chunks, n_g * p4)
    y = pl.pallas_call(
        _s5_out_kernel,
        grid=(n_g,),
        in_specs=[pl.BlockSpec((None, n_chunks, th), lambda g: (g, 0, 0)),
                  pl.BlockSpec((n_chunks, p4), lambda g: (0, g)),
                  pl.BlockSpec((None, th, th), lambda g: (g, 0, 0)),
                  pl.BlockSpec((None, p4, th), lambda g: (g, 0, 0)),
                  pl.BlockSpec((None, 1, th), lambda g: (g, 0, 0))],
        out_specs=pl.BlockSpec((None, n_chunks, th), lambda g: (g, 0, 0)),
        out_shape=jax.ShapeDtypeStruct((n_g, n_chunks, th), F32),
        compiler_params=_cparams("parallel"),
        name="s5_output",
    )(x, s_in, m_mat, q_mat, d_tile)
    return y.reshape(n_g, n_chunks, t_chunk, n_h).transpose(1, 2, 0, 3).reshape(n_rows, width)


def _glu_kernel(y_ref, w_ref, b_ref, o_ref):
    y = y_ref[...]
    gy = 0.5 * y * (1.0 + jnp.tanh(math.sqrt(2.0 / math.pi) * (y + 0.044715 * (y * y * y))))
    z = _dot(gy.astype(BF16), w_ref[...].astype(BF16)) + b_ref[...]
    o_ref[...] = (gy * _sigmoid(z)).astype(o_ref.dtype)


def s5_glu(y, w_glu, b_glu, layer, n_rows):
    width = y.shape[1]
    tm = _row_tile(n_rows, ROW_TILE)
    return pl.pallas_call(
        _glu_kernel,
        grid=(n_rows // tm,),
        in_specs=[pl.BlockSpec((tm, width), lambda i: (i, 0)),
                  pl.BlockSpec((None, width, width), lambda i: (layer, 0, 0)),
                  pl.BlockSpec((None, 1, width), lambda i: (layer, 0, 0))],
        out_specs=pl.BlockSpec((tm, width), lambda i: (i, 0)),
        out_shape=jax.ShapeDtypeStruct((n_rows, width), BF16),
        compiler_params=_cparams("parallel"),
        name="s5_glu",
    )(y, w_glu, b_glu.reshape(b_glu.shape[0], 1, width))


def _merge_kernel(os_ref, ow_ref, od_ref, ws_ref, ww_ref, wd_ref, gs_ref, gw_ref, gd_ref, o_ref):
    m = (_sigmoid(gs_ref[...]) * _dot(os_ref[...], ws_ref[...].astype(BF16))
         + _sigmoid(gw_ref[...]) * _dot(ow_ref[...], ww_ref[...].astype(BF16))
         + _sigmoid(gd_ref[...]) * _dot(od_ref[...], wd_ref[...].astype(BF16)))
    o_ref[...] = m.astype(o_ref.dtype)


def gated_merge(o_ssm, o_win, o_diff, w_s, w_w, w_d, proj, gate_col0, layer, n_rows, d):
    tm = _row_tile(n_rows, 1088)
    tn = 256
    assert gate_col0 % tn == 0 and d % tn == 0
    g0 = gate_col0 // tn
    nd = d // tn

    def branch(arr):
        return pl.BlockSpec((tm, arr.shape[1]), lambda i, j: (i, 0))

    def weight(w):
        return pl.BlockSpec((None, w.shape[1], tn), lambda i, j: (layer, 0, j))

    def gate(k):
        return pl.BlockSpec((tm, tn), lambda i, j: (i, g0 + k * nd + j))

    return pl.pallas_call(
        _merge_kernel,
        grid=(n_rows // tm, nd),
        in_specs=[branch(o_ssm), branch(o_win), branch(o_diff), weight(w_s), weight(w_w), weight(w_d),
                  gate(0), gate(1), gate(2)],
        out_specs=pl.BlockSpec((tm, tn), lambda i, j: (i, j)),
        out_shape=jax.ShapeDtypeStruct((n_rows, d), BF16),
        compiler_params=_cparams("parallel", "arbitrary"),
        name="gated_merge",
    )(o_ssm, o_win, o_diff, w_s, w_w, w_d, proj, proj, proj)


def _out_proj_kernel(m_ref, w_ref, x_ref, g_ref, o_ref):
    o_ref[...] = x_ref[...] + g_ref[0] * _dot(m_ref[...], w_ref[...].astype(BF16))


def out_proj_residual(m, w_out, x, gate, layer, n_rows, rows_lat, seq):
    d = x.shape[1]
    tm = _stream_tile(n_rows, rows_lat, seq)
    tn = min(512, d)
    n_batch = gate.shape[0] - 1
    idx = functools.partial(_stream_index, tm=tm, rows_lat=rows_lat, seq=seq, n_batch=n_batch)
    return pl.pallas_call(
        _out_proj_kernel,
        grid=(n_rows // tm, d // tn),
        in_specs=[pl.BlockSpec((tm, m.shape[1]), lambda i, j: (i, 0)),
                  pl.BlockSpec((None, m.shape[1], tn), lambda i, j: (layer, 0, j)),
                  pl.BlockSpec((tm, tn), lambda i, j: (i, j)),
                  pl.BlockSpec((1, 1, tn), lambda i, j: (idx(i), 0, j))],
        out_specs=pl.BlockSpec((tm, tn), lambda i, j: (i, j)),
        out_shape=jax.ShapeDtypeStruct((n_rows, d), F32),
        compiler_params=_cparams("parallel", "arbitrary"),
        name="out_proj_residual",
    )(m, w_out, x, gate)


def _expert_up_kernel(be_ref, x_ref, wg_ref, wl_ref, bg_ref, bl_ref, o_ref, wg_s, wl_s):
    r = pl.program_id(1)
    prev = be_ref[jnp.maximum(r - 1, 0)]

    @pl.when((r == 0) | (be_ref[r] != prev))
    def _():
        wg_s[...] = wg_ref[...].astype(BF16)
        wl_s[...] = wl_ref[...].astype(BF16)

    x = x_ref[...]
    glu = jnp.minimum(_dot(x, wg_s[...]) + bg_ref[...], SWIGLU_LIMIT)
    lin = jnp.clip(_dot(x, wl_s[...]) + bl_ref[...], -SWIGLU_LIMIT, SWIGLU_LIMIT)
    o_ref[...] = (glu * _sigmoid(SWIGLU_ALPHA * glu) * (lin + 1.0)).astype(o_ref.dtype)


def _expert_down_kernel(be_ref, a_ref, w_ref, b_ref, o_ref, w_s):
    r = pl.program_id(1)
    prev = be_ref[jnp.maximum(r - 1, 0)]

    @pl.when((r == 0) | (be_ref[r] != prev))
    def _():
        w_s[...] = w_ref[...].astype(BF16)

    o_ref[...] = _dot(a_ref[...], w_s[...]) + b_ref[...]


def expert_ffn(block_exp, xb, w1, b1, w2, b2, layer):
    n_rows, d = xb.shape
    n_exp, _, f2 = w1.shape[1:]
    f = f2 // 2
    n_blocks = n_rows // MOE_ROWS
    tf = 512
    nf = f // tf
    b1r = b1.reshape(b1.shape[0], n_exp, 1, f2)
    b2r = b2.reshape(b2.shape[0], n_exp, 1, d)
    act = pl.pallas_call(
        _expert_up_kernel,
        grid_spec=pltpu.PrefetchScalarGridSpec(
            num_scalar_prefetch=1,
            grid=(nf, n_blocks),
            in_specs=[
                pl.BlockSpec((MOE_ROWS, d), lambda j, r, be: (r, 0)),
                pl.BlockSpec((None, None, d, tf), lambda j, r, be: (layer, be[r], 0, j)),
                pl.BlockSpec((None, None, d, tf), lambda j, r, be: (layer, be[r], 0, nf + j)),
                pl.BlockSpec((None, None, 1, tf), lambda j, r, be: (layer, be[r], 0, j)),
                pl.BlockSpec((None, None, 1, tf), lambda j, r, be: (layer, be[r], 0, nf + j)),
            ],
            out_specs=pl.BlockSpec((MOE_ROWS, tf), lambda j, r, be: (r, j)),
            scratch_shapes=[pltpu.VMEM((d, tf), BF16), pltpu.VMEM((d, tf), BF16)],
        ),
        out_shape=jax.ShapeDtypeStruct((n_rows, f), BF16),
        compiler_params=_cparams("arbitrary", "arbitrary"),
        name="expert_up",
    )(block_exp, xb, w1, w1, b1r, b1r)
    tn = 512
    return pl.pallas_call(
        _expert_down_kernel,
        grid_spec=pltpu.PrefetchScalarGridSpec(
            num_scalar_prefetch=1,
            grid=(d // tn, n_blocks),
            in_specs=[
                pl.BlockSpec((MOE_ROWS, f), lambda j, r, be: (r, 0)),
                pl.BlockSpec((None, None, f, tn), lambda j, r, be: (layer, be[r], 0, j)),
                pl.BlockSpec((None, None, 1, tn), lambda j, r, be: (layer, be[r], 0, j)),
            ],
            out_specs=pl.BlockSpec((MOE_ROWS, tn), lambda j, r, be: (r, j)),
            scratch_shapes=[pltpu.VMEM((f, tn), BF16)],
        ),
        out_shape=jax.ShapeDtypeStruct((n_rows, d), F32),
        compiler_params=_cparams("arbitrary", "arbitrary"),
        name="expert_down",
    )(block_exp, act, w2, b2r)


def moe_ffn(h, logits, w1, b1, w2, b2, layer):
    n_tok, d = h.shape
    n_exp = w1.shape[1]
    top_val, top_idx = lax.top_k(logits, TOP_K)
    gates = jax.nn.softmax(top_val, axis=-1)
    n_assign = n_tok * TOP_K
    flat_e = top_idx.reshape(-1)
    flat_t = jnp.arange(n_assign, dtype=jnp.int32) // TOP_K
    order = jnp.argsort(flat_e)
    se, st = flat_e[order], flat_t[order]
    counts = jnp.zeros((n_exp,), jnp.int32).at[flat_e].add(1)
    padded = (counts + MOE_ROWS - 1) // MOE_ROWS * MOE_ROWS
    start = jnp.cumsum(counts) - counts
    pend = jnp.cumsum(padded)
    pstart = pend - padded
    dest = pstart[se] + jnp.arange(n_assign, dtype=jnp.int32) - start[se]
    n_blocks = -(-n_assign // MOE_ROWS) + n_exp
    n_rows = n_blocks * MOE_ROWS
    row_tok = jnp.zeros((n_rows,), jnp.int32).at[dest].set(st)
    block_exp = jnp.minimum(
        jnp.searchsorted(pend, jnp.arange(n_blocks, dtype=jnp.int32) * MOE_ROWS, side="right"), n_exp - 1
    ).astype(jnp.int32)
    pos = jnp.zeros((n_assign,), jnp.int32).at[order].set(dest).reshape(n_tok, TOP_K)
    xb = jnp.take(h, row_tok, axis=0)
    yb = expert_ffn(block_exp, xb, w1, b1, w2, b2, layer)
    return jnp.sum(jnp.take(yb, pos, axis=0) * gates[..., None], axis=1)


def _layer(i, x, c_all, p, n_batch, seq, n_ctx, tables, with_ctx):
    d = x.shape[1]
    rows_lat = n_batch * seq
    rows_all = rows_lat + n_batch * n_ctx
    n_out = rows_all if with_ctx else rows_lat
    n_stream = n_batch + 1

    mod = matmul(c_all, p["w_mod"], i, c_all.shape[0], c_all.shape[0], 512, F32, pre="silu", name="modulation")
    mod = (mod[:n_stream] + p["b_mod"][i]).reshape(n_stream, 6, 1, d)
    sh1, sc1, g1, sh2, sc2, g2 = (mod[:, k] for k in range(6))

    h1 = norm_mod(x, p["g_mix"][i], sc1, sh1, rows_all, rows_lat, seq, BF16)
    in_width = p["w_in"].shape[2]
    proj = matmul(h1, p["w_in"], i, rows_all, _row_tile(rows_all, 1088), 256, F32, name="in_proj")

    n_g = p["ssm_lambda_re"].shape[2]
    ssm_w = n_g * SSM_GROUP
    n_wh = p["win_sink"].shape[1]
    win_q = n_wh * WIN_HEAD_DIM
    win_kv = WIN_KV_HEADS * WIN_HEAD_DIM
    n_dh = (in_width - ssm_w - win_q - 2 * win_kv - 3 * d) // (2 * 2 * DIFF_QK_DIM + DIFF_V_DIM)
    diff_w = n_dh * 2 * DIFF_QK_DIM
    gate_col0 = ssm_w + win_q + 2 * win_kv + 3 * diff_w

    mats = s5_matrices(p["ssm_lambda_re"][i], p["ssm_lambda_im"][i], p["ssm_log_dt"][i], p["ssm_b_re"][i],
                       p["ssm_b_im"][i], p["ssm_c_re"][i], p["ssm_c_im"][i], p["ssm_d"][i], S5_CHUNK)
    y = s5_mixer(proj, mats, rows_all, n_batch, seq, n_ctx)
    o_ssm = s5_glu(y, p["w_glu"], p["b_glu"], i, n_out)

    segments = [("qw", win_q, True, WIN_HEAD_DIM ** -0.5), ("kw", win_kv, True, 1.0), ("vw", win_kv, False, 1.0),
                ("qd", diff_w, True, DIFF_QK_DIM ** -0.5), ("kd", diff_w, True, 1.0), ("vd", diff_w, False, 1.0)]
    lat, cx = {}, {}
    col = ssm_w
    for name, width, rope, scale in segments:
        lat[name] = attention_operand(proj, 0, rows_lat, col, width, rope, scale, tables, seq)
        cx[name] = attention_operand(proj, rows_lat, n_batch * n_ctx, col, width, False, scale, tables, seq)
        col += width

    sink = p["win_sink"][i].astype(F32)
    o_win = window_attention(sink, lat["qw"], lat["kw"], lat["vw"], cx["kw"], cx["vw"], n_batch, seq, n_ctx, n_wh, True)

    lam_p = p["diff_lambda"][i].astype(F32)
    lambda_init = 0.8 - 0.6 * math.exp(-0.3 * i)
    lam = (jnp.exp(jnp.sum(lam_p[0] * lam_p[1])) - jnp.exp(jnp.sum(lam_p[2] * lam_p[3])) + lambda_init).reshape(1)
    gain = p["diff_subln"][i].astype(F32).reshape(1, DIFF_V_DIM)
    o_diff = diff_attention(lam, gain, lat["qd"], lat["kd"], lat["vd"], cx["kd"], cx["vd"], n_batch, seq, seq, n_ctx,
                            n_dh, 1.0 - lambda_init)
    if with_ctx:
        o_win_c = window_attention(sink, cx["qw"], None, None, cx["kw"], cx["vw"], n_batch, n_ctx, n_ctx, n_wh, False)
        o_diff_c = diff_attention(lam, gain, cx["qd"], None, None, cx["kd"], cx["vd"], n_batch, n_ctx, seq, n_ctx,
                                  n_dh, 1.0 - lambda_init)
        o_win = jnp.concatenate([o_win, o_win_c], axis=0)
        o_diff = jnp.concatenate([o_diff, o_diff_c], axis=0)

    merged = gated_merge(o_ssm, o_win, o_diff, p["w_branch_ssm"], p["w_branch_win"], p["w_branch_diff"], proj,
                         gate_col0, i, n_out, d)
    x = out_proj_residual(merged, p["w_out"], x, g1, i, n_out, rows_lat, seq)

    h2, logits = norm_mod(x, p["g_ffn"][i], sc2, sh2, n_out, rows_lat, seq, BF16,
                          router=(p["w_router"][i], p["b_router"][i]))
    f = moe_ffn(h2, logits, p["w_exp1"], p["b_exp1"], p["w_exp2"], p["b_exp2"], i)
    out_lat = (x[:rows_lat].reshape(n_batch, seq, d) + g2[:n_batch] * f[:rows_lat].reshape(n_batch, seq, d)).reshape(rows_lat, d)
    if not with_ctx:
        return out_lat
    return jnp.concatenate([out_lat, x[rows_lat:] + g2[n_batch] * f[rows_lat:]], axis=0)


def kernel(x, c, ctx, c_ctx, w_mod, b_mod, g_mix, g_ffn, w_in, ssm_lambda_re, ssm_lambda_im, ssm_log_dt, ssm_b_re, ssm_b_im, ssm_c_re, ssm_c_im, ssm_d, w_glu, b_glu, win_sink, diff_lambda, diff_subln, w_branch_ssm, w_branch_win, w_branch_diff, w_out, w_router, b_router, w_exp1, b_exp1, w_exp2, b_exp2, g_final):
    n_batch, seq, d = x.shape
    n_ctx = ctx.shape[1]
    depth = w_mod.shape[0]
    p = dict(w_mod=w_mod, b_mod=b_mod, g_mix=g_mix, g_ffn=g_ffn, w_in=w_in, ssm_lambda_re=ssm_lambda_re,
             ssm_lambda_im=ssm_lambda_im, ssm_log_dt=ssm_log_dt, ssm_b_re=ssm_b_re, ssm_b_im=ssm_b_im,
             ssm_c_re=ssm_c_re, ssm_c_im=ssm_c_im, ssm_d=ssm_d, w_glu=w_glu, b_glu=b_glu, win_sink=win_sink,
             diff_lambda=diff_lambda, diff_subln=diff_subln, w_branch_ssm=w_branch_ssm, w_branch_win=w_branch_win,
             w_branch_diff=w_branch_diff, w_out=w_out, w_router=w_router, b_router=b_router, w_exp1=w_exp1,
             b_exp1=b_exp1, w_exp2=w_exp2, b_exp2=b_exp2)
    rows_lat = n_batch * seq
    rows = jnp.concatenate([x.reshape(rows_lat, d), ctx.reshape(n_batch * n_ctx, d)], axis=0)
    c_all = jnp.concatenate([c, c_ctx[None, :], jnp.zeros((8 - (n_batch + 1) % 8, d), F32)], axis=0)
    tables = rope_tables(seq, WIN_HEAD_DIM)
    for i in range(depth):
        with_ctx = i < depth - 1
        new = _layer(i, rows, c_all, p, n_batch, seq, n_ctx, tables, with_ctx)
        rows = new if with_ctx else jnp.concatenate([new, rows[rows_lat:]], axis=0)
    no_mod = jnp.zeros((n_batch + 1, 1, d), F32)
    out = norm_mod(rows, g_final, no_mod, no_mod, rows_lat, rows_lat, seq, F32)
    return out.reshape(n_batch, seq, d)
```

```python
import functools
import math

import jax
import jax.numpy as jnp
import numpy as np
from jax import lax
from jax.experimental import pallas as pl
from jax.experimental.pallas import tpu as pltpu

F32 = jnp.float32
BF16 = jnp.bfloat16

GRID_W = 64
SSM_GROUP = 16
WIN_KV_HEADS = 2
WIN_HEAD_DIM = 64
WINDOW = 128
DIFF_QK_DIM = 64
DIFF_V_DIM = 2 * DIFF_QK_DIM
ATTN_BLOCK = 128
ROPE_BASE = 10000.0
TOP_K = 4
SWIGLU_LIMIT = 7.0
SWIGLU_ALPHA = 1.702
NORM_EPS = 1e-6
SUBLN_EPS = 1e-5
NEG_INF = -1e30

LANE = 128
VMEM_LIMIT_BYTES = 56 * 1024 * 1024
S5_CHUNK = 32
MOE_ROWS = 256
ROW_TILE = 512


def _cparams(*sem):
    return pltpu.CompilerParams(dimension_semantics=sem, vmem_limit_bytes=VMEM_LIMIT_BYTES)


def _dot(a, b):
    return jnp.dot(a, b, preferred_element_type=F32)


def _dot_nt(a, b):
    return lax.dot_general(a, b, (((1,), (1,)), ((), ())), preferred_element_type=F32)


def _sigmoid(x):
    return 1.0 / (1.0 + jnp.exp(-x))


def _row_tile(n_rows, cap):
    best = 16
    for t in range(16, cap + 1, 16):
        if n_rows % t == 0:
            best = t
    return best


def _norm_mod_kernel(x_ref, g_ref, sc_ref, sh_ref, o_ref):
    x = x_ref[...]
    y = x * lax.rsqrt(jnp.mean(x * x, axis=-1, keepdims=True) + NORM_EPS) * g_ref[...]
    o_ref[...] = (y * (1.0 + sc_ref[0]) + sh_ref[0]).astype(o_ref.dtype)


def _norm_router_kernel(x_ref, g_ref, sc_ref, sh_ref, wr_ref, br_ref, o_ref, lg_ref):
    x = x_ref[...]
    y = x * lax.rsqrt(jnp.mean(x * x, axis=-1, keepdims=True) + NORM_EPS) * g_ref[...]
    h = y * (1.0 + sc_ref[0]) + sh_ref[0]
    o_ref[...] = _pack_halves(h)
    lg_ref[...] = jnp.dot(h, wr_ref[...], preferred_element_type=F32, precision=lax.Precision.HIGHEST) + br_ref[...]


def _stream_tile(n_rows, rows_lat, seq):
    tm = math.gcd(ROW_TILE, seq)
    return math.gcd(tm, n_rows - rows_lat) if n_rows > rows_lat else tm


def _stream_index(i, tm, rows_lat, seq, n_batch):
    r = i * tm
    return jnp.where(r >= rows_lat, n_batch, r // seq)


def norm_mod(x, gain, sc, sh, n_rows, rows_lat, seq, out_dtype, router=None):
    d = x.shape[1]
    tm = _stream_tile(n_rows, rows_lat, seq)
    n_batch = sc.shape[0] - 1
    idx = functools.partial(_stream_index, tm=tm, rows_lat=rows_lat, seq=seq, n_batch=n_batch)
    in_specs = [
        pl.BlockSpec((tm, d), lambda i: (i, 0)),
        pl.BlockSpec((1, d), lambda i: (0, 0)),
        pl.BlockSpec((1, 1, d), lambda i: (idx(i), 0, 0)),
        pl.BlockSpec((1, 1, d), lambda i: (idx(i), 0, 0)),
    ]
    if router is None:
        return pl.pallas_call(
            _norm_mod_kernel,
            grid=(n_rows // tm,),
            in_specs=in_specs,
            out_specs=pl.BlockSpec((tm, d), lambda i: (i, 0)),
            out_shape=jax.ShapeDtypeStruct((n_rows, d), out_dtype),
            compiler_params=_cparams("parallel"),
            name="norm_mod",
        )(x, gain.reshape(1, d), sc, sh)
    w_router, b_router = router
    n_exp = w_router.shape[1]
    return pl.pallas_call(
        _norm_router_kernel,
        grid=(n_rows // tm,),
        in_specs=in_specs + [
            pl.BlockSpec((d, n_exp), lambda i: (0, 0)),
            pl.BlockSpec((1, n_exp), lambda i: (0, 0)),
        ],
        out_specs=[pl.BlockSpec((tm, d // 2), lambda i: (i, 0)), pl.BlockSpec((tm, n_exp), lambda i: (i, 0))],
        out_shape=[jax.ShapeDtypeStruct((n_rows, d // 2), jnp.uint32), jax.ShapeDtypeStruct((n_rows, n_exp), F32)],
        compiler_params=_cparams("parallel"),
        name="norm_router",
    )(x, gain.reshape(1, d), sc, sh, w_router, b_router.reshape(1, n_exp))


def _mm_kernel(x_ref, w_ref, o_ref, *, pre):
    x = x_ref[...]
    if pre == "silu":
        x = x * _sigmoid(x)
    o_ref[...] = _dot(x.astype(BF16), w_ref[...].astype(BF16)).astype(o_ref.dtype)


def matmul(x, w_stack, layer, n_rows, tm, tn, out_dtype, pre=None, name="matmul"):
    k = x.shape[1]
    n = w_stack.shape[2]
    return pl.pallas_call(
        functools.partial(_mm_kernel, pre=pre),
        grid=(n_rows // tm, n // tn),
        in_specs=[
            pl.BlockSpec((tm, k), lambda i, j: (i, 0)),
            pl.BlockSpec((None, k, tn), lambda i, j: (layer, 0, j)),
        ],
        out_specs=pl.BlockSpec((tm, tn), lambda i, j: (i, j)),
        out_shape=jax.ShapeDtypeStruct((n_rows, n), out_dtype),
        compiler_params=_cparams("parallel", "arbitrary"),
        name=name,
    )(x, w_stack)


def _in_proj_kernel(x_ref, w_ref, *refs, mode, scale):
    o_ref = refs[-1]
    acc = _dot(x_ref[...], w_ref[...])
    if mode == "sigmoid":
        o_ref[...] = _sigmoid(acc).astype(o_ref.dtype)
        return
    if mode == "plain":
        o_ref[...] = (acc if scale == 1.0 else acc * scale).astype(o_ref.dtype)
        return
    cos_ref, sin_ref = refs[0], refs[1]
    lane = lax.broadcasted_iota(jnp.int32, (1, LANE), 1)
    first_half = (lane % 32) < 16
    cos = cos_ref[...] if scale == 1.0 else cos_ref[...] * scale
    sin = sin_ref[...] if scale == 1.0 else sin_ref[...] * scale
    for c in range(acc.shape[1] // LANE):
        x = acc[:, c * LANE:(c + 1) * LANE]
        partner = jnp.where(first_half, pltpu.roll(x, LANE - 16, axis=1), pltpu.roll(x, 16, axis=1))
        o_ref[:, c * LANE:(c + 1) * LANE] = (x * cos + partner * sin).astype(o_ref.dtype)


def in_proj(h, w, col0, width, mode, scale, out_dtype, tables, rows_lat, seq):
    n_rows, k = h.shape
    tm = _stream_tile(n_rows, rows_lat, seq)
    tn = math.gcd(512, width)
    assert col0 % tn == 0
    in_specs = [pl.BlockSpec((tm, k), lambda i, j: (i, 0)),
                pl.BlockSpec((k, tn), lambda i, j: (0, col0 // tn + j))]
    args = [h, w]
    if mode == "rope":
        cos_t, sin_t = tables
        tiles_per_seq = seq // tm
        lat_tiles = rows_lat // tm
        cos_x = jnp.concatenate([cos_t, jnp.ones((tm, LANE), F32)], axis=0)
        sin_x = jnp.concatenate([sin_t, jnp.zeros((tm, LANE), F32)], axis=0)
        tab = pl.BlockSpec((tm, LANE), lambda i, j: (jnp.where(i < lat_tiles, i % tiles_per_seq, tiles_per_seq), 0))
        in_specs += [tab, tab]
        args += [cos_x, sin_x]
    return pl.pallas_call(
        functools.partial(_in_proj_kernel, mode=mode, scale=scale),
        grid=(n_rows // tm, width // tn),
        in_specs=in_specs,
        out_specs=pl.BlockSpec((tm, tn), lambda i, j: (i, j)),
        out_shape=jax.ShapeDtypeStruct((n_rows, width), out_dtype),
        compiler_params=_cparams("parallel", "arbitrary"),
        name="in_proj_" + mode,
    )(*args)


def rope_tables(seq, head_dim):
    assert head_dim == 64
    rows = seq // GRID_W
    r, col = jnp.meshgrid(jnp.arange(rows, dtype=F32), jnp.arange(GRID_W, dtype=F32), indexing="ij")
    half = head_dim // 2
    inv_freq = ROPE_BASE ** (-jnp.arange(0, half, 2, dtype=F32) / half)
    ang_r = r.reshape(-1)[:, None] * inv_freq[None, :]
    ang_c = col.reshape(-1)[:, None] * inv_freq[None, :]
    cos = jnp.concatenate([jnp.cos(ang_r), jnp.cos(ang_r), jnp.cos(ang_c), jnp.cos(ang_c)], axis=-1)
    sin = jnp.concatenate([-jnp.sin(ang_r), jnp.sin(ang_r), -jnp.sin(ang_c), jnp.sin(ang_c)], axis=-1)
    return jnp.tile(cos, (1, LANE // head_dim)), jnp.tile(sin, (1, LANE // head_dim))


def _win_attn_kernel(sink_ref, q_ref, *refs, n_heads, band, seq):
    if band:
        kp_ref, kc_ref, kn_ref, vp_ref, vc_ref, vn_ref, kx_ref, vx_ref, o_ref = refs
    else:
        kx_ref, vx_ref, o_ref = refs
    blk = q_ref.shape[0]
    dh = WIN_HEAD_DIM
    grp = n_heads // WIN_KV_HEADS
    lane = lax.broadcasted_iota(jnp.int32, (1, LANE), 1)
    kx = kx_ref[...]
    vx = vx_ref[...]
    rows = grp * blk
    if band:
        n = pl.program_id(1)
        k_loc = jnp.concatenate([kp_ref[...], kc_ref[...], kn_ref[...]], axis=0)
        v_loc = jnp.concatenate([vp_ref[...], vc_ref[...], vn_ref[...]], axis=0)
        qpos = n * blk + (lax.broadcasted_iota(jnp.int32, (rows, 3 * blk), 0) & (blk - 1))
        kpos = (n - 1) * blk + lax.broadcasted_iota(jnp.int32, (rows, 3 * blk), 1)
        mask = (jnp.abs(qpos - kpos) <= WINDOW) & (kpos >= 0) & (kpos < seq)
    outs = [None] * n_heads
    for kvh in range(WIN_KV_HEADS):
        keep = (lane >= kvh * dh) & (lane < (kvh + 1) * dh)
        q_parts, sink_parts = [], []
        for g in range(grp):
            h = kvh * grp + g
            c = (h * dh) // LANE
            qc = q_ref[:, c * LANE:(c + 1) * LANE]
            if (h * dh) % LANE != kvh * dh:
                qc = jnp.concatenate([qc[:, dh:], qc[:, :dh]], axis=1)
            q_parts.append(jnp.where(keep, qc, jnp.zeros_like(qc)))
            sink_parts.append(jnp.full((blk, 1), sink_ref[h], F32))
        qs = jnp.concatenate(q_parts, axis=0)
        sink = jnp.concatenate(sink_parts, axis=0)
        s_x = _dot_nt(qs, kx)
        m = jnp.maximum(jnp.max(s_x, axis=-1, keepdims=True), sink)
        if band:
            s_l = jnp.where(mask, _dot_nt(qs, k_loc), NEG_INF)
            m = jnp.maximum(m, jnp.max(s_l, axis=-1, keepdims=True))
            e_l = jnp.exp(s_l - m)
        e_x = jnp.exp(s_x - m)
        den = jnp.sum(e_x, axis=-1, keepdims=True) + jnp.exp(sink - m)
        if band:
            den = den + jnp.sum(e_l, axis=-1, keepdims=True)
        inv = 1.0 / den
        o = _dot((e_x * inv).astype(BF16), vx)
        if band:
            o = o + _dot((e_l * inv).astype(BF16), v_loc)
        for g in range(grp):
            outs[kvh * grp + g] = o[g * blk:(g + 1) * blk, kvh * dh:(kvh + 1) * dh]
    o_ref[...] = jnp.concatenate(outs, axis=1).astype(o_ref.dtype)


def window_attention(sink, q, k, v, n_batch, seq, n_ctx, n_heads, rows_lat, ctx_queries):
    blk = ATTN_BLOCK
    seq_q = n_ctx if ctx_queries else seq
    nb = seq_q // blk
    q0 = rows_lat // blk if ctx_queries else 0
    x0 = rows_lat // n_ctx
    qw = n_heads * WIN_HEAD_DIM
    kvw = WIN_KV_HEADS * WIN_HEAD_DIM
    in_specs = [
        pl.BlockSpec(memory_space=pltpu.SMEM),
        pl.BlockSpec((blk, qw), lambda b, n: (q0 + b * nb + n, 0)),
    ]
    args = [sink, q]
    if not ctx_queries:
        band_specs = [
            pl.BlockSpec((blk, kvw), lambda b, n: (b * nb + jnp.maximum(n - 1, 0), 0)),
            pl.BlockSpec((blk, kvw), lambda b, n: (b * nb + n, 0)),
            pl.BlockSpec((blk, kvw), lambda b, n: (b * nb + jnp.minimum(n + 1, nb - 1), 0)),
        ]
        in_specs += band_specs + band_specs
        args += [k, k, k, v, v, v]
    in_specs += [pl.BlockSpec((n_ctx, kvw), lambda b, n: (x0 + b, 0))] * 2
    args += [k, v]
    return pl.pallas_call(
        functools.partial(_win_attn_kernel, n_heads=n_heads, band=not ctx_queries, seq=seq),
        grid=(n_batch, nb),
        in_specs=in_specs,
        out_specs=pl.BlockSpec((blk, qw), lambda b, n: (b * nb + n, 0)),
        out_shape=jax.ShapeDtypeStruct((n_batch * seq_q, qw), BF16),
        compiler_params=_cparams("parallel", "parallel"),
        name="context_window_attention" if ctx_queries else "window_attention",
    )(*args)


def _diff_attn_kernel(lam_ref, q_ref, gain_ref, *refs, with_lat, post_scale):
    if with_lat:
        kl_ref, vl_ref, kx_ref, vx_ref, o_ref = refs
    else:
        kx_ref, vx_ref, o_ref = refs
    lam = lam_ref[0]
    q = q_ref[...]
    lane = lax.broadcasted_iota(jnp.int32, (1, LANE), 1)
    zero = jnp.zeros_like(q)
    kx = kx_ref[...]

    def softmax_parts(qm):
        s_x = _dot_nt(qm, kx)
        m = jnp.max(s_x, axis=-1, keepdims=True)
        if with_lat:
            s_l = _dot_nt(qm, kl_ref[...])
            m = jnp.maximum(m, jnp.max(s_l, axis=-1, keepdims=True))
            e_l = jnp.exp(s_l - m)
        else:
            e_l = None
        e_x = jnp.exp(s_x - m)
        den = jnp.sum(e_x, axis=-1, keepdims=True)
        if with_lat:
            den = den + jnp.sum(e_l, axis=-1, keepdims=True)
        return e_l, e_x, 1.0 / den

    e1l, e1x, inv1 = softmax_parts(jnp.where(lane < DIFF_QK_DIM, q, zero))
    e2l, e2x, inv2 = softmax_parts(jnp.where(lane >= DIFF_QK_DIM, q, zero))
    w2 = lam * inv2
    o = _dot((e1x * inv1 - e2x * w2).astype(BF16), vx_ref[...])
    if with_lat:
        o = o + _dot((e1l * inv1 - e2l * w2).astype(BF16), vl_ref[...])
    o = o * lax.rsqrt(jnp.mean(o * o, axis=-1, keepdims=True) + SUBLN_EPS) * gain_ref[...]
    o_ref[...] = (o * post_scale).astype(o_ref.dtype)


def diff_attention(lam, gain, q, k, v, n_batch, seq, n_ctx, n_heads, rows_lat, ctx_queries, post_scale):
    seq_q = n_ctx if ctx_queries else seq
    tq = min(256, seq_q)
    nq = seq_q // tq
    q0 = rows_lat // tq if ctx_queries else 0
    x0 = rows_lat // n_ctx
    in_specs = [
        pl.BlockSpec(memory_space=pltpu.SMEM),
        pl.BlockSpec((tq, LANE), lambda b, h, i: (q0 + b * nq + i, h)),
        pl.BlockSpec((1, LANE), lambda b, h, i: (0, 0)),
    ]
    args = [lam, q, gain]
    if not ctx_queries:
        in_specs += [pl.BlockSpec((seq, LANE), lambda b, h, i: (b, h))] * 2
        args += [k, v]
    in_specs += [pl.BlockSpec((n_ctx, LANE), lambda b, h, i: (x0 + b, h))] * 2
    args += [k, v]
    return pl.pallas_call(
        functools.partial(_diff_attn_kernel, with_lat=not ctx_queries, post_scale=post_scale),
        grid=(n_batch, n_heads, nq),
        in_specs=in_specs,
        out_specs=pl.BlockSpec((tq, LANE), lambda b, h, i: (b * nq + i, h)),
        out_shape=jax.ShapeDtypeStruct((n_batch * seq_q, n_heads * LANE), BF16),
        compiler_params=_cparams("parallel", "parallel", "arbitrary"),
        name="context_diff_attention" if ctx_queries else "diff_attention",
    )(*args)


def s5_matrices(lam_re, lam_im, log_dt, b_re, b_im, c_re, c_im, d_skip, t_chunk):
    n_dir, n_g, n_p = lam_re.shape
    n_h = b_re.shape[-1]
    lam_re = jnp.minimum(lam_re.astype(F32), -1e-4)
    lam_im = lam_im.astype(F32)
    dt = jnp.exp(log_dt.astype(F32))[..., None]
    mag = jnp.exp(lam_re * dt)
    a_re = mag * jnp.cos(lam_im * dt)
    a_im = mag * jnp.sin(lam_im * dt)
    den = lam_re * lam_re + lam_im * lam_im
    k_re = ((a_re - 1.0) * lam_re + a_im * lam_im) / den
    k_im = (a_im * lam_re - (a_re - 1.0) * lam_im) / den
    b_re = b_re.astype(F32)
    b_im = b_im.astype(F32)
    bb_re = k_re[..., None] * b_re - k_im[..., None] * b_im
    bb_im = k_re[..., None] * b_im + k_im[..., None] * b_re
    c_re = c_re.astype(F32)
    c_im = c_im.astype(F32)
    ldt_re = lam_re * dt
    ldt_im = lam_im * dt
    steps = jnp.arange(t_chunk, dtype=F32)

    def power(d, t, p_last):
        lr = ldt_re[d][:, None, :] if p_last else ldt_re[d][:, :, None]
        li = ldt_im[d][:, None, :] if p_last else ldt_im[d][:, :, None]
        tt = t[None, :, None] if p_last else t[None, None, :]
        mag_t = jnp.exp(lr * tt)
        return mag_t * jnp.cos(li * tt), mag_t * jnp.sin(li * tt)

    def c_times_power(d, t):
        pr, pi = power(d, t, False)
        cr = jnp.swapaxes(c_re[d], 1, 2)[:, :, None, :]
        ci = jnp.swapaxes(c_im[d], 1, 2)[:, :, None, :]
        re = cr * pr[..., None] - ci * pi[..., None]
        im = cr * pi[..., None] + ci * pr[..., None]
        return jnp.concatenate([re, -im], axis=1).reshape(n_g, 2 * n_p, t_chunk * n_h)

    def b_times_power(d, t):
        pr, pi = power(d, t, True)
        br = jnp.swapaxes(bb_re[d], 1, 2)[:, None, :, :]
        bi = jnp.swapaxes(bb_im[d], 1, 2)[:, None, :, :]
        re = pr[:, :, None, :] * br - pi[:, :, None, :] * bi
        im = pr[:, :, None, :] * bi + pi[:, :, None, :] * br
        return jnp.concatenate([re, im], axis=-1).reshape(n_g, t_chunk * n_h, 2 * n_p)

    ca_lag = jnp.stack([c_times_power(0, steps), c_times_power(1, t_chunk - 1 - steps)])
    bbt = jnp.stack([jnp.concatenate([jnp.swapaxes(bb_re[d], 1, 2), jnp.swapaxes(bb_im[d], 1, 2)], axis=-1)
                     for d in range(2)])
    p_mat = jnp.concatenate([b_times_power(0, t_chunk - 1 - steps), b_times_power(1, steps)], axis=-1)
    q_mat = jnp.concatenate([c_times_power(0, steps + 1.0), c_times_power(1, t_chunk - steps)], axis=1)
    d_tile = jnp.tile(d_skip.astype(F32).reshape(n_g, 1, n_h), (1, 1, t_chunk))
    mag_t = jnp.exp(ldt_re * t_chunk)
    at_re, at_im = mag_t * jnp.cos(ldt_im * t_chunk), mag_t * jnp.sin(ldt_im * t_chunk)
    at_mul = jnp.concatenate([at_re, at_re], axis=-1)
    at_swp = jnp.concatenate([-at_im, at_im], axis=-1)
    return ca_lag, bbt, p_mat.astype(BF16), q_mat.astype(BF16), d_tile, at_mul, at_swp


def _s5_local_kernel(x_ref, p_ref, e_ref):
    e_ref[...] = _dot(x_ref[...].astype(BF16), p_ref[...])


def _s5_out_kernel(x_ref, s_ref, ca_ref, bbt_ref, q_ref, d_ref, y_ref, m_ref):
    n_h = bbt_ref.shape[1]
    th = m_ref.shape[0]
    t_chunk = th // n_h
    hi = lax.Precision.HIGHEST
    row_f = jnp.dot(bbt_ref[0], ca_ref[0], preferred_element_type=F32, precision=hi)
    row_b = jnp.dot(bbt_ref[1], ca_ref[1], preferred_element_type=F32, precision=hi)
    lane = lax.broadcasted_iota(jnp.int32, (n_h, th), 1)
    for j in range(t_chunk):
        fwd = jnp.where(lane >= n_h * j, pltpu.roll(row_f, n_h * j, axis=1), 0.0) if j else row_f
        back = n_h * (t_chunk - 1 - j)
        bwd = jnp.where(lane < n_h * (j + 1), pltpu.roll(row_b, th - back, axis=1), 0.0) if back else row_b
        m_ref[n_h * j:n_h * (j + 1), :] = (fwd + bwd).astype(m_ref.dtype)
    x = x_ref[...]
    y_ref[...] = _dot(x.astype(BF16), m_ref[...]) + _dot(s_ref[...].astype(BF16), q_ref[...]) + x * d_ref[...]


def _s5_scan_kernel(ef_ref, eb_ref, mul_ref, swp_ref, sf_ref, sb_ref, *, n_batch, lat_chunks, ctx_chunks):
    half = ef_ref.shape[2] // 2
    mul_f, mul_b = mul_ref[0], mul_ref[1]
    swp_f, swp_b = swp_ref[0], swp_ref[1]
    zero = jnp.zeros(ef_ref.shape[1:], F32)

    def step(c_f, c_b, carry):
        s_f, s_b = carry
        sf_ref[c_f] = s_f
        sb_ref[c_b] = s_b
        s_f = s_f * mul_f + pltpu.roll(s_f, half, axis=1) * swp_f + ef_ref[c_f]
        s_b = s_b * mul_b + pltpu.roll(s_b, half, axis=1) * swp_b + eb_ref[c_b]
        return s_f, s_b

    for b in range(n_batch):
        ctx0 = n_batch * lat_chunks + b * ctx_chunks
        lat0 = b * lat_chunks
        carry = lax.fori_loop(0, ctx_chunks, lambda i, cr: step(ctx0 + i, ctx0 + ctx_chunks - 1 - i, cr), (zero, zero))
        lax.fori_loop(0, lat_chunks, lambda i, cr: step(lat0 + i, lat0 + lat_chunks - 1 - i, cr), carry)


def s5_mixer(proj, mats, n_rows, n_batch, seq, n_ctx):
    ca_lag, bbt, p_mat, q_mat, d_tile, at_mul, at_swp = mats
    n_g, th, _ = p_mat.shape
    n_h = SSM_GROUP
    t_chunk = th // n_h
    p4 = p_mat.shape[2]
    n_p2 = p4 // 2
    width = n_g * n_h
    n_chunks = n_rows // t_chunk
    u = proj[:, :width]
    x = u.reshape(n_chunks, t_chunk, n_g, n_h).transpose(2, 0, 1, 3).reshape(n_g, n_chunks, th)
    e = pl.pallas_call(
        _s5_local_kernel,
        grid=(n_g,),
        in_specs=[pl.BlockSpec((None, n_chunks, th), lambda g: (g, 0, 0)),
                  pl.BlockSpec((None, th, p4), lambda g: (g, 0, 0))],
        out_specs=pl.BlockSpec((n_chunks, p4), lambda g: (0, g)),
        out_shape=jax.ShapeDtypeStruct((n_chunks, n_g * p4), F32),
        compiler_params=_cparams("parallel"),
        name="s5_local_state",
    )(x, p_mat)
    e = e.reshape(n_chunks, n_g, 2, n_p2)
    gb = 16
    spec = pl.BlockSpec((n_chunks, gb, n_p2), lambda g: (0, g, 0))
    tab = pl.BlockSpec((2, gb, n_p2), lambda g: (0, g, 0))
    s_f, s_b = pl.pallas_call(
        functools.partial(_s5_scan_kernel, n_batch=n_batch, lat_chunks=seq // t_chunk, ctx_chunks=n_ctx // t_chunk),
        grid=(n_g // gb,),
        in_specs=[spec, spec, tab, tab],
        out_specs=[spec, spec],
        out_shape=[jax.ShapeDtypeStruct((n_chunks, n_g, n_p2), F32)] * 2,
        compiler_params=_cparams("parallel"),
        name="s5_chunk_scan",
    )(e[:, :, 0], e[:, :, 1], at_mul, at_swp)
    s_in = jnp.stack([s_f, s_b], axis=2).reshape(n_chunks, n_g * p4)
    y = pl.pallas_call(
        _s5_out_kernel,
        grid=(n_g,),
        in_specs=[pl.BlockSpec((None, n_chunks, th), lambda g: (g, 0, 0)),
                  pl.BlockSpec((n_chunks, p4), lambda g: (0, g)),
                  pl.BlockSpec((2, None, p4 // 2, th), lambda g: (0, g, 0, 0)),
                  pl.BlockSpec((2, None, n_h, p4 // 2), lambda g: (0, g, 0, 0)),
                  pl.BlockSpec((None, p4, th), lambda g: (g, 0, 0)),
                  pl.BlockSpec((None, 1, th), lambda g: (g, 0, 0))],
        out_specs=pl.BlockSpec((None, n_chunks, th), lambda g: (g, 0, 0)),
        out_shape=jax.ShapeDtypeStruct((n_g, n_chunks, th), F32),
        scratch_shapes=[pltpu.VMEM((th, th), BF16)],
        compiler_params=_cparams("parallel"),
        name="s5_output",
    )(x, s_in, ca_lag, bbt, q_mat, d_tile)
    return y.reshape(n_g, n_chunks, t_chunk, n_h).transpose(1, 2, 0, 3).reshape(n_rows, width)


def _glu_kernel(y_ref, w_ref, b_ref, o_ref):
    y = y_ref[...]
    gy = 0.5 * y * (1.0 + jnp.tanh(math.sqrt(2.0 / math.pi) * (y + 0.044715 * (y * y * y))))
    z = _dot(gy.astype(BF16), w_ref[...].astype(BF16)) + b_ref[...]
    o_ref[...] = (gy * _sigmoid(z)).astype(o_ref.dtype)


def s5_glu(y, w_glu, b_glu, layer, n_rows):
    width = y.shape[1]
    tm = _row_tile(n_rows, ROW_TILE)
    return pl.pallas_call(
        _glu_kernel,
        grid=(n_rows // tm,),
        in_specs=[pl.BlockSpec((tm, width), lambda i: (i, 0)),
                  pl.BlockSpec((None, width, width), lambda i: (layer, 0, 0)),
                  pl.BlockSpec((None, 1, width), lambda i: (layer, 0, 0))],
        out_specs=pl.BlockSpec((tm, width), lambda i: (i, 0)),
        out_shape=jax.ShapeDtypeStruct((n_rows, width), BF16),
        compiler_params=_cparams("parallel"),
        name="s5_glu",
    )(y, w_glu, b_glu.reshape(b_glu.shape[0], 1, width))


def _merge_kernel(os_ref, ow_ref, od_ref, ws_ref, ww_ref, wd_ref, gs_ref, gw_ref, gd_ref, o_ref):
    m = (gs_ref[...].astype(F32) * _dot(os_ref[...], ws_ref[...].astype(BF16))
         + gw_ref[...].astype(F32) * _dot(ow_ref[...], ww_ref[...].astype(BF16))
         + gd_ref[...].astype(F32) * _dot(od_ref[...], wd_ref[...].astype(BF16)))
    o_ref[...] = m.astype(o_ref.dtype)


def gated_merge(o_ssm, o_win, o_diff, w_s, w_w, w_d, proj, layer, n_rows, d):
    tm = _row_tile(n_rows, 1088)
    tn = 256
    assert d % tn == 0
    g0 = 0
    nd = d // tn

    def branch(arr):
        return pl.BlockSpec((tm, arr.shape[1]), lambda i, j: (i, 0))

    def weight(w):
        return pl.BlockSpec((None, w.shape[1], tn), lambda i, j: (layer, 0, j))

    def gate(k):
        return pl.BlockSpec((tm, tn), lambda i, j: (i, g0 + k * nd + j))

    return pl.pallas_call(
        _merge_kernel,
        grid=(n_rows // tm, nd),
        in_specs=[branch(o_ssm), branch(o_win), branch(o_diff), weight(w_s), weight(w_w), weight(w_d),
                  gate(0), gate(1), gate(2)],
        out_specs=pl.BlockSpec((tm, tn), lambda i, j: (i, j)),
        out_shape=jax.ShapeDtypeStruct((n_rows, d), BF16),
        compiler_params=_cparams("parallel", "arbitrary"),
        name="gated_merge",
    )(o_ssm, o_win, o_diff, w_s, w_w, w_d, proj, proj, proj)


def _out_proj_kernel(m_ref, w_ref, x_ref, g_ref, o_ref):
    o_ref[...] = x_ref[...] + g_ref[0] * _dot(m_ref[...], w_ref[...].astype(BF16))


def out_proj_residual(m, w_out, x, gate, layer, n_rows, rows_lat, seq):
    d = x.shape[1]
    tm = _stream_tile(n_rows, rows_lat, seq)
    tn = min(512, d)
    n_batch = gate.shape[0] - 1
    idx = functools.partial(_stream_index, tm=tm, rows_lat=rows_lat, seq=seq, n_batch=n_batch)
    return pl.pallas_call(
        _out_proj_kernel,
        grid=(n_rows // tm, d // tn),
        in_specs=[pl.BlockSpec((tm, m.shape[1]), lambda i, j: (i, 0)),
                  pl.BlockSpec((None, m.shape[1], tn), lambda i, j: (layer, 0, j)),
                  pl.BlockSpec((tm, tn), lambda i, j: (i, j)),
                  pl.BlockSpec((1, 1, tn), lambda i, j: (idx(i), 0, j))],
        out_specs=pl.BlockSpec((tm, tn), lambda i, j: (i, j)),
        out_shape=jax.ShapeDtypeStruct((n_rows, d), F32),
        compiler_params=_cparams("parallel", "arbitrary"),
        name="out_proj_residual",
    )(m, w_out, x, gate)


def _pack_halves(h):
    half = h.shape[1] // 2
    lo = lax.bitcast_convert_type(h[:, :half].astype(BF16).astype(F32), jnp.uint32)
    hi = lax.bitcast_convert_type(h[:, half:].astype(BF16).astype(F32), jnp.uint32)
    return (lo >> 16) | (hi & jnp.uint32(0xFFFF0000))


def _unpack_halves(x):
    lo = lax.bitcast_convert_type(x << 16, F32).astype(BF16)
    hi = lax.bitcast_convert_type(x & jnp.uint32(0xFFFF0000), F32).astype(BF16)
    return lo, hi


def _dispatch_kernel(pos_ref, h_ref, xb_in_ref, xb_ref, sem):
    del xb_in_ref
    tb = h_ref.shape[0]
    base = pl.program_id(0) * (tb * TOP_K)

    def issue(j, carry):
        for k in range(TOP_K):
            dst = pos_ref[base + j * TOP_K + k]
            pltpu.make_async_copy(h_ref.at[pl.ds(j, 1)], xb_ref.at[pl.ds(dst, 1)], sem).start()
        return carry

    lax.fori_loop(0, tb, issue, 0, unroll=8)

    def drain(j, carry):
        for k in range(TOP_K):
            pltpu.make_async_copy(h_ref.at[pl.ds(j, 1)], xb_ref.at[pl.ds(0, 1)], sem).wait()
        return carry

    lax.fori_loop(0, tb, drain, 0, unroll=8)


def _combine_kernel(pos_ref, yb_ref, x_ref, g_ref, gates_ref, o_ref, buf, sem):
    tb = x_ref.shape[0]
    base = pl.program_id(0) * (tb * TOP_K)

    def issue(j, carry):
        for k in range(TOP_K):
            src = pos_ref[base + j * TOP_K + k]
            pltpu.make_async_copy(yb_ref.at[pl.ds(src, 1)], buf.at[k, pl.ds(j, 1)], sem).start()
        return carry

    lax.fori_loop(0, tb, issue, 0, unroll=8)

    def drain(j, carry):
        for k in range(TOP_K):
            pltpu.make_async_copy(yb_ref.at[pl.ds(0, 1)], buf.at[k, pl.ds(j, 1)], sem).wait()
        return carry

    lax.fori_loop(0, tb, drain, 0, unroll=8)
    gates = gates_ref[...]
    f = gates[:, 0:1] * buf[0]
    for k in range(1, TOP_K):
        f = f + gates[:, k:k + 1] * buf[k]
    o_ref[...] = x_ref[...] + g_ref[0] * f


def _expert_up_kernel(be_ref, nu_ref, x_ref, wg_ref, wl_ref, bg_ref, bl_ref, o_ref, wg_s, wl_s):
    r = pl.program_id(1)

    @pl.when(r < nu_ref[0])
    def _():
        prev = be_ref[jnp.maximum(r - 1, 0)]

        @pl.when((r == 0) | (be_ref[r] != prev))
        def _():
            wg_s[...] = wg_ref[...].astype(BF16)
            wl_s[...] = wl_ref[...].astype(BF16)

        lo, hi = _unpack_halves(x_ref[...])
        half = lo.shape[1]
        glu = _dot(lo, wg_s[:half, :]) + _dot(hi, wg_s[half:, :]) + bg_ref[...]
        lin = _dot(lo, wl_s[:half, :]) + _dot(hi, wl_s[half:, :]) + bl_ref[...]
        glu = jnp.minimum(glu, SWIGLU_LIMIT)
        lin = jnp.clip(lin, -SWIGLU_LIMIT, SWIGLU_LIMIT)
        o_ref[...] = (glu * _sigmoid(SWIGLU_ALPHA * glu) * (lin + 1.0)).astype(o_ref.dtype)

    @pl.when(r >= nu_ref[0])
    def _():
        o_ref[...] = jnp.zeros_like(o_ref)


def _expert_down_kernel(be_ref, nu_ref, a_ref, w_ref, b_ref, o_ref, w_s):
    r = pl.program_id(1)

    @pl.when(r < nu_ref[0])
    def _():
        prev = be_ref[jnp.maximum(r - 1, 0)]

        @pl.when((r == 0) | (be_ref[r] != prev))
        def _():
            w_s[...] = w_ref[...].astype(BF16)

        o_ref[...] = _dot(a_ref[...], w_s[...]) + b_ref[...]

    @pl.when(r >= nu_ref[0])
    def _():
        o_ref[...] = jnp.zeros_like(o_ref)


def expert_ffn(block_exp, n_used, xb, w1, b1, w2, b2, layer):
    n_rows, half = xb.shape
    d = 2 * half
    n_exp, _, f2 = w1.shape[1:]
    f = f2 // 2
    n_blocks = n_rows // MOE_ROWS
    tf = min(512, f)
    nf = f // tf
    b1r = b1.reshape(b1.shape[0], n_exp, 1, f2)
    b2r = b2.reshape(b2.shape[0], n_exp, 1, d)
    act = pl.pallas_call(
        _expert_up_kernel,
        grid_spec=pltpu.PrefetchScalarGridSpec(
            num_scalar_prefetch=2,
            grid=(nf, n_blocks),
            in_specs=[
                pl.BlockSpec((MOE_ROWS, half), lambda j, r, be, nu: (r, 0)),
                pl.BlockSpec((None, None, d, tf), lambda j, r, be, nu: (layer, be[r], 0, j)),
                pl.BlockSpec((None, None, d, tf), lambda j, r, be, nu: (layer, be[r], 0, nf + j)),
                pl.BlockSpec((None, None, 1, tf), lambda j, r, be, nu: (layer, be[r], 0, j)),
                pl.BlockSpec((None, None, 1, tf), lambda j, r, be, nu: (layer, be[r], 0, nf + j)),
            ],
            out_specs=pl.BlockSpec((MOE_ROWS, tf), lambda j, r, be, nu: (r, j)),
            scratch_shapes=[pltpu.VMEM((d, tf), BF16), pltpu.VMEM((d, tf), BF16)],
        ),
        out_shape=jax.ShapeDtypeStruct((n_rows, f), BF16),
        compiler_params=_cparams("arbitrary", "arbitrary"),
        name="expert_up",
    )(block_exp, n_used, xb, w1, w1, b1r, b1r)
    tn = min(512, d)
    return pl.pallas_call(
        _expert_down_kernel,
        grid_spec=pltpu.PrefetchScalarGridSpec(
            num_scalar_prefetch=2,
            grid=(d // tn, n_blocks),
            in_specs=[
                pl.BlockSpec((MOE_ROWS, f), lambda j, r, be, nu: (r, 0)),
                pl.BlockSpec((None, None, f, tn), lambda j, r, be, nu: (layer, be[r], 0, j)),
                pl.BlockSpec((None, None, 1, tn), lambda j, r, be, nu: (layer, be[r], 0, j)),
            ],
            out_specs=pl.BlockSpec((MOE_ROWS, tn), lambda j, r, be, nu: (r, j)),
            scratch_shapes=[pltpu.VMEM((f, tn), BF16)],
        ),
        out_shape=jax.ShapeDtypeStruct((n_rows, d), F32),
        compiler_params=_cparams("arbitrary", "arbitrary"),
        name="expert_down",
    )(block_exp, n_used, act, w2, b2r)


def moe_residual(x, hp, logits, gate, w1, b1, w2, b2, layer, rows_lat, seq):
    n_tok, half = hp.shape
    d = 2 * half
    n_exp = w1.shape[1]
    top_val, top_idx = lax.top_k(logits, TOP_K)
    gates = jax.nn.softmax(top_val, axis=-1)
    n_assign = n_tok * TOP_K
    flat_e = top_idx.reshape(-1)
    onehot = (flat_e[:, None] == jnp.arange(n_exp, dtype=flat_e.dtype)[None, :]).astype(jnp.int32)
    csum = jnp.cumsum(onehot, axis=0)
    counts = csum[-1]
    padded = (counts + MOE_ROWS - 1) // MOE_ROWS * MOE_ROWS
    pend = jnp.cumsum(padded)
    pstart = pend - padded
    pos = jnp.sum(onehot * (csum - 1 + pstart[None, :]), axis=1).astype(jnp.int32)
    n_blocks = -(-n_assign // MOE_ROWS) + n_exp
    n_rows = n_blocks * MOE_ROWS
    block_start = jnp.arange(n_blocks, dtype=jnp.int32) * MOE_ROWS
    block_exp = jnp.minimum(jnp.sum((block_start[:, None] >= pend[None, :]).astype(jnp.int32), axis=1), n_exp - 1)
    n_used = (pend[-1] // MOE_ROWS).astype(jnp.int32).reshape(1)

    tb = math.gcd(128, _stream_tile(n_tok, rows_lat, seq))
    xb = pl.pallas_call(
        _dispatch_kernel,
        grid_spec=pltpu.PrefetchScalarGridSpec(
            num_scalar_prefetch=1,
            grid=(n_tok // tb,),
            in_specs=[pl.BlockSpec((tb, half), lambda i, pos: (i, 0)), pl.BlockSpec(memory_space=pl.ANY)],
            out_specs=pl.BlockSpec(memory_space=pl.ANY),
            scratch_shapes=[pltpu.SemaphoreType.DMA(())],
        ),
        out_shape=jax.ShapeDtypeStruct((n_rows, half), jnp.uint32),
        input_output_aliases={2: 0},
        compiler_params=_cparams("arbitrary"),
        name="expert_dispatch",
    )(pos, hp, jnp.zeros((n_rows, half), jnp.uint32))
    yb = expert_ffn(block_exp.astype(jnp.int32), n_used, xb, w1, b1, w2, b2, layer)
    n_batch = gate.shape[0] - 1
    idx = functools.partial(_stream_index, tm=tb, rows_lat=rows_lat, seq=seq, n_batch=n_batch)
    return pl.pallas_call(
        _combine_kernel,
        grid_spec=pltpu.PrefetchScalarGridSpec(
            num_scalar_prefetch=1,
            grid=(n_tok // tb,),
            in_specs=[pl.BlockSpec(memory_space=pl.ANY),
                      pl.BlockSpec((tb, d), lambda i, pos: (i, 0)),
                      pl.BlockSpec((1, 1, d), lambda i, pos: (idx(i), 0, 0)),
                      pl.BlockSpec((tb, TOP_K), lambda i, pos: (i, 0))],
            out_specs=pl.BlockSpec((tb, d), lambda i, pos: (i, 0)),
            scratch_shapes=[pltpu.VMEM((TOP_K, tb, d), F32), pltpu.SemaphoreType.DMA(())],
        ),
        out_shape=jax.ShapeDtypeStruct((n_tok, d), F32),
        compiler_params=_cparams("arbitrary"),
        name="expert_combine",
    )(pos, yb, x, gate, gates)


def _layer(i, x, c_all, p, n_batch, seq, n_ctx, tables, with_ctx):
    d = x.shape[1]
    rows_lat = n_batch * seq
    rows_all = rows_lat + n_batch * n_ctx
    n_out = rows_all if with_ctx else rows_lat
    n_stream = n_batch + 1

    mod = matmul(c_all, p["w_mod"], i, c_all.shape[0], c_all.shape[0], 512, F32, pre="silu", name="modulation")
    mod = (mod[:n_stream] + p["b_mod"][i]).reshape(n_stream, 6, 1, d)
    sh1, sc1, g1, sh2, sc2, g2 = (mod[:, k] for k in range(6))

    h1 = norm_mod(x, p["g_mix"][i], sc1, sh1, rows_all, rows_lat, seq, BF16)
    in_width = p["w_in"].shape[2]

    n_g = p["ssm_lambda_re"].shape[2]
    ssm_w = n_g * SSM_GROUP
    n_wh = p["win_sink"].shape[1]
    win_q = n_wh * WIN_HEAD_DIM
    win_kv = WIN_KV_HEADS * WIN_HEAD_DIM
    n_dh = (in_width - ssm_w - win_q - 2 * win_kv - 3 * d) // (2 * 2 * DIFF_QK_DIM + DIFF_V_DIM)
    diff_w = n_dh * 2 * DIFF_QK_DIM
    widths = dict(u=ssm_w, qw=win_q, kw=win_kv, vw=win_kv, qd=diff_w, kd=diff_w, vd=diff_w, gates=3 * d)
    src, c0 = {}, 0
    for name in ("u", "qw", "kw", "vw", "qd", "kd", "vd", "gates"):
        src[name] = c0
        c0 += widths[name]
    order = ("u", "qw", "qd", "kd", "vd", "gates", "kw", "vw")
    w_bf = jnp.concatenate([p["w_in"][i][:, src[n]:src[n] + widths[n]] for n in order], axis=1).astype(BF16)
    modes = dict(u=("plain", 1.0, F32), qw=("rope", WIN_HEAD_DIM ** -0.5, BF16), kw=("rope", 1.0, BF16),
                 vw=("plain", 1.0, BF16), qd=("rope", DIFF_QK_DIM ** -0.5, BF16), kd=("rope", 1.0, BF16),
                 vd=("plain", 1.0, BF16), gates=("sigmoid", 1.0, BF16))
    seg, c0 = {}, 0
    for name in order:
        mode, scale, dtype = modes[name]
        seg[name] = in_proj(h1, w_bf, c0, widths[name], mode, scale, dtype, tables, rows_lat, seq)
        c0 += widths[name]

    mats = s5_matrices(p["ssm_lambda_re"][i], p["ssm_lambda_im"][i], p["ssm_log_dt"][i], p["ssm_b_re"][i],
                       p["ssm_b_im"][i], p["ssm_c_re"][i], p["ssm_c_im"][i], p["ssm_d"][i], S5_CHUNK)
    y = s5_mixer(seg["u"], mats, rows_all, n_batch, seq, n_ctx)
    o_ssm = s5_glu(y, p["w_glu"], p["b_glu"], i, n_out)

    sink = p["win_sink"][i].astype(F32)
    o_win = window_attention(sink, seg["qw"], seg["kw"], seg["vw"], n_batch, seq, n_ctx, n_wh, rows_lat, False)

    lam_p = p["diff_lambda"][i].astype(F32)
    lambda_init = 0.8 - 0.6 * math.exp(-0.3 * i)
    lam = (jnp.exp(jnp.sum(lam_p[0] * lam_p[1])) - jnp.exp(jnp.sum(lam_p[2] * lam_p[3])) + lambda_init).reshape(1)
    gain = p["diff_subln"][i].astype(F32).reshape(1, DIFF_V_DIM)
    o_diff = diff_attention(lam, gain, seg["qd"], seg["kd"], seg["vd"], n_batch, seq, n_ctx, n_dh, rows_lat, False,
                            1.0 - lambda_init)
    if with_ctx:
        o_win_c = window_attention(sink, seg["qw"], seg["kw"], seg["vw"], n_batch, seq, n_ctx, n_wh, rows_lat, True)
        o_diff_c = diff_attention(lam, gain, seg["qd"], seg["kd"], seg["vd"], n_batch, seq, n_ctx, n_dh, rows_lat, True,
                                  1.0 - lambda_init)
        o_win = jnp.concatenate([o_win, o_win_c], axis=0)
        o_diff = jnp.concatenate([o_diff, o_diff_c], axis=0)

    merged = gated_merge(o_ssm, o_win, o_diff, p["w_branch_ssm"], p["w_branch_win"], p["w_branch_diff"], seg["gates"],
                         i, n_out, d)
    x = out_proj_residual(merged, p["w_out"], x, g1, i, n_out, rows_lat, seq)

    hp, logits = norm_mod(x, p["g_ffn"][i], sc2, sh2, n_out, rows_lat, seq, BF16,
                          router=(p["w_router"][i], p["b_router"][i]))
    return moe_residual(x, hp, logits, g2, p["w_exp1"], p["b_exp1"], p["w_exp2"], p["b_exp2"], i, rows_lat, seq)


def kernel(x, c, ctx, c_ctx, w_mod, b_mod, g_mix, g_ffn, w_in, ssm_lambda_re, ssm_lambda_im, ssm_log_dt, ssm_b_re, ssm_b_im, ssm_c_re, ssm_c_im, ssm_d, w_glu, b_glu, win_sink, diff_lambda, diff_subln, w_branch_ssm, w_branch_win, w_branch_diff, w_out, w_router, b_router, w_exp1, b_exp1, w_exp2, b_exp2, g_final):
    n_batch, seq, d = x.shape
    n_ctx = ctx.shape[1]
    depth = w_mod.shape[0]
    p = dict(w_mod=w_mod, b_mod=b_mod, g_mix=g_mix, g_ffn=g_ffn, w_in=w_in, ssm_lambda_re=ssm_lambda_re,
             ssm_lambda_im=ssm_lambda_im, ssm_log_dt=ssm_log_dt, ssm_b_re=ssm_b_re, ssm_b_im=ssm_b_im,
             ssm_c_re=ssm_c_re, ssm_c_im=ssm_c_im, ssm_d=ssm_d, w_glu=w_glu, b_glu=b_glu, win_sink=win_sink,
             diff_lambda=diff_lambda, diff_subln=diff_subln, w_branch_ssm=w_branch_ssm, w_branch_win=w_branch_win,
             w_branch_diff=w_branch_diff, w_out=w_out, w_router=w_router, b_router=b_router, w_exp1=w_exp1,
             b_exp1=b_exp1, w_exp2=w_exp2, b_exp2=b_exp2)
    rows_lat = n_batch * seq
    rows = jnp.concatenate([x.reshape(rows_lat, d), ctx.reshape(n_batch * n_ctx, d)], axis=0)
    c_all = jnp.concatenate([c, c_ctx[None, :], jnp.zeros((8 - (n_batch + 1) % 8, d), F32)], axis=0)
    tables = rope_tables(seq, WIN_HEAD_DIM)
    for i in range(depth):
        with_ctx = i < depth - 1
        new = _layer(i, rows, c_all, p, n_batch, seq, n_ctx, tables, with_ctx)
        rows = new if with_ctx else jnp.concatenate([new, rows[rows_lat:]], axis=0)
    no_mod = jnp.zeros((n_batch + 1, 1, d), F32)
    out = norm_mod(rows, g_final, no_mod, no_mod, rows_lat, rows_lat, seq, F32)
    return out.reshape(n_batch, seq, d)
```

```python
import functools
import math

import jax
import jax.numpy as jnp
import numpy as np
from jax import lax
from jax.experimental import pallas as pl
from jax.experimental.pallas import tpu as pltpu

F32 = jnp.float32
BF16 = jnp.bfloat16

GRID_W = 64
SSM_GROUP = 16
WIN_KV_HEADS = 2
WIN_HEAD_DIM = 64
WINDOW = 128
DIFF_QK_DIM = 64
DIFF_V_DIM = 2 * DIFF_QK_DIM
ATTN_BLOCK = 128
ROPE_BASE = 10000.0
TOP_K = 4
SWIGLU_LIMIT = 7.0
SWIGLU_ALPHA = 1.702
NORM_EPS = 1e-6
SUBLN_EPS = 1e-5
NEG_INF = -1e30

LANE = 128
VMEM_LIMIT_BYTES = 56 * 1024 * 1024
S5_CHUNK = 32
MOE_ROWS = 512
ROW_TILE = 512


def _cparams(*sem):
    return pltpu.CompilerParams(dimension_semantics=sem, vmem_limit_bytes=VMEM_LIMIT_BYTES)


def _dot(a, b):
    return jnp.dot(a, b, preferred_element_type=F32)


def _dot_nt(a, b):
    return lax.dot_general(a, b, (((1,), (1,)), ((), ())), preferred_element_type=F32)


def _sigmoid(x):
    return 1.0 / (1.0 + jnp.exp(-x))


def _row_tile(n_rows, cap):
    best = 16
    for t in range(16, cap + 1, 16):
        if n_rows % t == 0:
            best = t
    return best


def _norm_mod_kernel(x_ref, g_ref, sc_ref, sh_ref, o_ref):
    x = x_ref[...]
    y = x * lax.rsqrt(jnp.mean(x * x, axis=-1, keepdims=True) + NORM_EPS) * g_ref[...]
    o_ref[...] = (y * (1.0 + sc_ref[0]) + sh_ref[0]).astype(o_ref.dtype)


def _norm_router_kernel(x_ref, g_ref, sc_ref, sh_ref, wr_ref, br_ref, o_ref, lg_ref):
    x = x_ref[...]
    y = x * lax.rsqrt(jnp.mean(x * x, axis=-1, keepdims=True) + NORM_EPS) * g_ref[...]
    h = y * (1.0 + sc_ref[0]) + sh_ref[0]
    o_ref[...] = _pack_halves(h)
    lg_ref[...] = jnp.dot(h, wr_ref[...], preferred_element_type=F32, precision=lax.Precision.HIGHEST) + br_ref[...]


def _stream_tile(n_rows, rows_lat, seq):
    tm = math.gcd(ROW_TILE, seq)
    return math.gcd(tm, n_rows - rows_lat) if n_rows > rows_lat else tm


def _stream_index(i, tm, rows_lat, seq, n_batch):
    r = i * tm
    return jnp.where(r >= rows_lat, n_batch, r // seq)


def norm_mod(x, gain, sc, sh, n_rows, rows_lat, seq, out_dtype, router=None):
    d = x.shape[1]
    tm = _stream_tile(n_rows, rows_lat, seq)
    n_batch = sc.shape[0] - 1
    idx = functools.partial(_stream_index, tm=tm, rows_lat=rows_lat, seq=seq, n_batch=n_batch)
    in_specs = [
        pl.BlockSpec((tm, d), lambda i: (i, 0)),
        pl.BlockSpec((1, d), lambda i: (0, 0)),
        pl.BlockSpec((1, 1, d), lambda i: (idx(i), 0, 0)),
        pl.BlockSpec((1, 1, d), lambda i: (idx(i), 0, 0)),
    ]
    if router is None:
        return pl.pallas_call(
            _norm_mod_kernel,
            grid=(n_rows // tm,),
            in_specs=in_specs,
            out_specs=pl.BlockSpec((tm, d), lambda i: (i, 0)),
            out_shape=jax.ShapeDtypeStruct((n_rows, d), out_dtype),
            compiler_params=_cparams("parallel"),
            name="norm_mod",
        )(x, gain.reshape(1, d), sc, sh)
    w_router, b_router = router
    n_exp = w_router.shape[1]
    return pl.pallas_call(
        _norm_router_kernel,
        grid=(n_rows // tm,),
        in_specs=in_specs + [
            pl.BlockSpec((d, n_exp), lambda i: (0, 0)),
            pl.BlockSpec((1, n_exp), lambda i: (0, 0)),
        ],
        out_specs=[pl.BlockSpec((tm, d // 2), lambda i: (i, 0)), pl.BlockSpec((tm, n_exp), lambda i: (i, 0))],
        out_shape=[jax.ShapeDtypeStruct((n_rows, d // 2), jnp.uint32), jax.ShapeDtypeStruct((n_rows, n_exp), F32)],
        compiler_params=_cparams("parallel"),
        name="norm_router",
    )(x, gain.reshape(1, d), sc, sh, w_router, b_router.reshape(1, n_exp))


def _mm_kernel(x_ref, w_ref, o_ref, *, pre):
    x = x_ref[...]
    if pre == "silu":
        x = x * _sigmoid(x)
    o_ref[...] = _dot(x.astype(BF16), w_ref[...].astype(BF16)).astype(o_ref.dtype)


def matmul(x, w_stack, layer, n_rows, tm, tn, out_dtype, pre=None, name="matmul"):
    k = x.shape[1]
    n = w_stack.shape[2]
    return pl.pallas_call(
        functools.partial(_mm_kernel, pre=pre),
        grid=(n_rows // tm, n // tn),
        in_specs=[
            pl.BlockSpec((tm, k), lambda i, j: (i, 0)),
            pl.BlockSpec((None, k, tn), lambda i, j: (layer, 0, j)),
        ],
        out_specs=pl.BlockSpec((tm, tn), lambda i, j: (i, j)),
        out_shape=jax.ShapeDtypeStruct((n_rows, n), out_dtype),
        compiler_params=_cparams("parallel", "arbitrary"),
        name=name,
    )(x, w_stack)


def _in_proj_kernel(x_ref, w_ref, *refs, mode, scale):
    o_ref = refs[-1]
    acc = _dot(x_ref[...], w_ref[...])
    if mode == "sigmoid":
        o_ref[...] = _sigmoid(acc).astype(o_ref.dtype)
        return
    if mode == "plain":
        o_ref[...] = (acc if scale == 1.0 else acc * scale).astype(o_ref.dtype)
        return
    cos_ref, sin_ref = refs[0], refs[1]
    lane = lax.broadcasted_iota(jnp.int32, (1, LANE), 1)
    first_half = (lane % 32) < 16
    cos = cos_ref[...] if scale == 1.0 else cos_ref[...] * scale
    sin = sin_ref[...] if scale == 1.0 else sin_ref[...] * scale
    for c in range(acc.shape[1] // LANE):
        x = acc[:, c * LANE:(c + 1) * LANE]
        partner = jnp.where(first_half, pltpu.roll(x, LANE - 16, axis=1), pltpu.roll(x, 16, axis=1))
        o_ref[:, c * LANE:(c + 1) * LANE] = (x * cos + partner * sin).astype(o_ref.dtype)


def in_proj(h, w, col0, width, mode, scale, out_dtype, tables, rows_lat, seq):
    n_rows, k = h.shape
    tm = _stream_tile(n_rows, rows_lat, seq)
    tn = math.gcd(1024, width)
    assert col0 % tn == 0
    in_specs = [pl.BlockSpec((tm, k), lambda i, j: (i, 0)),
                pl.BlockSpec((k, tn), lambda i, j: (0, col0 // tn + j))]
    args = [h, w]
    if mode == "rope":
        cos_t, sin_t = tables
        tiles_per_seq = seq // tm
        lat_tiles = rows_lat // tm
        cos_x = jnp.concatenate([cos_t, jnp.ones((tm, LANE), F32)], axis=0)
        sin_x = jnp.concatenate([sin_t, jnp.zeros((tm, LANE), F32)], axis=0)
        tab = pl.BlockSpec((tm, LANE), lambda i, j: (jnp.where(i < lat_tiles, i % tiles_per_seq, tiles_per_seq), 0))
        in_specs += [tab, tab]
        args += [cos_x, sin_x]
    return pl.pallas_call(
        functools.partial(_in_proj_kernel, mode=mode, scale=scale),
        grid=(n_rows // tm, width // tn),
        in_specs=in_specs,
        out_specs=pl.BlockSpec((tm, tn), lambda i, j: (i, j)),
        out_shape=jax.ShapeDtypeStruct((n_rows, width), out_dtype),
        compiler_params=_cparams("parallel", "arbitrary"),
        name="in_proj_" + mode,
    )(*args)


def rope_tables(seq, head_dim):
    assert head_dim == 64
    rows = seq // GRID_W
    r, col = jnp.meshgrid(jnp.arange(rows, dtype=F32), jnp.arange(GRID_W, dtype=F32), indexing="ij")
    half = head_dim // 2
    inv_freq = ROPE_BASE ** (-jnp.arange(0, half, 2, dtype=F32) / half)
    ang_r = r.reshape(-1)[:, None] * inv_freq[None, :]
    ang_c = col.reshape(-1)[:, None] * inv_freq[None, :]
    cos = jnp.concatenate([jnp.cos(ang_r), jnp.cos(ang_r), jnp.cos(ang_c), jnp.cos(ang_c)], axis=-1)
    sin = jnp.concatenate([-jnp.sin(ang_r), jnp.sin(ang_r), -jnp.sin(ang_c), jnp.sin(ang_c)], axis=-1)
    return jnp.tile(cos, (1, LANE // head_dim)), jnp.tile(sin, (1, LANE // head_dim))


def _win_attn_kernel(sink_ref, q_ref, *refs, n_heads, band, seq):
    if band:
        kp_ref, kc_ref, kn_ref, vp_ref, vc_ref, vn_ref, kx_ref, vx_ref, o_ref = refs
    else:
        kx_ref, vx_ref, o_ref = refs
    blk = q_ref.shape[0]
    dh = WIN_HEAD_DIM
    grp = n_heads // WIN_KV_HEADS
    lane = lax.broadcasted_iota(jnp.int32, (1, LANE), 1)
    kx = kx_ref[...]
    vx = vx_ref[...]
    rows = grp * blk
    if band:
        n = pl.program_id(1)
        k_loc = jnp.concatenate([kp_ref[...], kc_ref[...], kn_ref[...]], axis=0)
        v_loc = jnp.concatenate([vp_ref[...], vc_ref[...], vn_ref[...]], axis=0)
        qpos = n * blk + (lax.broadcasted_iota(jnp.int32, (rows, 3 * blk), 0) & (blk - 1))
        kpos = (n - 1) * blk + lax.broadcasted_iota(jnp.int32, (rows, 3 * blk), 1)
        mask = (jnp.abs(qpos - kpos) <= WINDOW) & (kpos >= 0) & (kpos < seq)
    outs = [None] * n_heads
    for kvh in range(WIN_KV_HEADS):
        keep = (lane >= kvh * dh) & (lane < (kvh + 1) * dh)
        q_parts, sink_parts = [], []
        for g in range(grp):
            h = kvh * grp + g
            c = (h * dh) // LANE
            qc = q_ref[:, c * LANE:(c + 1) * LANE]
            if (h * dh) % LANE != kvh * dh:
                qc = jnp.concatenate([qc[:, dh:], qc[:, :dh]], axis=1)
            q_parts.append(jnp.where(keep, qc, jnp.zeros_like(qc)))
            sink_parts.append(jnp.full((blk, 1), sink_ref[h], F32))
        qs = jnp.concatenate(q_parts, axis=0)
        sink = jnp.concatenate(sink_parts, axis=0)
        s_x = _dot_nt(qs, kx)
        m = jnp.maximum(jnp.max(s_x, axis=-1, keepdims=True), sink)
        if band:
            s_l = jnp.where(mask, _dot_nt(qs, k_loc), NEG_INF)
            m = jnp.maximum(m, jnp.max(s_l, axis=-1, keepdims=True))
            e_l = jnp.exp(s_l - m)
        e_x = jnp.exp(s_x - m)
        den = jnp.sum(e_x, axis=-1, keepdims=True) + jnp.exp(sink - m)
        if band:
            den = den + jnp.sum(e_l, axis=-1, keepdims=True)
        inv = 1.0 / den
        o = _dot((e_x * inv).astype(BF16), vx)
        if band:
            o = o + _dot((e_l * inv).astype(BF16), v_loc)
        for g in range(grp):
            outs[kvh * grp + g] = o[g * blk:(g + 1) * blk, kvh * dh:(kvh + 1) * dh]
    o_ref[...] = jnp.concatenate(outs, axis=1).astype(o_ref.dtype)


def window_attention(sink, q, k, v, n_batch, seq, n_ctx, n_heads, rows_lat, ctx_queries):
    blk = ATTN_BLOCK
    seq_q = n_ctx if ctx_queries else seq
    nb = seq_q // blk
    q0 = rows_lat // blk if ctx_queries else 0
    x0 = rows_lat // n_ctx
    qw = n_heads * WIN_HEAD_DIM
    kvw = WIN_KV_HEADS * WIN_HEAD_DIM
    in_specs = [
        pl.BlockSpec(memory_space=pltpu.SMEM),
        pl.BlockSpec((blk, qw), lambda b, n: (q0 + b * nb + n, 0)),
    ]
    args = [sink, q]
    if not ctx_queries:
        band_specs = [
            pl.BlockSpec((blk, kvw), lambda b, n: (b * nb + jnp.maximum(n - 1, 0), 0)),
            pl.BlockSpec((blk, kvw), lambda b, n: (b * nb + n, 0)),
            pl.BlockSpec((blk, kvw), lambda b, n: (b * nb + jnp.minimum(n + 1, nb - 1), 0)),
        ]
        in_specs += band_specs + band_specs
        args += [k, k, k, v, v, v]
    in_specs += [pl.BlockSpec((n_ctx, kvw), lambda b, n: (x0 + b, 0))] * 2
    args += [k, v]
    return pl.pallas_call(
        functools.partial(_win_attn_kernel, n_heads=n_heads, band=not ctx_queries, seq=seq),
        grid=(n_batch, nb),
        in_specs=in_specs,
        out_specs=pl.BlockSpec((blk, qw), lambda b, n: (b * nb + n, 0)),
        out_shape=jax.ShapeDtypeStruct((n_batch * seq_q, qw), BF16),
        compiler_params=_cparams("parallel", "parallel"),
        name="context_window_attention" if ctx_queries else "window_attention",
    )(*args)


def _diff_attn_kernel(lam_ref, q_ref, gain_ref, *refs, with_lat, post_scale):
    if with_lat:
        kl_ref, vl_ref, kx_ref, vx_ref, o_ref = refs
    else:
        kx_ref, vx_ref, o_ref = refs
    lam = lam_ref[0]
    q = q_ref[...]
    lane = lax.broadcasted_iota(jnp.int32, (1, LANE), 1)
    zero = jnp.zeros_like(q)
    kx = kx_ref[...]

    def softmax_parts(qm):
        s_x = _dot_nt(qm, kx)
        m = jnp.max(s_x, axis=-1, keepdims=True)
        if with_lat:
            s_l = _dot_nt(qm, kl_ref[...])
            m = jnp.maximum(m, jnp.max(s_l, axis=-1, keepdims=True))
            e_l = jnp.exp(s_l - m)
        else:
            e_l = None
        e_x = jnp.exp(s_x - m)
        den = jnp.sum(e_x, axis=-1, keepdims=True)
        if with_lat:
            den = den + jnp.sum(e_l, axis=-1, keepdims=True)
        return e_l, e_x, 1.0 / den

    e1l, e1x, inv1 = softmax_parts(jnp.where(lane < DIFF_QK_DIM, q, zero))
    e2l, e2x, inv2 = softmax_parts(jnp.where(lane >= DIFF_QK_DIM, q, zero))
    w2 = lam * inv2
    o = _dot((e1x * inv1 - e2x * w2).astype(BF16), vx_ref[...])
    if with_lat:
        o = o + _dot((e1l * inv1 - e2l * w2).astype(BF16), vl_ref[...])
    o = o * lax.rsqrt(jnp.mean(o * o, axis=-1, keepdims=True) + SUBLN_EPS) * gain_ref[...]
    o_ref[...] = (o * post_scale).astype(o_ref.dtype)


def diff_attention(lam, gain, q, k, v, n_batch, seq, n_ctx, n_heads, rows_lat, ctx_queries, post_scale):
    seq_q = n_ctx if ctx_queries else seq
    tq = min(256, seq_q)
    nq = seq_q // tq
    q0 = rows_lat // tq if ctx_queries else 0
    x0 = rows_lat // n_ctx
    in_specs = [
        pl.BlockSpec(memory_space=pltpu.SMEM),
        pl.BlockSpec((tq, LANE), lambda b, h, i: (q0 + b * nq + i, h)),
        pl.BlockSpec((1, LANE), lambda b, h, i: (0, 0)),
    ]
    args = [lam, q, gain]
    if not ctx_queries:
        in_specs += [pl.BlockSpec((seq, LANE), lambda b, h, i: (b, h))] * 2
        args += [k, v]
    in_specs += [pl.BlockSpec((n_ctx, LANE), lambda b, h, i: (x0 + b, h))] * 2
    args += [k, v]
    return pl.pallas_call(
        functools.partial(_diff_attn_kernel, with_lat=not ctx_queries, post_scale=post_scale),
        grid=(n_batch, n_heads, nq),
        in_specs=in_specs,
        out_specs=pl.BlockSpec((tq, LANE), lambda b, h, i: (b * nq + i, h)),
        out_shape=jax.ShapeDtypeStruct((n_batch * seq_q, n_heads * LANE), BF16),
        compiler_params=_cparams("parallel", "parallel", "arbitrary"),
        name="context_diff_attention" if ctx_queries else "diff_attention",
    )(*args)


def s5_matrices(lam_re, lam_im, log_dt, b_re, b_im, c_re, c_im, d_skip, t_chunk):
    n_dir, n_g, n_p = lam_re.shape
    n_h = b_re.shape[-1]
    lam_re = jnp.minimum(lam_re.astype(F32), -1e-4)
    lam_im = lam_im.astype(F32)
    dt = jnp.exp(log_dt.astype(F32))[..., None]
    mag = jnp.exp(lam_re * dt)
    a_re = mag * jnp.cos(lam_im * dt)
    a_im = mag * jnp.sin(lam_im * dt)
    den = lam_re * lam_re + lam_im * lam_im
    k_re = ((a_re - 1.0) * lam_re + a_im * lam_im) / den
    k_im = (a_im * lam_re - (a_re - 1.0) * lam_im) / den
    b_re = b_re.astype(F32)
    b_im = b_im.astype(F32)
    bb_re = k_re[..., None] * b_re - k_im[..., None] * b_im
    bb_im = k_re[..., None] * b_im + k_im[..., None] * b_re
    c_re = c_re.astype(F32)
    c_im = c_im.astype(F32)
    ldt_re = lam_re * dt
    ldt_im = lam_im * dt
    steps = jnp.arange(t_chunk, dtype=F32)

    def power(d, t, p_last):
        lr = ldt_re[d][:, None, :] if p_last else ldt_re[d][:, :, None]
        li = ldt_im[d][:, None, :] if p_last else ldt_im[d][:, :, None]
        tt = t[None, :, None] if p_last else t[None, None, :]
        mag_t = jnp.exp(lr * tt)
        return mag_t * jnp.cos(li * tt), mag_t * jnp.sin(li * tt)

    def c_times_power(d, t):
        pr, pi = power(d, t, False)
        cr = jnp.swapaxes(c_re[d], 1, 2)[:, :, None, :]
        ci = jnp.swapaxes(c_im[d], 1, 2)[:, :, None, :]
        re = cr * pr[..., None] - ci * pi[..., None]
        im = cr * pi[..., None] + ci * pr[..., None]
        return jnp.concatenate([re, -im], axis=1).reshape(n_g, 2 * n_p, t_chunk * n_h)

    def b_times_power(d, t):
        pr, pi = power(d, t, True)
        br = jnp.swapaxes(bb_re[d], 1, 2)[:, None, :, :]
        bi = jnp.swapaxes(bb_im[d], 1, 2)[:, None, :, :]
        re = pr[:, :, None, :] * br - pi[:, :, None, :] * bi
        im = pr[:, :, None, :] * bi + pi[:, :, None, :] * br
        return jnp.concatenate([re, im], axis=-1).reshape(n_g, t_chunk * n_h, 2 * n_p)

    ca_lag = jnp.stack([c_times_power(0, steps), c_times_power(1, t_chunk - 1 - steps)])
    bbt = jnp.stack([jnp.concatenate([jnp.swapaxes(bb_re[d], 1, 2), jnp.swapaxes(bb_im[d], 1, 2)], axis=-1)
                     for d in range(2)])
    p_mat = jnp.concatenate([b_times_power(0, t_chunk - 1 - steps), b_times_power(1, steps)], axis=-1)
    q_mat = jnp.concatenate([c_times_power(0, steps + 1.0), c_times_power(1, t_chunk - steps)], axis=1)
    d_tile = jnp.tile(d_skip.astype(F32).reshape(n_g, 1, n_h), (1, 1, t_chunk))
    mag_t = jnp.exp(ldt_re * t_chunk)
    at_re, at_im = mag_t * jnp.cos(ldt_im * t_chunk), mag_t * jnp.sin(ldt_im * t_chunk)
    at_mul = jnp.concatenate([at_re, at_re], axis=-1)
    at_swp = jnp.concatenate([-at_im, at_im], axis=-1)
    return ca_lag, bbt, p_mat.astype(BF16), q_mat.astype(BF16), d_tile, at_mul, at_swp


def _s5_local_kernel(x_ref, p_ref, e_ref):
    e_ref[...] = _dot(x_ref[...].astype(BF16), p_ref[...])


def _s5_out_kernel(x_ref, s_ref, ca_ref, bbt_ref, q_ref, d_ref, y_ref, m_ref):
    n_h = bbt_ref.shape[1]
    th = m_ref.shape[0]
    t_chunk = th // n_h
    hi = lax.Precision.HIGHEST
    row_f = jnp.dot(bbt_ref[0], ca_ref[0], preferred_element_type=F32, precision=hi)
    row_b = jnp.dot(bbt_ref[1], ca_ref[1], preferred_element_type=F32, precision=hi)
    lane = lax.broadcasted_iota(jnp.int32, (n_h, th), 1)
    for j in range(t_chunk):
        fwd = jnp.where(lane >= n_h * j, pltpu.roll(row_f, n_h * j, axis=1), 0.0) if j else row_f
        back = n_h * (t_chunk - 1 - j)
        bwd = jnp.where(lane < n_h * (j + 1), pltpu.roll(row_b, th - back, axis=1), 0.0) if back else row_b
        m_ref[n_h * j:n_h * (j + 1), :] = (fwd + bwd).astype(m_ref.dtype)
    x = x_ref[...]
    y_ref[...] = _dot(x.astype(BF16), m_ref[...]) + _dot(s_ref[...].astype(BF16), q_ref[...]) + x * d_ref[...]


def _s5_scan_kernel(ef_ref, eb_ref, mul_ref, swp_ref, sf_ref, sb_ref, *, n_batch, lat_chunks, ctx_chunks):
    half = ef_ref.shape[2] // 2
    mul_f, mul_b = mul_ref[0], mul_ref[1]
    swp_f, swp_b = swp_ref[0], swp_ref[1]
    zero = jnp.zeros(ef_ref.shape[1:], F32)

    def step(c_f, c_b, carry):
        s_f, s_b = carry
        sf_ref[c_f] = s_f
        sb_ref[c_b] = s_b
        s_f = s_f * mul_f + pltpu.roll(s_f, half, axis=1) * swp_f + ef_ref[c_f]
        s_b = s_b * mul_b + pltpu.roll(s_b, half, axis=1) * swp_b + eb_ref[c_b]
        return s_f, s_b

    for b in range(n_batch):
        ctx0 = n_batch * lat_chunks + b * ctx_chunks
        lat0 = b * lat_chunks
        carry = lax.fori_loop(0, ctx_chunks, lambda i, cr: step(ctx0 + i, ctx0 + ctx_chunks - 1 - i, cr), (zero, zero))
        lax.fori_loop(0, lat_chunks, lambda i, cr: step(lat0 + i, lat0 + lat_chunks - 1 - i, cr), carry)


def s5_mixer(proj, mats, n_rows, n_batch, seq, n_ctx):
    ca_lag, bbt, p_mat, q_mat, d_tile, at_mul, at_swp = mats
    n_g, th, _ = p_mat.shape
    n_h = SSM_GROUP
    t_chunk = th // n_h
    p4 = p_mat.shape[2]
    n_p2 = p4 // 2
    width = n_g * n_h
    n_chunks = n_rows // t_chunk
    u = proj[:, :width]
    x = u.reshape(n_chunks, t_chunk, n_g, n_h).transpose(2, 0, 1, 3).reshape(n_g, n_chunks, th)
    e = pl.pallas_call(
        _s5_local_kernel,
        grid=(n_g,),
        in_specs=[pl.BlockSpec((None, n_chunks, th), lambda g: (g, 0, 0)),
                  pl.BlockSpec((None, th, p4), lambda g: (g, 0, 0))],
        out_specs=pl.BlockSpec((n_chunks, p4), lambda g: (0, g)),
        out_shape=jax.ShapeDtypeStruct((n_chunks, n_g * p4), F32),
        compiler_params=_cparams("parallel"),
        name="s5_local_state",
    )(x, p_mat)
    e = e.reshape(n_chunks, n_g, 2, n_p2)
    gb = 16
    spec = pl.BlockSpec((n_chunks, gb, n_p2), lambda g: (0, g, 0))
    tab = pl.BlockSpec((2, gb, n_p2), lambda g: (0, g, 0))
    s_f, s_b = pl.pallas_call(
        functools.partial(_s5_scan_kernel, n_batch=n_batch, lat_chunks=seq // t_chunk, ctx_chunks=n_ctx // t_chunk),
        grid=(n_g // gb,),
        in_specs=[spec, spec, tab, tab],
        out_specs=[spec, spec],
        out_shape=[jax.ShapeDtypeStruct((n_chunks, n_g, n_p2), F32)] * 2,
        compiler_params=_cparams("parallel"),
        name="s5_chunk_scan",
    )(e[:, :, 0], e[:, :, 1], at_mul, at_swp)
    s_in = jnp.stack([s_f, s_b], axis=2).reshape(n_chunks, n_g * p4)
    y = pl.pallas_call(
        _s5_out_kernel,
        grid=(n_g,),
        in_specs=[pl.BlockSpec((None, n_chunks, th), lambda g: (g, 0, 0)),
                  pl.BlockSpec((n_chunks, p4), lambda g: (0, g)),
                  pl.BlockSpec((2, None, p4 // 2, th), lambda g: (0, g, 0, 0)),
                  pl.BlockSpec((2, None, n_h, p4 // 2), lambda g: (0, g, 0, 0)),
                  pl.BlockSpec((None, p4, th), lambda g: (g, 0, 0)),
                  pl.BlockSpec((None, 1, th), lambda g: (g, 0, 0))],
        out_specs=pl.BlockSpec((None, n_chunks, th), lambda g: (g, 0, 0)),
        out_shape=jax.ShapeDtypeStruct((n_g, n_chunks, th), F32),
        scratch_shapes=[pltpu.VMEM((th, th), BF16)],
        compiler_params=_cparams("parallel"),
        name="s5_output",
    )(x, s_in, ca_lag, bbt, q_mat, d_tile)
    return y.reshape(n_g, n_chunks, t_chunk, n_h).transpose(1, 2, 0, 3).reshape(n_rows, width)


def _glu_kernel(y_ref, w_ref, b_ref, o_ref):
    y = y_ref[...]
    gy = 0.5 * y * (1.0 + jnp.tanh(math.sqrt(2.0 / math.pi) * (y + 0.044715 * (y * y * y))))
    z = _dot(gy.astype(BF16), w_ref[...].astype(BF16)) + b_ref[...]
    o_ref[...] = (gy * _sigmoid(z)).astype(o_ref.dtype)


def s5_glu(y, w_glu, b_glu, layer, n_rows):
    width = y.shape[1]
    tm = _row_tile(n_rows, ROW_TILE)
    return pl.pallas_call(
        _glu_kernel,
        grid=(n_rows // tm,),
        in_specs=[pl.BlockSpec((tm, width), lambda i: (i, 0)),
                  pl.BlockSpec((None, width, width), lambda i: (layer, 0, 0)),
                  pl.BlockSpec((None, 1, width), lambda i: (layer, 0, 0))],
        out_specs=pl.BlockSpec((tm, width), lambda i: (i, 0)),
        out_shape=jax.ShapeDtypeStruct((n_rows, width), BF16),
        compiler_params=_cparams("parallel"),
        name="s5_glu",
    )(y, w_glu, b_glu.reshape(b_glu.shape[0], 1, width))


def _merge_kernel(os_ref, ow_ref, od_ref, ws_ref, ww_ref, wd_ref, gs_ref, gw_ref, gd_ref, o_ref):
    m = (gs_ref[...].astype(F32) * _dot(os_ref[...], ws_ref[...].astype(BF16))
         + gw_ref[...].astype(F32) * _dot(ow_ref[...], ww_ref[...].astype(BF16))
         + gd_ref[...].astype(F32) * _dot(od_ref[...], wd_ref[...].astype(BF16)))
    o_ref[...] = m.astype(o_ref.dtype)


def gated_merge(o_ssm, o_win, o_diff, w_s, w_w, w_d, proj, layer, n_rows, d):
    tm = _row_tile(n_rows, 1088)
    tn = 256
    assert d % tn == 0
    g0 = 0
    nd = d // tn

    def branch(arr):
        return pl.BlockSpec((tm, arr.shape[1]), lambda i, j: (i, 0))

    def weight(w):
        return pl.BlockSpec((None, w.shape[1], tn), lambda i, j: (layer, 0, j))

    def gate(k):
        return pl.BlockSpec((tm, tn), lambda i, j: (i, g0 + k * nd + j))

    return pl.pallas_call(
        _merge_kernel,
        grid=(n_rows // tm, nd),
        in_specs=[branch(o_ssm), branch(o_win), branch(o_diff), weight(w_s), weight(w_w), weight(w_d),
                  gate(0), gate(1), gate(2)],
        out_specs=pl.BlockSpec((tm, tn), lambda i, j: (i, j)),
        out_shape=jax.ShapeDtypeStruct((n_rows, d), BF16),
        compiler_params=_cparams("parallel", "arbitrary"),
        name="gated_merge",
    )(o_ssm, o_win, o_diff, w_s, w_w, w_d, proj, proj, proj)


def _out_proj_kernel(m_ref, w_ref, x_ref, g_ref, o_ref):
    o_ref[...] = x_ref[...] + g_ref[0] * _dot(m_ref[...], w_ref[...].astype(BF16))


def out_proj_residual(m, w_out, x, gate, layer, n_rows, rows_lat, seq):
    d = x.shape[1]
    tm = _stream_tile(n_rows, rows_lat, seq)
    tn = min(1024, d)
    n_batch = gate.shape[0] - 1
    idx = functools.partial(_stream_index, tm=tm, rows_lat=rows_lat, seq=seq, n_batch=n_batch)
    return pl.pallas_call(
        _out_proj_kernel,
        grid=(n_rows // tm, d // tn),
        in_specs=[pl.BlockSpec((tm, m.shape[1]), lambda i, j: (i, 0)),
                  pl.BlockSpec((None, m.shape[1], tn), lambda i, j: (layer, 0, j)),
                  pl.BlockSpec((tm, tn), lambda i, j: (i, j)),
                  pl.BlockSpec((1, 1, tn), lambda i, j: (idx(i), 0, j))],
        out_specs=pl.BlockSpec((tm, tn), lambda i, j: (i, j)),
        out_shape=jax.ShapeDtypeStruct((n_rows, d), F32),
        compiler_params=_cparams("parallel", "arbitrary"),
        name="out_proj_residual",
    )(m, w_out, x, gate)


def _pack_halves(h):
    half = h.shape[1] // 2
    lo = lax.bitcast_convert_type(h[:, :half].astype(BF16).astype(F32), jnp.uint32)
    hi = lax.bitcast_convert_type(h[:, half:].astype(BF16).astype(F32), jnp.uint32)
    return (lo >> 16) | (hi & jnp.uint32(0xFFFF0000))


def _unpack_halves(x):
    lo = lax.bitcast_convert_type(x << 16, F32).astype(BF16)
    hi = lax.bitcast_convert_type(x & jnp.uint32(0xFFFF0000), F32).astype(BF16)
    return lo, hi


def _dispatch_kernel(pos_ref, h_ref, xb_in_ref, xb_ref, sem):
    del xb_in_ref
    tb = h_ref.shape[0]
    base = pl.program_id(0) * (tb * TOP_K)

    def issue(j, carry):
        for k in range(TOP_K):
            dst = pos_ref[base + j * TOP_K + k]
            pltpu.make_async_copy(h_ref.at[pl.ds(j, 1)], xb_ref.at[pl.ds(dst, 1)], sem).start()
        return carry

    lax.fori_loop(0, tb, issue, 0, unroll=8)

    def drain(j, carry):
        for k in range(TOP_K):
            pltpu.make_async_copy(h_ref.at[pl.ds(j, 1)], xb_ref.at[pl.ds(0, 1)], sem).wait()
        return carry

    lax.fori_loop(0, tb, drain, 0, unroll=8)


def _combine_kernel(pos_ref, yb_ref, x_ref, g_ref, gates_ref, o_ref, buf, sem):
    tb = x_ref.shape[0]
    base = pl.program_id(0) * (tb * TOP_K)

    def issue(j, carry):
        for k in range(TOP_K):
            src = pos_ref[base + j * TOP_K + k]
            pltpu.make_async_copy(yb_ref.at[pl.ds(src, 1)], buf.at[k, pl.ds(j, 1)], sem).start()
        return carry

    lax.fori_loop(0, tb, issue, 0, unroll=8)

    def drain(j, carry):
        for k in range(TOP_K):
            pltpu.make_async_copy(yb_ref.at[pl.ds(0, 1)], buf.at[k, pl.ds(j, 1)], sem).wait()
        return carry

    lax.fori_loop(0, tb, drain, 0, unroll=8)
    gates = gates_ref[...]
    f = gates[:, 0:1] * buf[0]
    for k in range(1, TOP_K):
        f = f + gates[:, k:k + 1] * buf[k]
    o_ref[...] = x_ref[...] + g_ref[0] * f


def _expert_up_kernel(be_ref, nu_ref, x_ref, wg_ref, wl_ref, bg_ref, bl_ref, o_ref, wg_s, wl_s):
    r = pl.program_id(1)

    @pl.when(r < nu_ref[0])
    def _():
        prev = be_ref[jnp.maximum(r - 1, 0)]

        @pl.when((r == 0) | (be_ref[r] != prev))
        def _():
            wg_s[...] = wg_ref[...].astype(BF16)
            wl_s[...] = wl_ref[...].astype(BF16)

        lo, hi = _unpack_halves(x_ref[...])
        half = lo.shape[1]
        glu = _dot(lo, wg_s[:half, :]) + _dot(hi, wg_s[half:, :]) + bg_ref[...]
        lin = _dot(lo, wl_s[:half, :]) + _dot(hi, wl_s[half:, :]) + bl_ref[...]
        glu = jnp.minimum(glu, SWIGLU_LIMIT)
        lin = jnp.clip(lin, -SWIGLU_LIMIT, SWIGLU_LIMIT)
        o_ref[...] = (glu * _sigmoid(SWIGLU_ALPHA * glu) * (lin + 1.0)).astype(o_ref.dtype)

    @pl.when(r >= nu_ref[0])
    def _():
        o_ref[...] = jnp.zeros_like(o_ref)


def _expert_down_kernel(be_ref, nu_ref, a_ref, w_ref, b_ref, o_ref, w_s):
    r = pl.program_id(1)

    @pl.when(r < nu_ref[0])
    def _():
        prev = be_ref[jnp.maximum(r - 1, 0)]

        @pl.when((r == 0) | (be_ref[r] != prev))
        def _():
            w_s[...] = w_ref[...].astype(BF16)

        o_ref[...] = _dot(a_ref[...], w_s[...]) + b_ref[...]

    @pl.when(r >= nu_ref[0])
    def _():
        o_ref[...] = jnp.zeros_like(o_ref)


def expert_ffn(block_exp, n_used, xb, w1, b1, w2, b2, layer):
    n_rows, half = xb.shape
    d = 2 * half
    n_exp, _, f2 = w1.shape[1:]
    f = f2 // 2
    n_blocks = n_rows // MOE_ROWS
    tf = min(512, f)
    nf = f // tf
    b1r = b1.reshape(b1.shape[0], n_exp, 1, f2)
    b2r = b2.reshape(b2.shape[0], n_exp, 1, d)
    act = pl.pallas_call(
        _expert_up_kernel,
        grid_spec=pltpu.PrefetchScalarGridSpec(
            num_scalar_prefetch=2,
            grid=(nf, n_blocks),
            in_specs=[
                pl.BlockSpec((MOE_ROWS, half), lambda j, r, be, nu: (r, 0)),
                pl.BlockSpec((None, None, d, tf), lambda j, r, be, nu: (layer, be[r], 0, j)),
                pl.BlockSpec((None, None, d, tf), lambda j, r, be, nu: (layer, be[r], 0, nf + j)),
                pl.BlockSpec((None, None, 1, tf), lambda j, r, be, nu: (layer, be[r], 0, j)),
                pl.BlockSpec((None, None, 1, tf), lambda j, r, be, nu: (layer, be[r], 0, nf + j)),
            ],
            out_specs=pl.BlockSpec((MOE_ROWS, tf), lambda j, r, be, nu: (r, j)),
            scratch_shapes=[pltpu.VMEM((d, tf), BF16), pltpu.VMEM((d, tf), BF16)],
        ),
        out_shape=jax.ShapeDtypeStruct((n_rows, f), BF16),
        compiler_params=_cparams("arbitrary", "arbitrary"),
        name="expert_up",
    )(block_exp, n_used, xb, w1, w1, b1r, b1r)
    tn = min(512, d)
    return pl.pallas_call(
        _expert_down_kernel,
        grid_spec=pltpu.PrefetchScalarGridSpec(
            num_scalar_prefetch=2,
            grid=(d // tn, n_blocks),
            in_specs=[
                pl.BlockSpec((MOE_ROWS, f), lambda j, r, be, nu: (r, 0)),
                pl.BlockSpec((None, None, f, tn), lambda j, r, be, nu: (layer, be[r], 0, j)),
                pl.BlockSpec((None, None, 1, tn), lambda j, r, be, nu: (layer, be[r], 0, j)),
            ],
            out_specs=pl.BlockSpec((MOE_ROWS, tn), lambda j, r, be, nu: (r, j)),
            scratch_shapes=[pltpu.VMEM((f, tn), BF16)],
        ),
        out_shape=jax.ShapeDtypeStruct((n_rows, d), F32),
        compiler_params=_cparams("arbitrary", "arbitrary"),
        name="expert_down",
    )(block_exp, n_used, act, w2, b2r)


def _blocked_cumsum(onehot):
    n, e = onehot.shape
    blk = math.gcd(n, 512)
    x = onehot.reshape(n // blk, blk, e).astype(BF16)
    tril = jnp.tril(jnp.ones((blk, blk), BF16))
    within = jnp.einsum("ij,bje->bie", tril, x, preferred_element_type=F32).astype(jnp.int32)
    totals = within[:, -1, :]
    offsets = jnp.cumsum(totals, axis=0) - totals
    return (within + offsets[:, None, :]).reshape(n, e)


def moe_residual(x, hp, logits, gate, w1, b1, w2, b2, layer, rows_lat, seq):
    n_tok, half = hp.shape
    d = 2 * half
    n_exp = w1.shape[1]
    top_val, top_idx = lax.top_k(logits, TOP_K)
    gates = jax.nn.softmax(top_val, axis=-1)
    n_assign = n_tok * TOP_K
    flat_e = top_idx.reshape(-1)
    onehot = (flat_e[:, None] == jnp.arange(n_exp, dtype=flat_e.dtype)[None, :]).astype(jnp.int32)
    csum = _blocked_cumsum(onehot)
    counts = csum[-1]
    padded = (counts + MOE_ROWS - 1) // MOE_ROWS * MOE_ROWS
    pend = jnp.cumsum(padded)
    pstart = pend - padded
    pos = jnp.sum(onehot * (csum - 1 + pstart[None, :]), axis=1).astype(jnp.int32)
    n_blocks = -(-n_assign // MOE_ROWS) + n_exp
    n_rows = n_blocks * MOE_ROWS
    block_start = jnp.arange(n_blocks, dtype=jnp.int32) * MOE_ROWS
    block_exp = jnp.minimum(jnp.sum((block_start[:, None] >= pend[None, :]).astype(jnp.int32), axis=1), n_exp - 1)
    n_used = (pend[-1] // MOE_ROWS).astype(jnp.int32).reshape(1)

    tb = math.gcd(128, _stream_tile(n_tok, rows_lat, seq))
    xb = pl.pallas_call(
        _dispatch_kernel,
        grid_spec=pltpu.PrefetchScalarGridSpec(
            num_scalar_prefetch=1,
            grid=(n_tok // tb,),
            in_specs=[pl.BlockSpec((tb, half), lambda i, pos: (i, 0)), pl.BlockSpec(memory_space=pl.ANY)],
            out_specs=pl.BlockSpec(memory_space=pl.ANY),
            scratch_shapes=[pltpu.SemaphoreType.DMA(())],
        ),
        out_shape=jax.ShapeDtypeStruct((n_rows, half), jnp.uint32),
        input_output_aliases={2: 0},
        compiler_params=_cparams("arbitrary"),
        name="expert_dispatch",
    )(pos, hp, jnp.zeros((n_rows, half), jnp.uint32))
    yb = expert_ffn(block_exp.astype(jnp.int32), n_used, xb, w1, b1, w2, b2, layer)
    n_batch = gate.shape[0] - 1
    idx = functools.partial(_stream_index, tm=tb, rows_lat=rows_lat, seq=seq, n_batch=n_batch)
    return pl.pallas_call(
        _combine_kernel,
        grid_spec=pltpu.PrefetchScalarGridSpec(
            num_scalar_prefetch=1,
            grid=(n_tok // tb,),
            in_specs=[pl.BlockSpec(memory_space=pl.ANY),
                      pl.BlockSpec((tb, d), lambda i, pos: (i, 0)),
                      pl.BlockSpec((1, 1, d), lambda i, pos: (idx(i), 0, 0)),
                      pl.BlockSpec((tb, TOP_K), lambda i, pos: (i, 0))],
            out_specs=pl.BlockSpec((tb, d), lambda i, pos: (i, 0)),
            scratch_shapes=[pltpu.VMEM((TOP_K, tb, d), F32), pltpu.SemaphoreType.DMA(())],
        ),
        out_shape=jax.ShapeDtypeStruct((n_tok, d), F32),
        compiler_params=_cparams("arbitrary"),
        name="expert_combine",
    )(pos, yb, x, gate, gates)


def _layer(i, x, c_all, p, n_batch, seq, n_ctx, tables, with_ctx):
    d = x.shape[1]
    rows_lat = n_batch * seq
    rows_all = rows_lat + n_batch * n_ctx
    n_out = rows_all if with_ctx else rows_lat
    n_stream = n_batch + 1

    mod = matmul(c_all, p["w_mod"], i, c_all.shape[0], c_all.shape[0], 512, F32, pre="silu", name="modulation")
    mod = (mod[:n_stream] + p["b_mod"][i]).reshape(n_stream, 6, 1, d)
    sh1, sc1, g1, sh2, sc2, g2 = (mod[:, k] for k in range(6))

    h1 = norm_mod(x, p["g_mix"][i], sc1, sh1, rows_all, rows_lat, seq, BF16)
    in_width = p["w_in"].shape[2]

    n_g = p["ssm_lambda_re"].shape[2]
    ssm_w = n_g * SSM_GROUP
    n_wh = p["win_sink"].shape[1]
    win_q = n_wh * WIN_HEAD_DIM
    win_kv = WIN_KV_HEADS * WIN_HEAD_DIM
    n_dh = (in_width - ssm_w - win_q - 2 * win_kv - 3 * d) // (2 * 2 * DIFF_QK_DIM + DIFF_V_DIM)
    diff_w = n_dh * 2 * DIFF_QK_DIM
    widths = dict(u=ssm_w, qw=win_q, kw=win_kv, vw=win_kv, qd=diff_w, kd=diff_w, vd=diff_w, gates=3 * d)
    src, c0 = {}, 0
    for name in ("u", "qw", "kw", "vw", "qd", "kd", "vd", "gates"):
        src[name] = c0
        c0 += widths[name]
    order = ("u", "qw", "qd", "kd", "vd", "gates", "kw", "vw")
    w_bf = jnp.concatenate([p["w_in"][i][:, src[n]:src[n] + widths[n]] for n in order], axis=1).astype(BF16)
    modes = dict(u=("plain", 1.0, F32), qw=("rope", WIN_HEAD_DIM ** -0.5, BF16), kw=("rope", 1.0, BF16),
                 vw=("plain", 1.0, BF16), qd=("rope", DIFF_QK_DIM ** -0.5, BF16), kd=("rope", 1.0, BF16),
                 vd=("plain", 1.0, BF16), gates=("sigmoid", 1.0, BF16))
    seg, c0 = {}, 0
    for name in order:
        mode, scale, dtype = modes[name]
        seg[name] = in_proj(h1, w_bf, c0, widths[name], mode, scale, dtype, tables, rows_lat, seq)
        c0 += widths[name]

    mats = s5_matrices(p["ssm_lambda_re"][i], p["ssm_lambda_im"][i], p["ssm_log_dt"][i], p["ssm_b_re"][i],
                       p["ssm_b_im"][i], p["ssm_c_re"][i], p["ssm_c_im"][i], p["ssm_d"][i], S5_CHUNK)
    y = s5_mixer(seg["u"], mats, rows_all, n_batch, seq, n_ctx)
    o_ssm = s5_glu(y, p["w_glu"], p["b_glu"], i, n_out)

    sink = p["win_sink"][i].astype(F32)
    o_win = window_attention(sink, seg["qw"], seg["kw"], seg["vw"], n_batch, seq, n_ctx, n_wh, rows_lat, False)

    lam_p = p["diff_lambda"][i].astype(F32)
    lambda_init = 0.8 - 0.6 * math.exp(-0.3 * i)
    lam = (jnp.exp(jnp.sum(lam_p[0] * lam_p[1])) - jnp.exp(jnp.sum(lam_p[2] * lam_p[3])) + lambda_init).reshape(1)
    gain = p["diff_subln"][i].astype(F32).reshape(1, DIFF_V_DIM)
    o_diff = diff_attention(lam, gain, seg["qd"], seg["kd"], seg["vd"], n_batch, seq, n_ctx, n_dh, rows_lat, False,
                            1.0 - lambda_init)
    if with_ctx:
        o_win_c = window_attention(sink, seg["qw"], seg["kw"], seg["vw"], n_batch, seq, n_ctx, n_wh, rows_lat, True)
        o_diff_c = diff_attention(lam, gain, seg["qd"], seg["kd"], seg["vd"], n_batch, seq, n_ctx, n_dh, rows_lat, True,
                                  1.0 - lambda_init)
        o_win = jnp.concatenate([o_win, o_win_c], axis=0)
        o_diff = jnp.concatenate([o_diff, o_diff_c], axis=0)

    merged = gated_merge(o_ssm, o_win, o_diff, p["w_branch_ssm"], p["w_branch_win"], p["w_branch_diff"], seg["gates"],
                         i, n_out, d)
    x = out_proj_residual(merged, p["w_out"], x, g1, i, n_out, rows_lat, seq)

    hp, logits = norm_mod(x, p["g_ffn"][i], sc2, sh2, n_out, rows_lat, seq, BF16,
                          router=(p["w_router"][i], p["b_router"][i]))
    return moe_residual(x, hp, logits, g2, p["w_exp1"], p["b_exp1"], p["w_exp2"], p["b_exp2"], i, rows_lat, seq)


def kernel(x, c, ctx, c_ctx, w_mod, b_mod, g_mix, g_ffn, w_in, ssm_lambda_re, ssm_lambda_im, ssm_log_dt, ssm_b_re, ssm_b_im, ssm_c_re, ssm_c_im, ssm_d, w_glu, b_glu, win_sink, diff_lambda, diff_subln, w_branch_ssm, w_branch_win, w_branch_diff, w_out, w_router, b_router, w_exp1, b_exp1, w_exp2, b_exp2, g_final):
    n_batch, seq, d = x.shape
    n_ctx = ctx.shape[1]
    depth = w_mod.shape[0]
    p = dict(w_mod=w_mod, b_mod=b_mod, g_mix=g_mix, g_ffn=g_ffn, w_in=w_in, ssm_lambda_re=ssm_lambda_re,
             ssm_lambda_im=ssm_lambda_im, ssm_log_dt=ssm_log_dt, ssm_b_re=ssm_b_re, ssm_b_im=ssm_b_im,
             ssm_c_re=ssm_c_re, ssm_c_im=ssm_c_im, ssm_d=ssm_d, w_glu=w_glu, b_glu=b_glu, win_sink=win_sink,
             diff_lambda=diff_lambda, diff_subln=diff_subln, w_branch_ssm=w_branch_ssm, w_branch_win=w_branch_win,
             w_branch_diff=w_branch_diff, w_out=w_out, w_router=w_router, b_router=b_router, w_exp1=w_exp1,
             b_exp1=b_exp1, w_exp2=w_exp2, b_exp2=b_exp2)
    rows_lat = n_batch * seq
    rows = jnp.concatenate([x.reshape(rows_lat, d), ctx.reshape(n_batch * n_ctx, d)], axis=0)
    c_all = jnp.concatenate([c, c_ctx[None, :], jnp.zeros((8 - (n_batch + 1) % 8, d), F32)], axis=0)
    tables = rope_tables(seq, WIN_HEAD_DIM)
    for i in range(depth):
        with_ctx = i < depth - 1
        new = _layer(i, rows, c_all, p, n_batch, seq, n_ctx, tables, with_ctx)
        rows = new if with_ctx else jnp.concatenate([new, rows[rows_lat:]], axis=0)
    no_mod = jnp.zeros((n_batch + 1, 1, d), F32)
    out = norm_mod(rows, g_final, no_mod, no_mod, rows_lat, rows_lat, seq, F32)
    return out.reshape(n_batch, seq, d)
```

```python
import functools
import math

import jax
import jax.numpy as jnp
import numpy as np
from jax import lax
from jax.experimental import pallas as pl
from jax.experimental.pallas import tpu as pltpu

F32 = jnp.float32
BF16 = jnp.bfloat16

GRID_W = 64
SSM_GROUP = 16
WIN_KV_HEADS = 2
WIN_HEAD_DIM = 64
WINDOW = 128
DIFF_QK_DIM = 64
DIFF_V_DIM = 2 * DIFF_QK_DIM
ATTN_BLOCK = 128
ROPE_BASE = 10000.0
TOP_K = 4
SWIGLU_LIMIT = 7.0
SWIGLU_ALPHA = 1.702
NORM_EPS = 1e-6
SUBLN_EPS = 1e-5
NEG_INF = -1e30

LANE = 128
VMEM_LIMIT_BYTES = 56 * 1024 * 1024
S5_CHUNK = 32
MOE_LOAD_MARGIN = 1.06
ROW_TILE = 512


def _cparams(*sem):
    return pltpu.CompilerParams(dimension_semantics=sem, vmem_limit_bytes=VMEM_LIMIT_BYTES)


def _dot(a, b):
    return jnp.dot(a, b, preferred_element_type=F32)


def _dot_nt(a, b):
    return lax.dot_general(a, b, (((1,), (1,)), ((), ())), preferred_element_type=F32)


def _sigmoid(x):
    return 1.0 / (1.0 + jnp.exp(-x))


def _row_tile(n_rows, cap):
    best = 16
    for t in range(16, cap + 1, 16):
        if n_rows % t == 0:
            best = t
    return best


def _norm_mod_kernel(x_ref, g_ref, sc_ref, sh_ref, o_ref):
    x = x_ref[...]
    y = x * lax.rsqrt(jnp.mean(x * x, axis=-1, keepdims=True) + NORM_EPS) * g_ref[...]
    o_ref[...] = (y * (1.0 + sc_ref[0]) + sh_ref[0]).astype(o_ref.dtype)


def _norm_router_kernel(x_ref, g_ref, sc_ref, sh_ref, wr_ref, br_ref, o_ref, lg_ref):
    x = x_ref[...]
    y = x * lax.rsqrt(jnp.mean(x * x, axis=-1, keepdims=True) + NORM_EPS) * g_ref[...]
    h = y * (1.0 + sc_ref[0]) + sh_ref[0]
    o_ref[...] = _pack_halves(h)
    lg_ref[...] = jnp.dot(h, wr_ref[...], preferred_element_type=F32, precision=lax.Precision.HIGHEST) + br_ref[...]


def _stream_tile(n_rows, rows_lat, seq):
    tm = math.gcd(ROW_TILE, seq)
    return math.gcd(tm, n_rows - rows_lat) if n_rows > rows_lat else tm


def _stream_index(i, tm, rows_lat, seq, n_batch):
    r = i * tm
    return jnp.where(r >= rows_lat, n_batch, r // seq)


def norm_mod(x, gain, sc, sh, n_rows, rows_lat, seq, out_dtype, router=None):
    d = x.shape[1]
    tm = _stream_tile(n_rows, rows_lat, seq)
    n_batch = sc.shape[0] - 1
    idx = functools.partial(_stream_index, tm=tm, rows_lat=rows_lat, seq=seq, n_batch=n_batch)
    in_specs = [
        pl.BlockSpec((tm, d), lambda i: (i, 0)),
        pl.BlockSpec((1, d), lambda i: (0, 0)),
        pl.BlockSpec((1, 1, d), lambda i: (idx(i), 0, 0)),
        pl.BlockSpec((1, 1, d), lambda i: (idx(i), 0, 0)),
    ]
    if router is None:
        return pl.pallas_call(
            _norm_mod_kernel,
            grid=(n_rows // tm,),
            in_specs=in_specs,
            out_specs=pl.BlockSpec((tm, d), lambda i: (i, 0)),
            out_shape=jax.ShapeDtypeStruct((n_rows, d), out_dtype),
            compiler_params=_cparams("parallel"),
            name="norm_mod",
        )(x, gain.reshape(1, d), sc, sh)
    w_router, b_router = router
    n_exp = w_router.shape[1]
    return pl.pallas_call(
        _norm_router_kernel,
        grid=(n_rows // tm,),
        in_specs=in_specs + [
            pl.BlockSpec((d, n_exp), lambda i: (0, 0)),
            pl.BlockSpec((1, n_exp), lambda i: (0, 0)),
        ],
        out_specs=[pl.BlockSpec((tm, d // 2), lambda i: (i, 0)), pl.BlockSpec((tm, n_exp), lambda i: (i, 0))],
        out_shape=[jax.ShapeDtypeStruct((n_rows, d // 2), jnp.uint32), jax.ShapeDtypeStruct((n_rows, n_exp), F32)],
        compiler_params=_cparams("parallel"),
        name="norm_router",
    )(x, gain.reshape(1, d), sc, sh, w_router, b_router.reshape(1, n_exp))


def _mm_kernel(x_ref, w_ref, o_ref, *, pre):
    x = x_ref[...]
    if pre == "silu":
        x = x * _sigmoid(x)
    o_ref[...] = _dot(x.astype(BF16), w_ref[...].astype(BF16)).astype(o_ref.dtype)


def matmul(x, w_stack, layer, n_rows, tm, tn, out_dtype, pre=None, name="matmul"):
    k = x.shape[1]
    n = w_stack.shape[2]
    return pl.pallas_call(
        functools.partial(_mm_kernel, pre=pre),
        grid=(n_rows // tm, n // tn),
        in_specs=[
            pl.BlockSpec((tm, k), lambda i, j: (i, 0)),
            pl.BlockSpec((None, k, tn), lambda i, j: (layer, 0, j)),
        ],
        out_specs=pl.BlockSpec((tm, tn), lambda i, j: (i, j)),
        out_shape=jax.ShapeDtypeStruct((n_rows, n), out_dtype),
        compiler_params=_cparams("parallel", "arbitrary"),
        name=name,
    )(x, w_stack)


def _in_proj_kernel(x_ref, w_ref, *refs, mode, scale):
    o_ref = refs[-1]
    acc = _dot(x_ref[...], w_ref[...])
    if mode == "sigmoid":
        o_ref[...] = _sigmoid(acc).astype(o_ref.dtype)
        return
    if mode == "plain":
        o_ref[...] = (acc if scale == 1.0 else acc * scale).astype(o_ref.dtype)
        return
    cos_ref, sin_ref = refs[0], refs[1]
    lane = lax.broadcasted_iota(jnp.int32, (1, LANE), 1)
    first_half = (lane % 32) < 16
    cos = cos_ref[...] if scale == 1.0 else cos_ref[...] * scale
    sin = sin_ref[...] if scale == 1.0 else sin_ref[...] * scale
    for c in range(acc.shape[1] // LANE):
        x = acc[:, c * LANE:(c + 1) * LANE]
        partner = jnp.where(first_half, pltpu.roll(x, LANE - 16, axis=1), pltpu.roll(x, 16, axis=1))
        o_ref[:, c * LANE:(c + 1) * LANE] = (x * cos + partner * sin).astype(o_ref.dtype)


def in_proj(h, w, col0, width, mode, scale, out_dtype, tables, rows_lat, seq):
    n_rows, k = h.shape
    tm = _stream_tile(n_rows, rows_lat, seq)
    tn = math.gcd(1024, width)
    assert col0 % tn == 0
    in_specs = [pl.BlockSpec((tm, k), lambda i, j: (i, 0)),
                pl.BlockSpec((k, tn), lambda i, j: (0, col0 // tn + j))]
    args = [h, w]
    if mode == "rope":
        cos_t, sin_t = tables
        tiles_per_seq = seq // tm
        lat_tiles = rows_lat // tm
        cos_x = jnp.concatenate([cos_t, jnp.ones((tm, LANE), F32)], axis=0)
        sin_x = jnp.concatenate([sin_t, jnp.zeros((tm, LANE), F32)], axis=0)
        tab = pl.BlockSpec((tm, LANE), lambda i, j: (jnp.where(i < lat_tiles, i % tiles_per_seq, tiles_per_seq), 0))
        in_specs += [tab, tab]
        args += [cos_x, sin_x]
    return pl.pallas_call(
        functools.partial(_in_proj_kernel, mode=mode, scale=scale),
        grid=(n_rows // tm, width // tn),
        in_specs=in_specs,
        out_specs=pl.BlockSpec((tm, tn), lambda i, j: (i, j)),
        out_shape=jax.ShapeDtypeStruct((n_rows, width), out_dtype),
        compiler_params=_cparams("parallel", "arbitrary"),
        name="in_proj_" + mode,
    )(*args)


def rope_tables(seq, head_dim):
    assert head_dim == 64
    rows = seq // GRID_W
    r, col = jnp.meshgrid(jnp.arange(rows, dtype=F32), jnp.arange(GRID_W, dtype=F32), indexing="ij")
    half = head_dim // 2
    inv_freq = ROPE_BASE ** (-jnp.arange(0, half, 2, dtype=F32) / half)
    ang_r = r.reshape(-1)[:, None] * inv_freq[None, :]
    ang_c = col.reshape(-1)[:, None] * inv_freq[None, :]
    cos = jnp.concatenate([jnp.cos(ang_r), jnp.cos(ang_r), jnp.cos(ang_c), jnp.cos(ang_c)], axis=-1)
    sin = jnp.concatenate([-jnp.sin(ang_r), jnp.sin(ang_r), -jnp.sin(ang_c), jnp.sin(ang_c)], axis=-1)
    return jnp.tile(cos, (1, LANE // head_dim)), jnp.tile(sin, (1, LANE // head_dim))


def _win_attn_kernel(sink_ref, q_ref, *refs, n_heads, band, seq):
    if band:
        kp_ref, kc_ref, kn_ref, vp_ref, vc_ref, vn_ref, kx_ref, vx_ref, o_ref = refs
    else:
        kx_ref, vx_ref, o_ref = refs
    blk = q_ref.shape[0]
    dh = WIN_HEAD_DIM
    grp = n_heads // WIN_KV_HEADS
    lane = lax.broadcasted_iota(jnp.int32, (1, LANE), 1)
    rows = grp * blk
    if band:
        n = pl.program_id(1)
        k_all = jnp.concatenate([kp_ref[...], kc_ref[...], kn_ref[...], kx_ref[...]], axis=0)
        v_all = jnp.concatenate([vp_ref[...], vc_ref[...], vn_ref[...], vx_ref[...]], axis=0)
        n_keys = k_all.shape[0]
        col = lax.broadcasted_iota(jnp.int32, (rows, n_keys), 1)
        qpos = n * blk + (lax.broadcasted_iota(jnp.int32, (rows, n_keys), 0) & (blk - 1))
        kpos = (n - 1) * blk + col
        mask = (col >= 3 * blk) | ((jnp.abs(qpos - kpos) <= WINDOW) & (kpos >= 0) & (kpos < seq))
    else:
        k_all, v_all = kx_ref[...], vx_ref[...]
        n_keys = k_all.shape[0]
    ones_col = (lax.broadcasted_iota(jnp.int32, (n_keys, LANE), 1) == 0).astype(BF16)
    v_aug = jnp.concatenate([v_all, ones_col], axis=1)
    outs = [None] * n_heads
    for kvh in range(WIN_KV_HEADS):
        keep = (lane >= kvh * dh) & (lane < (kvh + 1) * dh)
        q_parts, sink_parts = [], []
        for g in range(grp):
            h = kvh * grp + g
            c = (h * dh) // LANE
            qc = q_ref[:, c * LANE:(c + 1) * LANE]
            if (h * dh) % LANE != kvh * dh:
                qc = jnp.concatenate([qc[:, dh:], qc[:, :dh]], axis=1)
            q_parts.append(jnp.where(keep, qc, jnp.zeros_like(qc)))
            sink_parts.append(jnp.full((blk, 1), sink_ref[h], F32))
        qs = jnp.concatenate(q_parts, axis=0)
        sink = jnp.concatenate(sink_parts, axis=0)
        s = _dot_nt(qs, k_all)
        if band:
            s = jnp.where(mask, s, NEG_INF)
        m = jnp.maximum(jnp.max(s, axis=-1, keepdims=True), sink)
        acc = _dot(jnp.exp(s - m).astype(BF16), v_aug)
        inv = 1.0 / (acc[:, LANE:LANE + 1] + jnp.exp(sink - m))
        o = acc[:, kvh * dh:(kvh + 1) * dh] * inv
        for g in range(grp):
            outs[kvh * grp + g] = o[g * blk:(g + 1) * blk, :]
    o_ref[...] = jnp.concatenate(outs, axis=1).astype(o_ref.dtype)


def window_attention(sink, q, k, v, n_batch, seq, n_ctx, n_heads, rows_lat, ctx_queries):
    blk = ATTN_BLOCK
    seq_q = n_ctx if ctx_queries else seq
    nb = seq_q // blk
    q0 = rows_lat // blk if ctx_queries else 0
    x0 = rows_lat // n_ctx
    qw = n_heads * WIN_HEAD_DIM
    kvw = WIN_KV_HEADS * WIN_HEAD_DIM
    in_specs = [
        pl.BlockSpec(memory_space=pltpu.SMEM),
        pl.BlockSpec((blk, qw), lambda b, n: (q0 + b * nb + n, 0)),
    ]
    args = [sink, q]
    if not ctx_queries:
        band_specs = [
            pl.BlockSpec((blk, kvw), lambda b, n: (b * nb + jnp.maximum(n - 1, 0), 0)),
            pl.BlockSpec((blk, kvw), lambda b, n: (b * nb + n, 0)),
            pl.BlockSpec((blk, kvw), lambda b, n: (b * nb + jnp.minimum(n + 1, nb - 1), 0)),
        ]
        in_specs += band_specs + band_specs
        args += [k, k, k, v, v, v]
    in_specs += [pl.BlockSpec((n_ctx, kvw), lambda b, n: (x0 + b, 0))] * 2
    args += [k, v]
    return pl.pallas_call(
        functools.partial(_win_attn_kernel, n_heads=n_heads, band=not ctx_queries, seq=seq),
        grid=(n_batch, nb),
        in_specs=in_specs,
        out_specs=pl.BlockSpec((blk, qw), lambda b, n: (b * nb + n, 0)),
        out_shape=jax.ShapeDtypeStruct((n_batch * seq_q, qw), BF16),
        compiler_params=_cparams("parallel", "parallel"),
        name="context_window_attention" if ctx_queries else "window_attention",
    )(*args)


def _diff_attn_kernel(lam_ref, q_ref, gain_ref, *refs, with_lat, post_scale):
    if with_lat:
        kl_ref, vl_ref, kx_ref, vx_ref, o_ref = refs
    else:
        kx_ref, vx_ref, o_ref = refs
    lam = lam_ref[0]
    q = q_ref[...]
    lane = lax.broadcasted_iota(jnp.int32, (1, LANE), 1)
    zero = jnp.zeros_like(q)
    kx = kx_ref[...]

    def softmax_parts(qm):
        s_x = _dot_nt(qm, kx)
        m = jnp.max(s_x, axis=-1, keepdims=True)
        if with_lat:
            s_l = _dot_nt(qm, kl_ref[...])
            m = jnp.maximum(m, jnp.max(s_l, axis=-1, keepdims=True))
            e_l = jnp.exp(s_l - m)
        else:
            e_l = None
        e_x = jnp.exp(s_x - m)
        den = jnp.sum(e_x, axis=-1, keepdims=True)
        if with_lat:
            den = den + jnp.sum(e_l, axis=-1, keepdims=True)
        return e_l, e_x, 1.0 / den

    e1l, e1x, inv1 = softmax_parts(jnp.where(lane < DIFF_QK_DIM, q, zero))
    e2l, e2x, inv2 = softmax_parts(jnp.where(lane >= DIFF_QK_DIM, q, zero))
    w2 = lam * inv2
    o = _dot((e1x * inv1 - e2x * w2).astype(BF16), vx_ref[...])
    if with_lat:
        o = o + _dot((e1l * inv1 - e2l * w2).astype(BF16), vl_ref[...])
    o = o * lax.rsqrt(jnp.mean(o * o, axis=-1, keepdims=True) + SUBLN_EPS) * gain_ref[...]
    o_ref[...] = (o * post_scale).astype(o_ref.dtype)


def diff_attention(lam, gain, q, k, v, n_batch, seq, n_ctx, n_heads, rows_lat, ctx_queries, post_scale):
    seq_q = n_ctx if ctx_queries else seq
    tq = min(256, seq_q)
    nq = seq_q // tq
    q0 = rows_lat // tq if ctx_queries else 0
    x0 = rows_lat // n_ctx
    in_specs = [
        pl.BlockSpec(memory_space=pltpu.SMEM),
        pl.BlockSpec((tq, LANE), lambda b, h, i: (q0 + b * nq + i, h)),
        pl.BlockSpec((1, LANE), lambda b, h, i: (0, 0)),
    ]
    args = [lam, q, gain]
    if not ctx_queries:
        in_specs += [pl.BlockSpec((seq, LANE), lambda b, h, i: (b, h))] * 2
        args += [k, v]
    in_specs += [pl.BlockSpec((n_ctx, LANE), lambda b, h, i: (x0 + b, h))] * 2
    args += [k, v]
    return pl.pallas_call(
        functools.partial(_diff_attn_kernel, with_lat=not ctx_queries, post_scale=post_scale),
        grid=(n_batch, n_heads, nq),
        in_specs=in_specs,
        out_specs=pl.BlockSpec((tq, LANE), lambda b, h, i: (b * nq + i, h)),
        out_shape=jax.ShapeDtypeStruct((n_batch * seq_q, n_heads * LANE), BF16),
        compiler_params=_cparams("parallel", "parallel", "arbitrary"),
        name="context_diff_attention" if ctx_queries else "diff_attention",
    )(*args)


def s5_matrices(lam_re, lam_im, log_dt, b_re, b_im, c_re, c_im, d_skip, t_chunk):
    n_dir, n_g, n_p = lam_re.shape
    n_h = b_re.shape[-1]
    lam_re = jnp.minimum(lam_re.astype(F32), -1e-4)
    lam_im = lam_im.astype(F32)
    dt = jnp.exp(log_dt.astype(F32))[..., None]
    mag = jnp.exp(lam_re * dt)
    a_re = mag * jnp.cos(lam_im * dt)
    a_im = mag * jnp.sin(lam_im * dt)
    den = lam_re * lam_re + lam_im * lam_im
    k_re = ((a_re - 1.0) * lam_re + a_im * lam_im) / den
    k_im = (a_im * lam_re - (a_re - 1.0) * lam_im) / den
    b_re = b_re.astype(F32)
    b_im = b_im.astype(F32)
    bb_re = k_re[..., None] * b_re - k_im[..., None] * b_im
    bb_im = k_re[..., None] * b_im + k_im[..., None] * b_re
    c_re = c_re.astype(F32)
    c_im = c_im.astype(F32)
    ldt_re = lam_re * dt
    ldt_im = lam_im * dt
    steps = jnp.arange(t_chunk, dtype=F32)

    def power(d, t, p_last):
        lr = ldt_re[d][:, None, :] if p_last else ldt_re[d][:, :, None]
        li = ldt_im[d][:, None, :] if p_last else ldt_im[d][:, :, None]
        tt = t[None, :, None] if p_last else t[None, None, :]
        mag_t = jnp.exp(lr * tt)
        return mag_t * jnp.cos(li * tt), mag_t * jnp.sin(li * tt)

    def c_times_power(d, t):
        pr, pi = power(d, t, False)
        cr = jnp.swapaxes(c_re[d], 1, 2)[:, :, None, :]
        ci = jnp.swapaxes(c_im[d], 1, 2)[:, :, None, :]
        re = cr * pr[..., None] - ci * pi[..., None]
        im = cr * pi[..., None] + ci * pr[..., None]
        return jnp.concatenate([re, -im], axis=1).reshape(n_g, 2 * n_p, t_chunk * n_h)

    def b_times_power(d, t):
        pr, pi = power(d, t, True)
        br = jnp.swapaxes(bb_re[d], 1, 2)[:, None, :, :]
        bi = jnp.swapaxes(bb_im[d], 1, 2)[:, None, :, :]
        re = pr[:, :, None, :] * br - pi[:, :, None, :] * bi
        im = pr[:, :, None, :] * bi + pi[:, :, None, :] * br
        return jnp.concatenate([re, im], axis=-1).reshape(n_g, t_chunk * n_h, 2 * n_p)

    ca_lag = jnp.stack([c_times_power(0, steps), c_times_power(1, t_chunk - 1 - steps)])
    bbt = jnp.stack([jnp.concatenate([jnp.swapaxes(bb_re[d], 1, 2), jnp.swapaxes(bb_im[d], 1, 2)], axis=-1)
                     for d in range(2)])
    p_mat = jnp.concatenate([b_times_power(0, t_chunk - 1 - steps), b_times_power(1, steps)], axis=-1)
    q_mat = jnp.concatenate([c_times_power(0, steps + 1.0), c_times_power(1, t_chunk - steps)], axis=1)
    d_tile = jnp.tile(d_skip.astype(F32).reshape(n_g, 1, n_h), (1, 1, t_chunk))
    mag_t = jnp.exp(ldt_re * t_chunk)
    at_re, at_im = mag_t * jnp.cos(ldt_im * t_chunk), mag_t * jnp.sin(ldt_im * t_chunk)
    at_mul = jnp.concatenate([at_re, at_re], axis=-1)
    at_swp = jnp.concatenate([-at_im, at_im], axis=-1)
    return ca_lag, bbt, p_mat.astype(BF16), q_mat.astype(BF16), d_tile, at_mul, at_swp


def _s5_local_kernel(x_ref, p_ref, e_ref):
    e_ref[...] = _dot(x_ref[...].astype(BF16), p_ref[...])


def _s5_out_kernel(x_ref, s_ref, ca_ref, bbt_ref, q_ref, d_ref, y_ref, m_ref):
    n_h = bbt_ref.shape[1]
    th = m_ref.shape[0]
    t_chunk = th // n_h
    hi = lax.Precision.HIGHEST
    row_f = jnp.dot(bbt_ref[0], ca_ref[0], preferred_element_type=F32, precision=hi)
    row_b = jnp.dot(bbt_ref[1], ca_ref[1], preferred_element_type=F32, precision=hi)
    lane = lax.broadcasted_iota(jnp.int32, (n_h, th), 1)
    for j in range(t_chunk):
        fwd = jnp.where(lane >= n_h * j, pltpu.roll(row_f, n_h * j, axis=1), 0.0) if j else row_f
        back = n_h * (t_chunk - 1 - j)
        bwd = jnp.where(lane < n_h * (j + 1), pltpu.roll(row_b, th - back, axis=1), 0.0) if back else row_b
        m_ref[n_h * j:n_h * (j + 1), :] = (fwd + bwd).astype(m_ref.dtype)
    x = x_ref[...]
    y_ref[...] = _dot(x.astype(BF16), m_ref[...]) + _dot(s_ref[...].astype(BF16), q_ref[...]) + x * d_ref[...]


def _s5_scan_kernel(ef_ref, eb_ref, mul_ref, swp_ref, sf_ref, sb_ref, *, n_batch, lat_chunks, ctx_chunks):
    half = ef_ref.shape[2] // 2
    mul_f, mul_b = mul_ref[0], mul_ref[1]
    swp_f, swp_b = swp_ref[0], swp_ref[1]
    zero = jnp.zeros(ef_ref.shape[1:], F32)

    def step(c_f, c_b, carry):
        s_f, s_b = carry
        sf_ref[c_f] = s_f
        sb_ref[c_b] = s_b
        s_f = s_f * mul_f + pltpu.roll(s_f, half, axis=1) * swp_f + ef_ref[c_f]
        s_b = s_b * mul_b + pltpu.roll(s_b, half, axis=1) * swp_b + eb_ref[c_b]
        return s_f, s_b

    for b in range(n_batch):
        ctx0 = n_batch * lat_chunks + b * ctx_chunks
        lat0 = b * lat_chunks
        carry = lax.fori_loop(0, ctx_chunks, lambda i, cr: step(ctx0 + i, ctx0 + ctx_chunks - 1 - i, cr), (zero, zero))
        lax.fori_loop(0, lat_chunks, lambda i, cr: step(lat0 + i, lat0 + lat_chunks - 1 - i, cr), carry)


def s5_mixer(proj, mats, n_rows, n_batch, seq, n_ctx):
    ca_lag, bbt, p_mat, q_mat, d_tile, at_mul, at_swp = mats
    n_g, th, _ = p_mat.shape
    n_h = SSM_GROUP
    t_chunk = th // n_h
    p4 = p_mat.shape[2]
    n_p2 = p4 // 2
    width = n_g * n_h
    n_chunks = n_rows // t_chunk
    u = proj[:, :width]
    x = u.reshape(n_chunks, t_chunk, n_g, n_h).transpose(2, 0, 1, 3).reshape(n_g, n_chunks, th)
    e = pl.pallas_call(
        _s5_local_kernel,
        grid=(n_g,),
        in_specs=[pl.BlockSpec((None, n_chunks, th), lambda g: (g, 0, 0)),
                  pl.BlockSpec((None, th, p4), lambda g: (g, 0, 0))],
        out_specs=pl.BlockSpec((n_chunks, p4), lambda g: (0, g)),
        out_shape=jax.ShapeDtypeStruct((n_chunks, n_g * p4), F32),
        compiler_params=_cparams("parallel"),
        name="s5_local_state",
    )(x, p_mat)
    e = e.reshape(n_chunks, n_g, 2, n_p2)
    gb = 16
    spec = pl.BlockSpec((n_chunks, gb, n_p2), lambda g: (0, g, 0))
    tab = pl.BlockSpec((2, gb, n_p2), lambda g: (0, g, 0))
    s_f, s_b = pl.pallas_call(
        functools.partial(_s5_scan_kernel, n_batch=n_batch, lat_chunks=seq // t_chunk, ctx_chunks=n_ctx // t_chunk),
        grid=(n_g // gb,),
        in_specs=[spec, spec, tab, tab],
        out_specs=[spec, spec],
        out_shape=[jax.ShapeDtypeStruct((n_chunks, n_g, n_p2), F32)] * 2,
        compiler_params=_cparams("parallel"),
        name="s5_chunk_scan",
    )(e[:, :, 0], e[:, :, 1], at_mul, at_swp)
    s_in = jnp.stack([s_f, s_b], axis=2).reshape(n_chunks, n_g * p4)
    y = pl.pallas_call(
        _s5_out_kernel,
        grid=(n_g,),
        in_specs=[pl.BlockSpec((None, n_chunks, th), lambda g: (g, 0, 0)),
                  pl.BlockSpec((n_chunks, p4), lambda g: (0, g)),
                  pl.BlockSpec((2, None, p4 // 2, th), lambda g: (0, g, 0, 0)),
                  pl.BlockSpec((2, None, n_h, p4 // 2), lambda g: (0, g, 0, 0)),
                  pl.BlockSpec((None, p4, th), lambda g: (g, 0, 0)),
                  pl.BlockSpec((None, 1, th), lambda g: (g, 0, 0))],
        out_specs=pl.BlockSpec((None, n_chunks, th), lambda g: (g, 0, 0)),
        out_shape=jax.ShapeDtypeStruct((n_g, n_chunks, th), F32),
        scratch_shapes=[pltpu.VMEM((th, th), BF16)],
        compiler_params=_cparams("parallel"),
        name="s5_output",
    )(x, s_in, ca_lag, bbt, q_mat, d_tile)
    return y.reshape(n_g, n_chunks, t_chunk, n_h).transpose(1, 2, 0, 3).reshape(n_rows, width)


def _glu_kernel(y_ref, w_ref, b_ref, o_ref):
    y = y_ref[...]
    gy = 0.5 * y * (1.0 + jnp.tanh(math.sqrt(2.0 / math.pi) * (y + 0.044715 * (y * y * y))))
    z = _dot(gy.astype(BF16), w_ref[...].astype(BF16)) + b_ref[...]
    o_ref[...] = (gy * _sigmoid(z)).astype(o_ref.dtype)


def s5_glu(y, w_glu, b_glu, layer, n_rows):
    width = y.shape[1]
    tm = _row_tile(n_rows, ROW_TILE)
    return pl.pallas_call(
        _glu_kernel,
        grid=(n_rows // tm,),
        in_specs=[pl.BlockSpec((tm, width), lambda i: (i, 0)),
                  pl.BlockSpec((None, width, width), lambda i: (layer, 0, 0)),
                  pl.BlockSpec((None, 1, width), lambda i: (layer, 0, 0))],
        out_specs=pl.BlockSpec((tm, width), lambda i: (i, 0)),
        out_shape=jax.ShapeDtypeStruct((n_rows, width), BF16),
        compiler_params=_cparams("parallel"),
        name="s5_glu",
    )(y, w_glu, b_glu.reshape(b_glu.shape[0], 1, width))


def _merge_kernel(os_ref, ow_ref, od_ref, ws_ref, ww_ref, wd_ref, gs_ref, gw_ref, gd_ref, o_ref):
    m = (gs_ref[...].astype(F32) * _dot(os_ref[...], ws_ref[...].astype(BF16))
         + gw_ref[...].astype(F32) * _dot(ow_ref[...], ww_ref[...].astype(BF16))
         + gd_ref[...].astype(F32) * _dot(od_ref[...], wd_ref[...].astype(BF16)))
    o_ref[...] = m.astype(o_ref.dtype)


def gated_merge(o_ssm, o_win, o_diff, w_s, w_w, w_d, proj, layer, n_rows, d):
    tm = _row_tile(n_rows, 1088)
    tn = 256
    assert d % tn == 0
    g0 = 0
    nd = d // tn

    def branch(arr):
        return pl.BlockSpec((tm, arr.shape[1]), lambda i, j: (i, 0))

    def weight(w):
        return pl.BlockSpec((None, w.shape[1], tn), lambda i, j: (layer, 0, j))

    def gate(k):
        return pl.BlockSpec((tm, tn), lambda i, j: (i, g0 + k * nd + j))

    return pl.pallas_call(
        _merge_kernel,
        grid=(n_rows // tm, nd),
        in_specs=[branch(o_ssm), branch(o_win), branch(o_diff), weight(w_s), weight(w_w), weight(w_d),
                  gate(0), gate(1), gate(2)],
        out_specs=pl.BlockSpec((tm, tn), lambda i, j: (i, j)),
        out_shape=jax.ShapeDtypeStruct((n_rows, d), BF16),
        compiler_params=_cparams("parallel", "arbitrary"),
        name="gated_merge",
    )(o_ssm, o_win, o_diff, w_s, w_w, w_d, proj, proj, proj)


def _out_proj_kernel(m_ref, w_ref, x_ref, g_ref, o_ref):
    o_ref[...] = x_ref[...] + g_ref[0] * _dot(m_ref[...], w_ref[...].astype(BF16))


def out_proj_residual(m, w_out, x, gate, layer, n_rows, rows_lat, seq):
    d = x.shape[1]
    tm = _stream_tile(n_rows, rows_lat, seq)
    tn = min(1024, d)
    n_batch = gate.shape[0] - 1
    idx = functools.partial(_stream_index, tm=tm, rows_lat=rows_lat, seq=seq, n_batch=n_batch)
    return pl.pallas_call(
        _out_proj_kernel,
        grid=(n_rows // tm, d // tn),
        in_specs=[pl.BlockSpec((tm, m.shape[1]), lambda i, j: (i, 0)),
                  pl.BlockSpec((None, m.shape[1], tn), lambda i, j: (layer, 0, j)),
                  pl.BlockSpec((tm, tn), lambda i, j: (i, j)),
                  pl.BlockSpec((1, 1, tn), lambda i, j: (idx(i), 0, j))],
        out_specs=pl.BlockSpec((tm, tn), lambda i, j: (i, j)),
        out_shape=jax.ShapeDtypeStruct((n_rows, d), F32),
        compiler_params=_cparams("parallel", "arbitrary"),
        name="out_proj_residual",
    )(m, w_out, x, gate)


def _pack_halves(h):
    half = h.shape[1] // 2
    lo = lax.bitcast_convert_type(h[:, :half].astype(BF16).astype(F32), jnp.uint32)
    hi = lax.bitcast_convert_type(h[:, half:].astype(BF16).astype(F32), jnp.uint32)
    return (lo >> 16) | (hi & jnp.uint32(0xFFFF0000))


def _unpack_halves(x):
    lo = lax.bitcast_convert_type(x << 16, F32).astype(BF16)
    hi = lax.bitcast_convert_type(x & jnp.uint32(0xFFFF0000), F32).astype(BF16)
    return lo, hi


def _dispatch_kernel(pos_ref, h_ref, xb_in_ref, xb_ref, sem):
    del xb_in_ref
    tb = h_ref.shape[0]
    base = pl.program_id(0) * (tb * TOP_K)

    def issue(j, carry):
        for k in range(TOP_K):
            dst = pos_ref[base + j * TOP_K + k]
            pltpu.make_async_copy(h_ref.at[pl.ds(j, 1)], xb_ref.at[pl.ds(dst, 1)], sem).start()
        return carry

    lax.fori_loop(0, tb, issue, 0, unroll=8)

    def drain(j, carry):
        for k in range(TOP_K):
            pltpu.make_async_copy(h_ref.at[pl.ds(j, 1)], xb_ref.at[pl.ds(0, 1)], sem).wait()
        return carry

    lax.fori_loop(0, tb, drain, 0, unroll=8)


def _combine_kernel(pos_ref, yb_ref, x_ref, g_ref, gates_ref, o_ref, buf, sem):
    tb = x_ref.shape[0]
    base = pl.program_id(0) * (tb * TOP_K)

    def issue(j, carry):
        for k in range(TOP_K):
            src = pos_ref[base + j * TOP_K + k]
            pltpu.make_async_copy(yb_ref.at[pl.ds(src, 1)], buf.at[k, pl.ds(j, 1)], sem).start()
        return carry

    lax.fori_loop(0, tb, issue, 0, unroll=8)

    def drain(j, carry):
        for k in range(TOP_K):
            pltpu.make_async_copy(yb_ref.at[pl.ds(0, 1)], buf.at[k, pl.ds(j, 1)], sem).wait()
        return carry

    lax.fori_loop(0, tb, drain, 0, unroll=8)
    gates = gates_ref[...]
    f = gates[:, 0:1] * buf[0]
    for k in range(1, TOP_K):
        f = f + gates[:, k:k + 1] * buf[k]
    o_ref[...] = x_ref[...] + g_ref[0] * f


def _expert_up_kernel(be_ref, nu_ref, x_ref, wg_ref, wl_ref, bg_ref, bl_ref, o_ref, wg_s, wl_s):
    r = pl.program_id(1)

    @pl.when(r < nu_ref[0])
    def _():
        prev = be_ref[jnp.maximum(r - 1, 0)]

        @pl.when((r == 0) | (be_ref[r] != prev))
        def _():
            wg_s[...] = wg_ref[...].astype(BF16)
            wl_s[...] = wl_ref[...].astype(BF16)

        lo, hi = _unpack_halves(x_ref[...])
        half = lo.shape[1]
        glu = _dot(lo, wg_s[:half, :]) + _dot(hi, wg_s[half:, :]) + bg_ref[...]
        lin = _dot(lo, wl_s[:half, :]) + _dot(hi, wl_s[half:, :]) + bl_ref[...]
        glu = jnp.minimum(glu, SWIGLU_LIMIT)
        lin = jnp.clip(lin, -SWIGLU_LIMIT, SWIGLU_LIMIT)
        o_ref[...] = (glu * _sigmoid(SWIGLU_ALPHA * glu) * (lin + 1.0)).astype(o_ref.dtype)

    @pl.when(r >= nu_ref[0])
    def _():
        o_ref[...] = jnp.zeros_like(o_ref)


def _expert_down_kernel(be_ref, nu_ref, a_ref, w_ref, b_ref, o_ref, w_s):
    r = pl.program_id(1)

    @pl.when(r < nu_ref[0])
    def _():
        prev = be_ref[jnp.maximum(r - 1, 0)]

        @pl.when((r == 0) | (be_ref[r] != prev))
        def _():
            w_s[...] = w_ref[...].astype(BF16)

        o_ref[...] = _dot(a_ref[...], w_s[...]) + b_ref[...]

    @pl.when(r >= nu_ref[0])
    def _():
        o_ref[...] = jnp.zeros_like(o_ref)


def expert_ffn(block_exp, n_used, xb, w1, b1, w2, b2, layer, block_rows):
    n_rows, half = xb.shape
    d = 2 * half
    n_exp, _, f2 = w1.shape[1:]
    f = f2 // 2
    n_blocks = n_rows // block_rows
    tf = min(512, f)
    nf = f // tf
    b1r = b1.reshape(b1.shape[0], n_exp, 1, f2)
    b2r = b2.reshape(b2.shape[0], n_exp, 1, d)
    act = pl.pallas_call(
        _expert_up_kernel,
        grid_spec=pltpu.PrefetchScalarGridSpec(
            num_scalar_prefetch=2,
            grid=(nf, n_blocks),
            in_specs=[
                pl.BlockSpec((block_rows, half), lambda j, r, be, nu: (jnp.minimum(r, nu[0] - 1), 0)),
                pl.BlockSpec((None, None, d, tf), lambda j, r, be, nu: (layer, be[r], 0, j)),
                pl.BlockSpec((None, None, d, tf), lambda j, r, be, nu: (layer, be[r], 0, nf + j)),
                pl.BlockSpec((None, None, 1, tf), lambda j, r, be, nu: (layer, be[r], 0, j)),
                pl.BlockSpec((None, None, 1, tf), lambda j, r, be, nu: (layer, be[r], 0, nf + j)),
            ],
            out_specs=pl.BlockSpec((block_rows, tf), lambda j, r, be, nu: (r, j)),
            scratch_shapes=[pltpu.VMEM((d, tf), BF16), pltpu.VMEM((d, tf), BF16)],
        ),
        out_shape=jax.ShapeDtypeStruct((n_rows, f), BF16),
        compiler_params=_cparams("arbitrary", "arbitrary"),
        name="expert_up",
    )(block_exp, n_used, xb, w1, w1, b1r, b1r)
    tn = min(1024, d)
    return pl.pallas_call(
        _expert_down_kernel,
        grid_spec=pltpu.PrefetchScalarGridSpec(
            num_scalar_prefetch=2,
            grid=(d // tn, n_blocks),
            in_specs=[
                pl.BlockSpec((block_rows, f), lambda j, r, be, nu: (jnp.minimum(r, nu[0] - 1), 0)),
                pl.BlockSpec((None, None, f, tn), lambda j, r, be, nu: (layer, be[r], 0, j)),
                pl.BlockSpec((None, None, 1, tn), lambda j, r, be, nu: (layer, be[r], 0, j)),
            ],
            out_specs=pl.BlockSpec((block_rows, tn), lambda j, r, be, nu: (r, j)),
            scratch_shapes=[pltpu.VMEM((f, tn), BF16)],
        ),
        out_shape=jax.ShapeDtypeStruct((n_rows, d), F32),
        compiler_params=_cparams("arbitrary", "arbitrary"),
        name="expert_down",
    )(block_exp, n_used, act, w2, b2r)


def _blocked_cumsum(onehot):
    n, e = onehot.shape
    blk = math.gcd(n, 512)
    x = onehot.reshape(n // blk, blk, e).astype(BF16)
    tril = jnp.tril(jnp.ones((blk, blk), BF16))
    within = jnp.einsum("ij,bje->bie", tril, x, preferred_element_type=F32).astype(jnp.int32)
    totals = within[:, -1, :]
    offsets = jnp.cumsum(totals, axis=0) - totals
    return (within + offsets[:, None, :]).reshape(n, e)


def moe_residual(x, hp, logits, gate, w1, b1, w2, b2, layer, rows_lat, seq):
    n_tok, half = hp.shape
    d = 2 * half
    n_exp = w1.shape[1]
    top_val, top_idx = lax.top_k(logits, TOP_K)
    gates = jax.nn.softmax(top_val, axis=-1)
    n_assign = n_tok * TOP_K
    flat_e = top_idx.reshape(-1)
    onehot = (flat_e[:, None] == jnp.arange(n_exp, dtype=flat_e.dtype)[None, :]).astype(jnp.int32)
    csum = _blocked_cumsum(onehot)
    counts = csum[-1]
    block_rows = 16 * -(-int(MOE_LOAD_MARGIN * n_assign / n_exp) // 16)
    padded = (counts + block_rows - 1) // block_rows * block_rows
    pend = jnp.cumsum(padded)
    pstart = pend - padded
    pos = jnp.sum(onehot * (csum - 1 + pstart[None, :]), axis=1).astype(jnp.int32)
    n_blocks = -(-n_assign // block_rows) + n_exp
    n_rows = n_blocks * block_rows
    block_start = jnp.arange(n_blocks, dtype=jnp.int32) * block_rows
    block_exp = jnp.minimum(jnp.sum((block_start[:, None] >= pend[None, :]).astype(jnp.int32), axis=1), n_exp - 1)
    n_used = (pend[-1] // block_rows).astype(jnp.int32).reshape(1)

    tb = math.gcd(128, _stream_tile(n_tok, rows_lat, seq))
    xb = pl.pallas_call(
        _dispatch_kernel,
        grid_spec=pltpu.PrefetchScalarGridSpec(
            num_scalar_prefetch=1,
            grid=(n_tok // tb,),
            in_specs=[pl.BlockSpec((tb, half), lambda i, pos: (i, 0)), pl.BlockSpec(memory_space=pl.ANY)],
            out_specs=pl.BlockSpec(memory_space=pl.ANY),
            scratch_shapes=[pltpu.SemaphoreType.DMA(())],
        ),
        out_shape=jax.ShapeDtypeStruct((n_rows, half), jnp.uint32),
        input_output_aliases={2: 0},
        compiler_params=_cparams("arbitrary"),
        name="expert_dispatch",
    )(pos, hp, jnp.zeros((n_rows, half), jnp.uint32))
    yb = expert_ffn(block_exp.astype(jnp.int32), n_used, xb, w1, b1, w2, b2, layer, block_rows)
    n_batch = gate.shape[0] - 1
    idx = functools.partial(_stream_index, tm=tb, rows_lat=rows_lat, seq=seq, n_batch=n_batch)
    return pl.pallas_call(
        _combine_kernel,
        grid_spec=pltpu.PrefetchScalarGridSpec(
            num_scalar_prefetch=1,
            grid=(n_tok // tb,),
            in_specs=[pl.BlockSpec(memory_space=pl.ANY),
                      pl.BlockSpec((tb, d), lambda i, pos: (i, 0)),
                      pl.BlockSpec((1, 1, d), lambda i, pos: (idx(i), 0, 0)),
                      pl.BlockSpec((tb, TOP_K), lambda i, pos: (i, 0))],
            out_specs=pl.BlockSpec((tb, d), lambda i, pos: (i, 0)),
            scratch_shapes=[pltpu.VMEM((TOP_K, tb, d), F32), pltpu.SemaphoreType.DMA(())],
        ),
        out_shape=jax.ShapeDtypeStruct((n_tok, d), F32),
        compiler_params=_cparams("arbitrary"),
        name="expert_combine",
    )(pos, yb, x, gate, gates)


def _layer(i, x, c_all, p, n_batch, seq, n_ctx, tables, with_ctx):
    d = x.shape[1]
    rows_lat = n_batch * seq
    rows_all = rows_lat + n_batch * n_ctx
    n_out = rows_all if with_ctx else rows_lat
    n_stream = n_batch + 1

    mod = matmul(c_all, p["w_mod"], i, c_all.shape[0], c_all.shape[0], 512, F32, pre="silu", name="modulation")
    mod = (mod[:n_stream] + p["b_mod"][i]).reshape(n_stream, 6, 1, d)
    sh1, sc1, g1, sh2, sc2, g2 = (mod[:, k] for k in range(6))

    h1 = norm_mod(x, p["g_mix"][i], sc1, sh1, rows_all, rows_lat, seq, BF16)
    in_width = p["w_in"].shape[2]

    n_g = p["ssm_lambda_re"].shape[2]
    ssm_w = n_g * SSM_GROUP
    n_wh = p["win_sink"].shape[1]
    win_q = n_wh * WIN_HEAD_DIM
    win_kv = WIN_KV_HEADS * WIN_HEAD_DIM
    n_dh = (in_width - ssm_w - win_q - 2 * win_kv - 3 * d) // (2 * 2 * DIFF_QK_DIM + DIFF_V_DIM)
    diff_w = n_dh * 2 * DIFF_QK_DIM
    widths = dict(u=ssm_w, qw=win_q, kw=win_kv, vw=win_kv, qd=diff_w, kd=diff_w, vd=diff_w, gates=3 * d)
    src, c0 = {}, 0
    for name in ("u", "qw", "kw", "vw", "qd", "kd", "vd", "gates"):
        src[name] = c0
        c0 += widths[name]
    modes = dict(u=("plain", 1.0, F32), qw=("rope", WIN_HEAD_DIM ** -0.5, BF16), kw=("rope", 1.0, BF16),
                 vw=("plain", 1.0, BF16), qd=("rope", DIFF_QK_DIM ** -0.5, BF16), kd=("rope", 1.0, BF16),
                 vd=("plain", 1.0, BF16), gates=("sigmoid", 1.0, BF16))
    seg = {}
    for name, (mode, scale, dtype) in modes.items():
        w_seg = p["w_in"][i][:, src[name]:src[name] + widths[name]].astype(BF16)
        seg[name] = in_proj(h1, w_seg, 0, widths[name], mode, scale, dtype, tables, rows_lat, seq)

    mats = s5_matrices(p["ssm_lambda_re"][i], p["ssm_lambda_im"][i], p["ssm_log_dt"][i], p["ssm_b_re"][i],
                       p["ssm_b_im"][i], p["ssm_c_re"][i], p["ssm_c_im"][i], p["ssm_d"][i], S5_CHUNK)
    y = s5_mixer(seg["u"], mats, rows_all, n_batch, seq, n_ctx)
    o_ssm = s5_glu(y, p["w_glu"], p["b_glu"], i, n_out)

    sink = p["win_sink"][i].astype(F32)
    o_win = window_attention(sink, seg["qw"], seg["kw"], seg["vw"], n_batch, seq, n_ctx, n_wh, rows_lat, False)

    lam_p = p["diff_lambda"][i].astype(F32)
    lambda_init = 0.8 - 0.6 * math.exp(-0.3 * i)
    lam = (jnp.exp(jnp.sum(lam_p[0] * lam_p[1])) - jnp.exp(jnp.sum(lam_p[2] * lam_p[3])) + lambda_init).reshape(1)
    gain = p["diff_subln"][i].astype(F32).reshape(1, DIFF_V_DIM)
    o_diff = diff_attention(lam, gain, seg["qd"], seg["kd"], seg["vd"], n_batch, seq, n_ctx, n_dh, rows_lat, False,
                            1.0 - lambda_init)
    if with_ctx:
        o_win_c = window_attention(sink, seg["qw"], seg["kw"], seg["vw"], n_batch, seq, n_ctx, n_wh, rows_lat, True)
        o_diff_c = diff_attention(lam, gain, seg["qd"], seg["kd"], seg["vd"], n_batch, seq, n_ctx, n_dh, rows_lat, True,
                                  1.0 - lambda_init)
        o_win = jnp.concatenate([o_win, o_win_c], axis=0)
        o_diff = jnp.concatenate([o_diff, o_diff_c], axis=0)

    merged = gated_merge(o_ssm, o_win, o_diff, p["w_branch_ssm"], p["w_branch_win"], p["w_branch_diff"], seg["gates"],
                         i, n_out, d)
    x = out_proj_residual(merged, p["w_out"], x, g1, i, n_out, rows_lat, seq)

    hp, logits = norm_mod(x, p["g_ffn"][i], sc2, sh2, n_out, rows_lat, seq, BF16,
                          router=(p["w_router"][i], p["b_router"][i]))
    return moe_residual(x, hp, logits, g2, p["w_exp1"], p["b_exp1"], p["w_exp2"], p["b_exp2"], i, rows_lat, seq)


def kernel(x, c, ctx, c_ctx, w_mod, b_mod, g_mix, g_ffn, w_in, ssm_lambda_re, ssm_lambda_im, ssm_log_dt, ssm_b_re, ssm_b_im, ssm_c_re, ssm_c_im, ssm_d, w_glu, b_glu, win_sink, diff_lambda, diff_subln, w_branch_ssm, w_branch_win, w_branch_diff, w_out, w_router, b_router, w_exp1, b_exp1, w_exp2, b_exp2, g_final):
    n_batch, seq, d = x.shape
    n_ctx = ctx.shape[1]
    depth = w_mod.shape[0]
    p = dict(w_mod=w_mod, b_mod=b_mod, g_mix=g_mix, g_ffn=g_ffn, w_in=w_in, ssm_lambda_re=ssm_lambda_re,
             ssm_lambda_im=ssm_lambda_im, ssm_log_dt=ssm_log_dt, ssm_b_re=ssm_b_re, ssm_b_im=ssm_b_im,
             ssm_c_re=ssm_c_re, ssm_c_im=ssm_c_im, ssm_d=ssm_d, w_glu=w_glu, b_glu=b_glu, win_sink=win_sink,
             diff_lambda=diff_lambda, diff_subln=diff_subln, w_branch_ssm=w_branch_ssm, w_branch_win=w_branch_win,
             w_branch_diff=w_branch_diff, w_out=w_out, w_router=w_router, b_router=b_router, w_exp1=w_exp1,
             b_exp1=b_exp1, w_exp2=w_exp2, b_exp2=b_exp2)
    rows_lat = n_batch * seq
    rows = jnp.concatenate([x.reshape(rows_lat, d), ctx.reshape(n_batch * n_ctx, d)], axis=0)
    c_all = jnp.concatenate([c, c_ctx[None, :], jnp.zeros((8 - (n_batch + 1) % 8, d), F32)], axis=0)
    tables = rope_tables(seq, WIN_HEAD_DIM)
    for i in range(depth):
        with_ctx = i < depth - 1
        rows = _layer(i, rows, c_all, p, n_batch, seq, n_ctx, tables, with_ctx)
    no_mod = jnp.zeros((n_batch + 1, 1, d), F32)
    out = norm_mod(rows, g_final, no_mod, no_mod, rows_lat, rows_lat, seq, F32)
    return out.reshape(n_batch, seq, d)
```

```python
import functools
import math

import jax
import jax.numpy as jnp
import numpy as np
from jax import lax
from jax.experimental import pallas as pl
from jax.experimental.pallas import tpu as pltpu

F32 = jnp.float32
BF16 = jnp.bfloat16

GRID_W = 64
SSM_GROUP = 16
WIN_KV_HEADS = 2
WIN_HEAD_DIM = 64
WINDOW = 128
DIFF_QK_DIM = 64
DIFF_V_DIM = 2 * DIFF_QK_DIM
ATTN_BLOCK = 128
ROPE_BASE = 10000.0
TOP_K = 4
SWIGLU_LIMIT = 7.0
SWIGLU_ALPHA = 1.702
NORM_EPS = 1e-6
SUBLN_EPS = 1e-5
NEG_INF = -1e30

LANE = 128
VMEM_LIMIT_BYTES = 56 * 1024 * 1024
S5_CHUNK = 32
MOE_LOAD_MARGIN = 1.06
ROW_TILE = 512


def _cparams(*sem):
    return pltpu.CompilerParams(dimension_semantics=sem, vmem_limit_bytes=VMEM_LIMIT_BYTES)


def _dot(a, b):
    return jnp.dot(a, b, preferred_element_type=F32)


def _dot_nt(a, b):
    return lax.dot_general(a, b, (((1,), (1,)), ((), ())), preferred_element_type=F32)


def _sigmoid(x):
    return 1.0 / (1.0 + jnp.exp(-x))


def _row_tile(n_rows, cap):
    best = 16
    for t in range(16, cap + 1, 16):
        if n_rows % t == 0:
            best = t
    return best


def _norm_mod_kernel(x_ref, g_ref, sc_ref, sh_ref, o_ref):
    x = x_ref[...]
    y = x * lax.rsqrt(jnp.mean(x * x, axis=-1, keepdims=True) + NORM_EPS) * g_ref[...]
    o_ref[...] = (y * (1.0 + sc_ref[0]) + sh_ref[0]).astype(o_ref.dtype)


def _norm_router_kernel(x_ref, g_ref, sc_ref, sh_ref, wr_ref, br_ref, o_ref, lg_ref):
    x = x_ref[...]
    y = x * lax.rsqrt(jnp.mean(x * x, axis=-1, keepdims=True) + NORM_EPS) * g_ref[...]
    h = y * (1.0 + sc_ref[0]) + sh_ref[0]
    o_ref[...] = _pack_halves(h)
    lg_ref[...] = jnp.dot(h, wr_ref[...], preferred_element_type=F32, precision=lax.Precision.HIGHEST) + br_ref[...]


def _stream_tile(n_rows, rows_lat, seq):
    tm = math.gcd(ROW_TILE, seq)
    return math.gcd(tm, n_rows - rows_lat) if n_rows > rows_lat else tm


def _stream_index(i, tm, rows_lat, seq, n_batch):
    r = i * tm
    return jnp.where(r >= rows_lat, n_batch, r // seq)


def norm_mod(x, gain, sc, sh, n_rows, rows_lat, seq, out_dtype, router=None):
    d = x.shape[1]
    tm = _stream_tile(n_rows, rows_lat, seq)
    n_batch = sc.shape[0] - 1
    idx = functools.partial(_stream_index, tm=tm, rows_lat=rows_lat, seq=seq, n_batch=n_batch)
    in_specs = [
        pl.BlockSpec((tm, d), lambda i: (i, 0)),
        pl.BlockSpec((1, d), lambda i: (0, 0)),
        pl.BlockSpec((1, 1, d), lambda i: (idx(i), 0, 0)),
        pl.BlockSpec((1, 1, d), lambda i: (idx(i), 0, 0)),
    ]
    if router is None:
        return pl.pallas_call(
            _norm_mod_kernel,
            grid=(n_rows // tm,),
            in_specs=in_specs,
            out_specs=pl.BlockSpec((tm, d), lambda i: (i, 0)),
            out_shape=jax.ShapeDtypeStruct((n_rows, d), out_dtype),
            compiler_params=_cparams("parallel"),
            name="norm_mod",
        )(x, gain.reshape(1, d), sc, sh)
    w_router, b_router = router
    n_exp = w_router.shape[1]
    return pl.pallas_call(
        _norm_router_kernel,
        grid=(n_rows // tm,),
        in_specs=in_specs + [
            pl.BlockSpec((d, n_exp), lambda i: (0, 0)),
            pl.BlockSpec((1, n_exp), lambda i: (0, 0)),
        ],
        out_specs=[pl.BlockSpec((tm, d // 2), lambda i: (i, 0)), pl.BlockSpec((tm, n_exp), lambda i: (i, 0))],
        out_shape=[jax.ShapeDtypeStruct((n_rows, d // 2), jnp.uint32), jax.ShapeDtypeStruct((n_rows, n_exp), F32)],
        compiler_params=_cparams("parallel"),
        name="norm_router",
    )(x, gain.reshape(1, d), sc, sh, w_router, b_router.reshape(1, n_exp))


def _mm_kernel(x_ref, w_ref, o_ref, *, pre):
    x = x_ref[...]
    if pre == "silu":
        x = x * _sigmoid(x)
    o_ref[...] = _dot(x.astype(BF16), w_ref[...].astype(BF16)).astype(o_ref.dtype)


def matmul(x, w_stack, layer, n_rows, tm, tn, out_dtype, pre=None, name="matmul"):
    k = x.shape[1]
    n = w_stack.shape[2]
    return pl.pallas_call(
        functools.partial(_mm_kernel, pre=pre),
        grid=(n_rows // tm, n // tn),
        in_specs=[
            pl.BlockSpec((tm, k), lambda i, j: (i, 0)),
            pl.BlockSpec((None, k, tn), lambda i, j: (layer, 0, j)),
        ],
        out_specs=pl.BlockSpec((tm, tn), lambda i, j: (i, j)),
        out_shape=jax.ShapeDtypeStruct((n_rows, n), out_dtype),
        compiler_params=_cparams("parallel", "arbitrary"),
        name=name,
    )(x, w_stack)


def _in_proj_kernel(x_ref, w_ref, *refs, mode, scale):
    o_ref = refs[-1]
    acc = _dot(x_ref[...], w_ref[...])
    if mode == "sigmoid":
        o_ref[...] = _sigmoid(acc).astype(o_ref.dtype)
        return
    if mode == "plain":
        o_ref[...] = (acc if scale == 1.0 else acc * scale).astype(o_ref.dtype)
        return
    cos_ref, sin_ref = refs[0], refs[1]
    lane = lax.broadcasted_iota(jnp.int32, (1, LANE), 1)
    first_half = (lane % 32) < 16
    cos = cos_ref[...] if scale == 1.0 else cos_ref[...] * scale
    sin = sin_ref[...] if scale == 1.0 else sin_ref[...] * scale
    for c in range(acc.shape[1] // LANE):
        x = acc[:, c * LANE:(c + 1) * LANE]
        partner = jnp.where(first_half, pltpu.roll(x, LANE - 16, axis=1), pltpu.roll(x, 16, axis=1))
        o_ref[:, c * LANE:(c + 1) * LANE] = (x * cos + partner * sin).astype(o_ref.dtype)


def in_proj(h, w, col0, width, mode, scale, out_dtype, tables, rows_lat, seq):
    n_rows, k = h.shape
    tm = _stream_tile(n_rows, rows_lat, seq)
    tn = math.gcd(1024, width)
    assert col0 % tn == 0
    in_specs = [pl.BlockSpec((tm, k), lambda i, j: (i, 0)),
                pl.BlockSpec((k, tn), lambda i, j: (0, col0 // tn + j))]
    args = [h, w]
    if mode == "rope":
        cos_t, sin_t = tables
        tiles_per_seq = seq // tm
        lat_tiles = rows_lat // tm
        cos_x = jnp.concatenate([cos_t, jnp.ones((tm, LANE), F32)], axis=0)
        sin_x = jnp.concatenate([sin_t, jnp.zeros((tm, LANE), F32)], axis=0)
        tab = pl.BlockSpec((tm, LANE), lambda i, j: (jnp.where(i < lat_tiles, i % tiles_per_seq, tiles_per_seq), 0))
        in_specs += [tab, tab]
        args += [cos_x, sin_x]
    return pl.pallas_call(
        functools.partial(_in_proj_kernel, mode=mode, scale=scale),
        grid=(n_rows // tm, width // tn),
        in_specs=in_specs,
        out_specs=pl.BlockSpec((tm, tn), lambda i, j: (i, j)),
        out_shape=jax.ShapeDtypeStruct((n_rows, width), out_dtype),
        compiler_params=_cparams("parallel", "arbitrary"),
        name="in_proj_" + mode,
    )(*args)


def rope_tables(seq, head_dim):
    assert head_dim == 64
    rows = seq // GRID_W
    r, col = jnp.meshgrid(jnp.arange(rows, dtype=F32), jnp.arange(GRID_W, dtype=F32), indexing="ij")
    half = head_dim // 2
    inv_freq = ROPE_BASE ** (-jnp.arange(0, half, 2, dtype=F32) / half)
    ang_r = r.reshape(-1)[:, None] * inv_freq[None, :]
    ang_c = col.reshape(-1)[:, None] * inv_freq[None, :]
    cos = jnp.concatenate([jnp.cos(ang_r), jnp.cos(ang_r), jnp.cos(ang_c), jnp.cos(ang_c)], axis=-1)
    sin = jnp.concatenate([-jnp.sin(ang_r), jnp.sin(ang_r), -jnp.sin(ang_c), jnp.sin(ang_c)], axis=-1)
    return jnp.tile(cos, (1, LANE // head_dim)), jnp.tile(sin, (1, LANE // head_dim))


def _win_attn_kernel(sink_ref, q_ref, *refs, n_heads, band, seq):
    if band:
        kp_ref, kc_ref, kn_ref, vp_ref, vc_ref, vn_ref, kx_ref, vx_ref, o_ref = refs
    else:
        kx_ref, vx_ref, o_ref = refs
    blk = q_ref.shape[0]
    dh = WIN_HEAD_DIM
    grp = n_heads // WIN_KV_HEADS
    lane = lax.broadcasted_iota(jnp.int32, (1, LANE), 1)
    rows = grp * blk
    if band:
        n = pl.program_id(1)
        k_all = jnp.concatenate([kp_ref[...], kc_ref[...], kn_ref[...], kx_ref[...]], axis=0)
        v_all = jnp.concatenate([vp_ref[...], vc_ref[...], vn_ref[...], vx_ref[...]], axis=0)
        n_keys = k_all.shape[0]
        col = lax.broadcasted_iota(jnp.int32, (rows, n_keys), 1)
        qpos = n * blk + (lax.broadcasted_iota(jnp.int32, (rows, n_keys), 0) & (blk - 1))
        kpos = (n - 1) * blk + col
        mask = (col >= 3 * blk) | ((jnp.abs(qpos - kpos) <= WINDOW) & (kpos >= 0) & (kpos < seq))
    else:
        k_all, v_all = kx_ref[...], vx_ref[...]
        n_keys = k_all.shape[0]
    ones_col = (lax.broadcasted_iota(jnp.int32, (n_keys, LANE), 1) == 0).astype(BF16)
    v_aug = jnp.concatenate([v_all, ones_col], axis=1)
    outs = [None] * n_heads
    for kvh in range(WIN_KV_HEADS):
        keep = (lane >= kvh * dh) & (lane < (kvh + 1) * dh)
        q_parts, sink_parts = [], []
        for g in range(grp):
            h = kvh * grp + g
            c = (h * dh) // LANE
            qc = q_ref[:, c * LANE:(c + 1) * LANE]
            if (h * dh) % LANE != kvh * dh:
                qc = jnp.concatenate([qc[:, dh:], qc[:, :dh]], axis=1)
            q_parts.append(jnp.where(keep, qc, jnp.zeros_like(qc)))
            sink_parts.append(jnp.full((blk, 1), sink_ref[h], F32))
        qs = jnp.concatenate(q_parts, axis=0)
        sink = jnp.concatenate(sink_parts, axis=0)
        s = _dot_nt(qs, k_all)
        if band:
            s = jnp.where(mask, s, NEG_INF)
        m = jnp.maximum(jnp.max(s, axis=-1, keepdims=True), sink)
        acc = _dot(jnp.exp(s - m).astype(BF16), v_aug)
        inv = 1.0 / (acc[:, LANE:LANE + 1] + jnp.exp(sink - m))
        o = acc[:, kvh * dh:(kvh + 1) * dh] * inv
        for g in range(grp):
            outs[kvh * grp + g] = o[g * blk:(g + 1) * blk, :]
    o_ref[...] = jnp.concatenate(outs, axis=1).astype(o_ref.dtype)


def window_attention(sink, q, k, v, n_batch, seq, n_ctx, n_heads, rows_lat, ctx_queries):
    blk = ATTN_BLOCK
    seq_q = n_ctx if ctx_queries else seq
    nb = seq_q // blk
    q0 = rows_lat // blk if ctx_queries else 0
    x0 = rows_lat // n_ctx
    qw = n_heads * WIN_HEAD_DIM
    kvw = WIN_KV_HEADS * WIN_HEAD_DIM
    in_specs = [
        pl.BlockSpec(memory_space=pltpu.SMEM),
        pl.BlockSpec((blk, qw), lambda b, n: (q0 + b * nb + n, 0)),
    ]
    args = [sink, q]
    if not ctx_queries:
        band_specs = [
            pl.BlockSpec((blk, kvw), lambda b, n: (b * nb + jnp.maximum(n - 1, 0), 0)),
            pl.BlockSpec((blk, kvw), lambda b, n: (b * nb + n, 0)),
            pl.BlockSpec((blk, kvw), lambda b, n: (b * nb + jnp.minimum(n + 1, nb - 1), 0)),
        ]
        in_specs += band_specs + band_specs
        args += [k, k, k, v, v, v]
    in_specs += [pl.BlockSpec((n_ctx, kvw), lambda b, n: (x0 + b, 0))] * 2
    args += [k, v]
    return pl.pallas_call(
        functools.partial(_win_attn_kernel, n_heads=n_heads, band=not ctx_queries, seq=seq),
        grid=(n_batch, nb),
        in_specs=in_specs,
        out_specs=pl.BlockSpec((blk, qw), lambda b, n: (b * nb + n, 0)),
        out_shape=jax.ShapeDtypeStruct((n_batch * seq_q, qw), BF16),
        compiler_params=_cparams("parallel", "parallel"),
        name="context_window_attention" if ctx_queries else "window_attention",
    )(*args)


def _diff_attn_kernel(lam_ref, q_ref, gain_ref, *refs, with_lat, post_scale):
    if with_lat:
        kl_ref, vl_ref, kx_ref, vx_ref, o_ref = refs
    else:
        kx_ref, vx_ref, o_ref = refs
    lam = lam_ref[0]
    q = q_ref[...]
    lane = lax.broadcasted_iota(jnp.int32, (1, LANE), 1)
    zero = jnp.zeros_like(q)
    kx = kx_ref[...]

    def softmax_parts(qm):
        s_x = _dot_nt(qm, kx)
        m = jnp.max(s_x, axis=-1, keepdims=True)
        if with_lat:
            s_l = _dot_nt(qm, kl_ref[...])
            m = jnp.maximum(m, jnp.max(s_l, axis=-1, keepdims=True))
            e_l = jnp.exp(s_l - m)
        else:
            e_l = None
        e_x = jnp.exp(s_x - m)
        den = jnp.sum(e_x, axis=-1, keepdims=True)
        if with_lat:
            den = den + jnp.sum(e_l, axis=-1, keepdims=True)
        return e_l, e_x, 1.0 / den

    e1l, e1x, inv1 = softmax_parts(jnp.where(lane < DIFF_QK_DIM, q, zero))
    e2l, e2x, inv2 = softmax_parts(jnp.where(lane >= DIFF_QK_DIM, q, zero))
    w2 = lam * inv2
    o = _dot((e1x * inv1 - e2x * w2).astype(BF16), vx_ref[...])
    if with_lat:
        o = o + _dot((e1l * inv1 - e2l * w2).astype(BF16), vl_ref[...])
    o = o * lax.rsqrt(jnp.mean(o * o, axis=-1, keepdims=True) + SUBLN_EPS) * gain_ref[...]
    o_ref[...] = (o * post_scale).astype(o_ref.dtype)


def diff_attention(lam, gain, q, k, v, n_batch, seq, n_ctx, n_heads, rows_lat, ctx_queries, post_scale):
    seq_q = n_ctx if ctx_queries else seq
    tq = min(256, seq_q)
    nq = seq_q // tq
    q0 = rows_lat // tq if ctx_queries else 0
    x0 = rows_lat // n_ctx
    in_specs = [
        pl.BlockSpec(memory_space=pltpu.SMEM),
        pl.BlockSpec((tq, LANE), lambda b, h, i: (q0 + b * nq + i, h)),
        pl.BlockSpec((1, LANE), lambda b, h, i: (0, 0)),
    ]
    args = [lam, q, gain]
    if not ctx_queries:
        in_specs += [pl.BlockSpec((seq, LANE), lambda b, h, i: (b, h))] * 2
        args += [k, v]
    in_specs += [pl.BlockSpec((n_ctx, LANE), lambda b, h, i: (x0 + b, h))] * 2
    args += [k, v]
    return pl.pallas_call(
        functools.partial(_diff_attn_kernel, with_lat=not ctx_queries, post_scale=post_scale),
        grid=(n_batch, n_heads, nq),
        in_specs=in_specs,
        out_specs=pl.BlockSpec((tq, LANE), lambda b, h, i: (b * nq + i, h)),
        out_shape=jax.ShapeDtypeStruct((n_batch * seq_q, n_heads * LANE), BF16),
        compiler_params=_cparams("parallel", "parallel", "arbitrary"),
        name="context_diff_attention" if ctx_queries else "diff_attention",
    )(*args)


def s5_matrices(lam_re, lam_im, log_dt, b_re, b_im, c_re, c_im, d_skip, t_chunk):
    n_dir, n_g, n_p = lam_re.shape
    n_h = b_re.shape[-1]
    lam_re = jnp.minimum(lam_re.astype(F32), -1e-4)
    lam_im = lam_im.astype(F32)
    dt = jnp.exp(log_dt.astype(F32))[..., None]
    mag = jnp.exp(lam_re * dt)
    a_re = mag * jnp.cos(lam_im * dt)
    a_im = mag * jnp.sin(lam_im * dt)
    den = lam_re * lam_re + lam_im * lam_im
    k_re = ((a_re - 1.0) * lam_re + a_im * lam_im) / den
    k_im = (a_im * lam_re - (a_re - 1.0) * lam_im) / den
    b_re = b_re.astype(F32)
    b_im = b_im.astype(F32)
    bb_re = k_re[..., None] * b_re - k_im[..., None] * b_im
    bb_im = k_re[..., None] * b_im + k_im[..., None] * b_re
    c_re = c_re.astype(F32)
    c_im = c_im.astype(F32)
    ldt_re = lam_re * dt
    ldt_im = lam_im * dt
    steps = jnp.arange(t_chunk, dtype=F32)

    def power(d, t, p_last):
        lr = ldt_re[d][:, None, :] if p_last else ldt_re[d][:, :, None]
        li = ldt_im[d][:, None, :] if p_last else ldt_im[d][:, :, None]
        tt = t[None, :, None] if p_last else t[None, None, :]
        mag_t = jnp.exp(lr * tt)
        return mag_t * jnp.cos(li * tt), mag_t * jnp.sin(li * tt)

    def c_times_power(d, t):
        pr, pi = power(d, t, False)
        cr = jnp.swapaxes(c_re[d], 1, 2)[:, :, None, :]
        ci = jnp.swapaxes(c_im[d], 1, 2)[:, :, None, :]
        re = cr * pr[..., None] - ci * pi[..., None]
        im = cr * pi[..., None] + ci * pr[..., None]
        return jnp.concatenate([re, -im], axis=1).reshape(n_g, 2 * n_p, t_chunk * n_h)

    def b_times_power(d, t):
        pr, pi = power(d, t, True)
        br = jnp.swapaxes(bb_re[d], 1, 2)[:, None, :, :]
        bi = jnp.swapaxes(bb_im[d], 1, 2)[:, None, :, :]
        re = pr[:, :, None, :] * br - pi[:, :, None, :] * bi
        im = pr[:, :, None, :] * bi + pi[:, :, None, :] * br
        return jnp.concatenate([re, im], axis=-1).reshape(n_g, t_chunk * n_h, 2 * n_p)

    ca_lag = jnp.stack([c_times_power(0, steps), c_times_power(1, t_chunk - 1 - steps)])
    bbt = jnp.stack([jnp.concatenate([jnp.swapaxes(bb_re[d], 1, 2), jnp.swapaxes(bb_im[d], 1, 2)], axis=-1)
                     for d in range(2)])
    p_mat = jnp.concatenate([b_times_power(0, t_chunk - 1 - steps), b_times_power(1, steps)], axis=-1)
    q_mat = jnp.concatenate([c_times_power(0, steps + 1.0), c_times_power(1, t_chunk - steps)], axis=1)
    d_tile = jnp.tile(d_skip.astype(F32).reshape(n_g, 1, n_h), (1, 1, t_chunk))
    mag_t = jnp.exp(ldt_re * t_chunk)
    at_re, at_im = mag_t * jnp.cos(ldt_im * t_chunk), mag_t * jnp.sin(ldt_im * t_chunk)
    at_mul = jnp.concatenate([at_re, at_re], axis=-1)
    at_swp = jnp.concatenate([-at_im, at_im], axis=-1)
    return ca_lag, bbt, p_mat.astype(BF16), q_mat.astype(BF16), d_tile, at_mul, at_swp


def _s5_local_kernel(x_ref, p_ref, e_ref):
    e_ref[...] = _dot(x_ref[...].astype(BF16), p_ref[...])


def _s5_out_kernel(x_ref, s_ref, ca_ref, bbt_ref, q_ref, d_ref, y_ref, m_ref):
    n_h = bbt_ref.shape[1]
    th = m_ref.shape[0]
    t_chunk = th // n_h
    hi = lax.Precision.HIGHEST
    row_f = jnp.dot(bbt_ref[0], ca_ref[0], preferred_element_type=F32, precision=hi)
    row_b = jnp.dot(bbt_ref[1], ca_ref[1], preferred_element_type=F32, precision=hi)
    lane = lax.broadcasted_iota(jnp.int32, (n_h, th), 1)
    for j in range(t_chunk):
        fwd = jnp.where(lane >= n_h * j, pltpu.roll(row_f, n_h * j, axis=1), 0.0) if j else row_f
        back = n_h * (t_chunk - 1 - j)
        bwd = jnp.where(lane < n_h * (j + 1), pltpu.roll(row_b, th - back, axis=1), 0.0) if back else row_b
        m_ref[n_h * j:n_h * (j + 1), :] = (fwd + bwd).astype(m_ref.dtype)
    x = x_ref[...]
    y_ref[...] = _dot(x.astype(BF16), m_ref[...]) + _dot(s_ref[...].astype(BF16), q_ref[...]) + x * d_ref[...]


def _s5_scan_kernel(ef_ref, eb_ref, mul_ref, swp_ref, sf_ref, sb_ref, *, n_batch, lat_chunks, ctx_chunks):
    half = ef_ref.shape[2] // 2
    mul_f, mul_b = mul_ref[0], mul_ref[1]
    swp_f, swp_b = swp_ref[0], swp_ref[1]
    zero = jnp.zeros(ef_ref.shape[1:], F32)

    def step(c_f, c_b, carry):
        s_f, s_b = carry
        sf_ref[c_f] = s_f
        sb_ref[c_b] = s_b
        s_f = s_f * mul_f + pltpu.roll(s_f, half, axis=1) * swp_f + ef_ref[c_f]
        s_b = s_b * mul_b + pltpu.roll(s_b, half, axis=1) * swp_b + eb_ref[c_b]
        return s_f, s_b

    for b in range(n_batch):
        ctx0 = n_batch * lat_chunks + b * ctx_chunks
        lat0 = b * lat_chunks
        carry = lax.fori_loop(0, ctx_chunks, lambda i, cr: step(ctx0 + i, ctx0 + ctx_chunks - 1 - i, cr), (zero, zero))
        lax.fori_loop(0, lat_chunks, lambda i, cr: step(lat0 + i, lat0 + lat_chunks - 1 - i, cr), carry)


def s5_mixer(proj, mats, n_rows, n_batch, seq, n_ctx):
    ca_lag, bbt, p_mat, q_mat, d_tile, at_mul, at_swp = mats
    n_g, th, _ = p_mat.shape
    n_h = SSM_GROUP
    t_chunk = th // n_h
    p4 = p_mat.shape[2]
    n_p2 = p4 // 2
    width = n_g * n_h
    n_chunks = n_rows // t_chunk
    u = proj[:, :width]
    x = u.reshape(n_chunks, t_chunk, n_g, n_h).transpose(2, 0, 1, 3).reshape(n_g, n_chunks, th)
    e = pl.pallas_call(
        _s5_local_kernel,
        grid=(n_g,),
        in_specs=[pl.BlockSpec((None, n_chunks, th), lambda g: (g, 0, 0)),
                  pl.BlockSpec((None, th, p4), lambda g: (g, 0, 0))],
        out_specs=pl.BlockSpec((n_chunks, p4), lambda g: (0, g)),
        out_shape=jax.ShapeDtypeStruct((n_chunks, n_g * p4), F32),
        compiler_params=_cparams("parallel"),
        name="s5_local_state",
    )(x, p_mat)
    e = e.reshape(n_chunks, n_g, 2, n_p2)
    gb = 16
    spec = pl.BlockSpec((n_chunks, gb, n_p2), lambda g: (0, g, 0))
    tab = pl.BlockSpec((2, gb, n_p2), lambda g: (0, g, 0))
    s_f, s_b = pl.pallas_call(
        functools.partial(_s5_scan_kernel, n_batch=n_batch, lat_chunks=seq // t_chunk, ctx_chunks=n_ctx // t_chunk),
        grid=(n_g // gb,),
        in_specs=[spec, spec, tab, tab],
        out_specs=[spec, spec],
        out_shape=[jax.ShapeDtypeStruct((n_chunks, n_g, n_p2), F32)] * 2,
        compiler_params=_cparams("parallel"),
        name="s5_chunk_scan",
    )(e[:, :, 0], e[:, :, 1], at_mul, at_swp)
    s_in = jnp.stack([s_f, s_b], axis=2).reshape(n_chunks, n_g * p4)
    y = pl.pallas_call(
        _s5_out_kernel,
        grid=(n_g,),
        in_specs=[pl.BlockSpec((None, n_chunks, th), lambda g: (g, 0, 0)),
                  pl.BlockSpec((n_chunks, p4), lambda g: (0, g)),
                  pl.BlockSpec((2, None, p4 // 2, th), lambda g: (0, g, 0, 0)),
                  pl.BlockSpec((2, None, n_h, p4 // 2), lambda g: (0, g, 0, 0)),
                  pl.BlockSpec((None, p4, th), lambda g: (g, 0, 0)),
                  pl.BlockSpec((None, 1, th), lambda g: (g, 0, 0))],
        out_specs=pl.BlockSpec((None, n_chunks, th), lambda g: (g, 0, 0)),
        out_shape=jax.ShapeDtypeStruct((n_g, n_chunks, th), F32),
        scratch_shapes=[pltpu.VMEM((th, th), BF16)],
        compiler_params=_cparams("parallel"),
        name="s5_output",
    )(x, s_in, ca_lag, bbt, q_mat, d_tile)
    return y.reshape(n_g, n_chunks, t_chunk, n_h).transpose(1, 2, 0, 3).reshape(n_rows, width)


def _glu_kernel(y_ref, w_ref, b_ref, o_ref):
    y = y_ref[...]
    gy = 0.5 * y * (1.0 + jnp.tanh(math.sqrt(2.0 / math.pi) * (y + 0.044715 * (y * y * y))))
    z = _dot(gy.astype(BF16), w_ref[...].astype(BF16)) + b_ref[...]
    o_ref[...] = (gy * _sigmoid(z)).astype(o_ref.dtype)


def s5_glu(y, w_glu, b_glu, layer, n_rows):
    width = y.shape[1]
    tm = _row_tile(n_rows, ROW_TILE)
    return pl.pallas_call(
        _glu_kernel,
        grid=(n_rows // tm,),
        in_specs=[pl.BlockSpec((tm, width), lambda i: (i, 0)),
                  pl.BlockSpec((None, width, width), lambda i: (layer, 0, 0)),
                  pl.BlockSpec((None, 1, width), lambda i: (layer, 0, 0))],
        out_specs=pl.BlockSpec((tm, width), lambda i: (i, 0)),
        out_shape=jax.ShapeDtypeStruct((n_rows, width), BF16),
        compiler_params=_cparams("parallel"),
        name="s5_glu",
    )(y, w_glu, b_glu.reshape(b_glu.shape[0], 1, width))


def _merge_kernel(os_ref, ow_ref, od_ref, ws_ref, ww_ref, wd_ref, gs_ref, gw_ref, gd_ref, o_ref):
    m = (gs_ref[...].astype(F32) * _dot(os_ref[...], ws_ref[...].astype(BF16))
         + gw_ref[...].astype(F32) * _dot(ow_ref[...], ww_ref[...].astype(BF16))
         + gd_ref[...].astype(F32) * _dot(od_ref[...], wd_ref[...].astype(BF16)))
    o_ref[...] = m.astype(o_ref.dtype)


def gated_merge(o_ssm, o_win, o_diff, w_s, w_w, w_d, proj, layer, n_rows, d):
    tm = _row_tile(n_rows, 1088)
    tn = 256
    assert d % tn == 0
    g0 = 0
    nd = d // tn

    def branch(arr):
        return pl.BlockSpec((tm, arr.shape[1]), lambda i, j: (i, 0))

    def weight(w):
        return pl.BlockSpec((None, w.shape[1], tn), lambda i, j: (layer, 0, j))

    def gate(k):
        return pl.BlockSpec((tm, tn), lambda i, j: (i, g0 + k * nd + j))

    return pl.pallas_call(
        _merge_kernel,
        grid=(n_rows // tm, nd),
        in_specs=[branch(o_ssm), branch(o_win), branch(o_diff), weight(w_s), weight(w_w), weight(w_d),
                  gate(0), gate(1), gate(2)],
        out_specs=pl.BlockSpec((tm, tn), lambda i, j: (i, j)),
        out_shape=jax.ShapeDtypeStruct((n_rows, d), BF16),
        compiler_params=_cparams("parallel", "arbitrary"),
        name="gated_merge",
    )(o_ssm, o_win, o_diff, w_s, w_w, w_d, proj, proj, proj)


def _out_proj_kernel(m_ref, w_ref, x_ref, g_ref, o_ref):
    o_ref[...] = x_ref[...] + g_ref[0] * _dot(m_ref[...], w_ref[...].astype(BF16))


def out_proj_residual(m, w_out, x, gate, layer, n_rows, rows_lat, seq):
    d = x.shape[1]
    tm = _stream_tile(n_rows, rows_lat, seq)
    tn = min(1024, d)
    n_batch = gate.shape[0] - 1
    idx = functools.partial(_stream_index, tm=tm, rows_lat=rows_lat, seq=seq, n_batch=n_batch)
    return pl.pallas_call(
        _out_proj_kernel,
        grid=(n_rows // tm, d // tn),
        in_specs=[pl.BlockSpec((tm, m.shape[1]), lambda i, j: (i, 0)),
                  pl.BlockSpec((None, m.shape[1], tn), lambda i, j: (layer, 0, j)),
                  pl.BlockSpec((tm, tn), lambda i, j: (i, j)),
                  pl.BlockSpec((1, 1, tn), lambda i, j: (idx(i), 0, j))],
        out_specs=pl.BlockSpec((tm, tn), lambda i, j: (i, j)),
        out_shape=jax.ShapeDtypeStruct((n_rows, d), F32),
        compiler_params=_cparams("parallel", "arbitrary"),
        name="out_proj_residual",
    )(m, w_out, x, gate)


def _pack_halves(h):
    half = h.shape[1] // 2
    lo = lax.bitcast_convert_type(h[:, :half].astype(BF16).astype(F32), jnp.uint32)
    hi = lax.bitcast_convert_type(h[:, half:].astype(BF16).astype(F32), jnp.uint32)
    return (lo >> 16) | (hi & jnp.uint32(0xFFFF0000))


def _unpack_halves(x):
    lo = lax.bitcast_convert_type(x << 16, F32).astype(BF16)
    hi = lax.bitcast_convert_type(x & jnp.uint32(0xFFFF0000), F32).astype(BF16)
    return lo, hi


def _dispatch_kernel(pos_ref, pad_lo_ref, pad_hi_ref, h_ref, xb_ref, zero_ref, sem):
    tb = h_ref.shape[0]
    base = pl.program_id(0) * (tb * TOP_K)

    @pl.when(pl.program_id(0) == 0)
    def _():
        zero_ref[...] = jnp.zeros_like(zero_ref)
        for e in range(pad_lo_ref.shape[0]):
            def fill(row, carry):
                pltpu.make_async_copy(zero_ref.at[pl.ds(0, 1)], xb_ref.at[pl.ds(row, 1)], sem).start()
                return carry

            def fill_done(row, carry):
                pltpu.make_async_copy(zero_ref.at[pl.ds(0, 1)], xb_ref.at[pl.ds(row, 1)], sem).wait()
                return carry

            lax.fori_loop(pad_lo_ref[e], pad_hi_ref[e], fill, 0)
            lax.fori_loop(pad_lo_ref[e], pad_hi_ref[e], fill_done, 0)

    def issue(j, carry):
        for k in range(TOP_K):
            dst = pos_ref[base + j * TOP_K + k]
            pltpu.make_async_copy(h_ref.at[pl.ds(j, 1)], xb_ref.at[pl.ds(dst, 1)], sem).start()
        return carry

    lax.fori_loop(0, tb, issue, 0, unroll=8)

    def drain(j, carry):
        for k in range(TOP_K):
            pltpu.make_async_copy(h_ref.at[pl.ds(j, 1)], xb_ref.at[pl.ds(0, 1)], sem).wait()
        return carry

    lax.fori_loop(0, tb, drain, 0, unroll=8)


def _combine_kernel(pos_ref, yb_ref, x_ref, g_ref, gates_ref, o_ref, buf, sem):
    tb = x_ref.shape[0]
    base = pl.program_id(0) * (tb * TOP_K)

    def issue(j, carry):
        for k in range(TOP_K):
            src = pos_ref[base + j * TOP_K + k]
            pltpu.make_async_copy(yb_ref.at[pl.ds(src, 1)], buf.at[k, pl.ds(j, 1)], sem).start()
        return carry

    lax.fori_loop(0, tb, issue, 0, unroll=8)

    def drain(j, carry):
        for k in range(TOP_K):
            pltpu.make_async_copy(yb_ref.at[pl.ds(0, 1)], buf.at[k, pl.ds(j, 1)], sem).wait()
        return carry

    lax.fori_loop(0, tb, drain, 0, unroll=8)
    gates = gates_ref[...]
    f = gates[:, 0:1] * buf[0]
    for k in range(1, TOP_K):
        f = f + gates[:, k:k + 1] * buf[k]
    o_ref[...] = x_ref[...] + g_ref[0] * f


def _expert_up_kernel(be_ref, nu_ref, x_ref, wg_ref, wl_ref, bg_ref, bl_ref, o_ref):
    @pl.when(pl.program_id(0) < nu_ref[0])
    def _():
        wg = wg_ref[...].astype(BF16)
        wl = wl_ref[...].astype(BF16)
        lo, hi = _unpack_halves(x_ref[...])
        half = lo.shape[1]
        glu = _dot(lo, wg[:half, :]) + _dot(hi, wg[half:, :]) + bg_ref[...]
        lin = _dot(lo, wl[:half, :]) + _dot(hi, wl[half:, :]) + bl_ref[...]
        glu = jnp.minimum(glu, SWIGLU_LIMIT)
        lin = jnp.clip(lin, -SWIGLU_LIMIT, SWIGLU_LIMIT)
        o_ref[...] = (glu * _sigmoid(SWIGLU_ALPHA * glu) * (lin + 1.0)).astype(o_ref.dtype)


def _expert_down_kernel(be_ref, nu_ref, a_ref, w_ref, b_ref, o_ref):
    @pl.when(pl.program_id(0) < nu_ref[0])
    def _():
        o_ref[...] = _dot(a_ref[...], w_ref[...].astype(BF16)) + b_ref[...]


def expert_ffn(block_exp, n_used, xb, w1, b1, w2, b2, layer, block_rows):
    n_rows, half = xb.shape
    d = 2 * half
    n_exp, _, f2 = w1.shape[1:]
    f = f2 // 2
    n_blocks = n_rows // block_rows
    tf = min(512, f)
    nf = f // tf
    b1r = b1.reshape(b1.shape[0], n_exp, 1, f2)
    b2r = b2.reshape(b2.shape[0], n_exp, 1, d)

    def rows(r, nu):
        return jnp.minimum(r, nu[0] - 1)

    def tile(r, j, nu, n_tiles):
        return jnp.where(r < nu[0], j, n_tiles - 1)

    act = pl.pallas_call(
        _expert_up_kernel,
        grid_spec=pltpu.PrefetchScalarGridSpec(
            num_scalar_prefetch=2,
            grid=(n_blocks, nf),
            in_specs=[
                pl.BlockSpec((block_rows, half), lambda r, j, be, nu: (rows(r, nu), 0)),
                pl.BlockSpec((None, None, d, tf), lambda r, j, be, nu: (layer, be[rows(r, nu)], 0, tile(r, j, nu, nf))),
                pl.BlockSpec((None, None, d, tf),
                             lambda r, j, be, nu: (layer, be[rows(r, nu)], 0, nf + tile(r, j, nu, nf))),
                pl.BlockSpec((None, None, 1, tf), lambda r, j, be, nu: (layer, be[rows(r, nu)], 0, tile(r, j, nu, nf))),
                pl.BlockSpec((None, None, 1, tf),
                             lambda r, j, be, nu: (layer, be[rows(r, nu)], 0, nf + tile(r, j, nu, nf))),
            ],
            out_specs=pl.BlockSpec((block_rows, tf), lambda r, j, be, nu: (rows(r, nu), tile(r, j, nu, nf))),
        ),
        out_shape=jax.ShapeDtypeStruct((n_rows, f), BF16),
        compiler_params=_cparams("arbitrary", "arbitrary"),
        name="expert_up",
    )(block_exp, n_used, xb, w1, w1, b1r, b1r)
    tn = min(1024, d)
    nd = d // tn
    return pl.pallas_call(
        _expert_down_kernel,
        grid_spec=pltpu.PrefetchScalarGridSpec(
            num_scalar_prefetch=2,
            grid=(n_blocks, nd),
            in_specs=[
                pl.BlockSpec((block_rows, f), lambda r, j, be, nu: (rows(r, nu), 0)),
                pl.BlockSpec((None, None, f, tn), lambda r, j, be, nu: (layer, be[rows(r, nu)], 0, tile(r, j, nu, nd))),
                pl.BlockSpec((None, None, 1, tn), lambda r, j, be, nu: (layer, be[rows(r, nu)], 0, tile(r, j, nu, nd))),
            ],
            out_specs=pl.BlockSpec((block_rows, tn), lambda r, j, be, nu: (rows(r, nu), tile(r, j, nu, nd))),
        ),
        out_shape=jax.ShapeDtypeStruct((n_rows, d), F32),
        compiler_params=_cparams("arbitrary", "arbitrary"),
        name="expert_down",
    )(block_exp, n_used, act, w2, b2r)


def _blocked_cumsum(onehot):
    n, e = onehot.shape
    blk = math.gcd(n, 512)
    x = onehot.reshape(n // blk, blk, e).astype(BF16)
    tril = jnp.tril(jnp.ones((blk, blk), BF16))
    within = jnp.einsum("ij,bje->bie", tril, x, preferred_element_type=F32).astype(jnp.int32)
    totals = within[:, -1, :]
    offsets = jnp.cumsum(totals, axis=0) - totals
    return (within + offsets[:, None, :]).reshape(n, e)


def moe_residual(x, hp, logits, gate, w1, b1, w2, b2, layer, rows_lat, seq):
    n_tok, half = hp.shape
    d = 2 * half
    n_exp = w1.shape[1]
    top_val, top_idx = lax.top_k(logits, TOP_K)
    gates = jax.nn.softmax(top_val, axis=-1)
    n_assign = n_tok * TOP_K
    flat_e = top_idx.reshape(-1)
    onehot = (flat_e[:, None] == jnp.arange(n_exp, dtype=flat_e.dtype)[None, :]).astype(jnp.int32)
    csum = _blocked_cumsum(onehot)
    counts = csum[-1]
    block_rows = 16 * -(-int(MOE_LOAD_MARGIN * n_assign / n_exp) // 16)
    padded = (counts + block_rows - 1) // block_rows * block_rows
    pend = jnp.cumsum(padded)
    pstart = pend - padded
    pos = jnp.sum(onehot * (csum - 1 + pstart[None, :]), axis=1).astype(jnp.int32)
    n_blocks = -(-n_assign // block_rows) + n_exp
    n_rows = n_blocks * block_rows
    block_start = jnp.arange(n_blocks, dtype=jnp.int32) * block_rows
    block_exp = jnp.minimum(jnp.sum((block_start[:, None] >= pend[None, :]).astype(jnp.int32), axis=1), n_exp - 1)
    n_used = (pend[-1] // block_rows).astype(jnp.int32).reshape(1)

    tb = math.gcd(128, _stream_tile(n_tok, rows_lat, seq))
    xb = pl.pallas_call(
        _dispatch_kernel,
        grid_spec=pltpu.PrefetchScalarGridSpec(
            num_scalar_prefetch=3,
            grid=(n_tok // tb,),
            in_specs=[pl.BlockSpec((tb, half), lambda i, pos, lo, hi: (i, 0))],
            out_specs=pl.BlockSpec(memory_space=pl.ANY),
            scratch_shapes=[pltpu.VMEM((8, half), jnp.uint32), pltpu.SemaphoreType.DMA(())],
        ),
        out_shape=jax.ShapeDtypeStruct((n_rows, half), jnp.uint32),
        compiler_params=_cparams("arbitrary"),
        name="expert_dispatch",
    )(pos, (pstart + counts).astype(jnp.int32), pend.astype(jnp.int32), hp)
    yb = expert_ffn(block_exp.astype(jnp.int32), n_used, xb, w1, b1, w2, b2, layer, block_rows)
    n_batch = gate.shape[0] - 1
    idx = functools.partial(_stream_index, tm=tb, rows_lat=rows_lat, seq=seq, n_batch=n_batch)
    return pl.pallas_call(
        _combine_kernel,
        grid_spec=pltpu.PrefetchScalarGridSpec(
            num_scalar_prefetch=1,
            grid=(n_tok // tb,),
            in_specs=[pl.BlockSpec(memory_space=pl.ANY),
                      pl.BlockSpec((tb, d), lambda i, pos: (i, 0)),
                      pl.BlockSpec((1, 1, d), lambda i, pos: (idx(i), 0, 0)),
                      pl.BlockSpec((tb, TOP_K), lambda i, pos: (i, 0))],
            out_specs=pl.BlockSpec((tb, d), lambda i, pos: (i, 0)),
            scratch_shapes=[pltpu.VMEM((TOP_K, tb, d), F32), pltpu.SemaphoreType.DMA(())],
        ),
        out_shape=jax.ShapeDtypeStruct((n_tok, d), F32),
        compiler_params=_cparams("arbitrary"),
        name="expert_combine",
    )(pos, yb, x, gate, gates)


def _layer(i, x, c_all, p, n_batch, seq, n_ctx, tables, with_ctx):
    d = x.shape[1]
    rows_lat = n_batch * seq
    rows_all = rows_lat + n_batch * n_ctx
    n_out = rows_all if with_ctx else rows_lat
    n_stream = n_batch + 1

    mod = matmul(c_all, p["w_mod"], i, c_all.shape[0], c_all.shape[0], 512, F32, pre="silu", name="modulation")
    mod = (mod[:n_stream] + p["b_mod"][i]).reshape(n_stream, 6, 1, d)
    sh1, sc1, g1, sh2, sc2, g2 = (mod[:, k] for k in range(6))

    h1 = norm_mod(x, p["g_mix"][i], sc1, sh1, rows_all, rows_lat, seq, BF16)
    in_width = p["w_in"].shape[2]

    n_g = p["ssm_lambda_re"].shape[2]
    ssm_w = n_g * SSM_GROUP
    n_wh = p["win_sink"].shape[1]
    win_q = n_wh * WIN_HEAD_DIM
    win_kv = WIN_KV_HEADS * WIN_HEAD_DIM
    n_dh = (in_width - ssm_w - win_q - 2 * win_kv - 3 * d) // (2 * 2 * DIFF_QK_DIM + DIFF_V_DIM)
    diff_w = n_dh * 2 * DIFF_QK_DIM
    widths = dict(u=ssm_w, qw=win_q, kw=win_kv, vw=win_kv, qd=diff_w, kd=diff_w, vd=diff_w, gates=3 * d)
    src, c0 = {}, 0
    for name in ("u", "qw", "kw", "vw", "qd", "kd", "vd", "gates"):
        src[name] = c0
        c0 += widths[name]
    modes = dict(u=("plain", 1.0, F32), qw=("rope", WIN_HEAD_DIM ** -0.5, BF16), kw=("rope", 1.0, BF16),
                 vw=("plain", 1.0, BF16), qd=("rope", DIFF_QK_DIM ** -0.5, BF16), kd=("rope", 1.0, BF16),
                 vd=("plain", 1.0, BF16), gates=("sigmoid", 1.0, BF16))
    seg = {}
    for name, (mode, scale, dtype) in modes.items():
        w_seg = p["w_in"][i][:, src[name]:src[name] + widths[name]].astype(BF16)
        seg[name] = in_proj(h1, w_seg, 0, widths[name], mode, scale, dtype, tables, rows_lat, seq)

    mats = s5_matrices(p["ssm_lambda_re"][i], p["ssm_lambda_im"][i], p["ssm_log_dt"][i], p["ssm_b_re"][i],
                       p["ssm_b_im"][i], p["ssm_c_re"][i], p["ssm_c_im"][i], p["ssm_d"][i], S5_CHUNK)
    y = s5_mixer(seg["u"], mats, rows_all, n_batch, seq, n_ctx)
    o_ssm = s5_glu(y, p["w_glu"], p["b_glu"], i, n_out)

    sink = p["win_sink"][i].astype(F32)
    o_win = window_attention(sink, seg["qw"], seg["kw"], seg["vw"], n_batch, seq, n_ctx, n_wh, rows_lat, False)

    lam_p = p["diff_lambda"][i].astype(F32)
    lambda_init = 0.8 - 0.6 * math.exp(-0.3 * i)
    lam = (jnp.exp(jnp.sum(lam_p[0] * lam_p[1])) - jnp.exp(jnp.sum(lam_p[2] * lam_p[3])) + lambda_init).reshape(1)
    gain = p["diff_subln"][i].astype(F32).reshape(1, DIFF_V_DIM)
    o_diff = diff_attention(lam, gain, seg["qd"], seg["kd"], seg["vd"], n_batch, seq, n_ctx, n_dh, rows_lat, False,
                            1.0 - lambda_init)
    if with_ctx:
        o_win_c = window_attention(sink, seg["qw"], seg["kw"], seg["vw"], n_batch, seq, n_ctx, n_wh, rows_lat, True)
        o_diff_c = diff_attention(lam, gain, seg["qd"], seg["kd"], seg["vd"], n_batch, seq, n_ctx, n_dh, rows_lat, True,
                                  1.0 - lambda_init)
        o_win = jnp.concatenate([o_win, o_win_c], axis=0)
        o_diff = jnp.concatenate([o_diff, o_diff_c], axis=0)

    merged = gated_merge(o_ssm, o_win, o_diff, p["w_branch_ssm"], p["w_branch_win"], p["w_branch_diff"], seg["gates"],
                         i, n_out, d)
    x = out_proj_residual(merged, p["w_out"], x, g1, i, n_out, rows_lat, seq)

    hp, logits = norm_mod(x, p["g_ffn"][i], sc2, sh2, n_out, rows_lat, seq, BF16,
                          router=(p["w_router"][i], p["b_router"][i]))
    return moe_residual(x, hp, logits, g2, p["w_exp1"], p["b_exp1"], p["w_exp2"], p["b_exp2"], i, rows_lat, seq)


def kernel(x, c, ctx, c_ctx, w_mod, b_mod, g_mix, g_ffn, w_in, ssm_lambda_re, ssm_lambda_im, ssm_log_dt, ssm_b_re, ssm_b_im, ssm_c_re, ssm_c_im, ssm_d, w_glu, b_glu, win_sink, diff_lambda, diff_subln, w_branch_ssm, w_branch_win, w_branch_diff, w_out, w_router, b_router, w_exp1, b_exp1, w_exp2, b_exp2, g_final):
    n_batch, seq, d = x.shape
    n_ctx = ctx.shape[1]
    depth = w_mod.shape[0]
    p = dict(w_mod=w_mod, b_mod=b_mod, g_mix=g_mix, g_ffn=g_ffn, w_in=w_in, ssm_lambda_re=ssm_lambda_re,
             ssm_lambda_im=ssm_lambda_im, ssm_log_dt=ssm_log_dt, ssm_b_re=ssm_b_re, ssm_b_im=ssm_b_im,
             ssm_c_re=ssm_c_re, ssm_c_im=ssm_c_im, ssm_d=ssm_d, w_glu=w_glu, b_glu=b_glu, win_sink=win_sink,
             diff_lambda=diff_lambda, diff_subln=diff_subln, w_branch_ssm=w_branch_ssm, w_branch_win=w_branch_win,
             w_branch_diff=w_branch_diff, w_out=w_out, w_router=w_router, b_router=b_router, w_exp1=w_exp1,
             b_exp1=b_exp1, w_exp2=w_exp2, b_exp2=b_exp2)
    rows_lat = n_batch * seq
    rows = jnp.concatenate([x.reshape(rows_lat, d), ctx.reshape(n_batch * n_ctx, d)], axis=0)
    c_all = jnp.concatenate([c, c_ctx[None, :], jnp.zeros((8 - (n_batch + 1) % 8, d), F32)], axis=0)
    tables = rope_tables(seq, WIN_HEAD_DIM)
    for i in range(depth):
        with_ctx = i < depth - 1
        rows = _layer(i, rows, c_all, p, n_batch, seq, n_ctx, tables, with_ctx)
    no_mod = jnp.zeros((n_batch + 1, 1, d), F32)
    out = norm_mod(rows, g_final, no_mod, no_mod, rows_lat, rows_lat, seq, F32)
    return out.reshape(n_batch, seq, d)
```

```python
import functools
import math

import jax
import jax.numpy as jnp
import numpy as np
from jax import lax
from jax.experimental import pallas as pl
from jax.experimental.pallas import tpu as pltpu

F32 = jnp.float32
BF16 = jnp.bfloat16

GRID_W = 64
SSM_GROUP = 16
WIN_KV_HEADS = 2
WIN_HEAD_DIM = 64
WINDOW = 128
DIFF_QK_DIM = 64
DIFF_V_DIM = 2 * DIFF_QK_DIM
ATTN_BLOCK = 128
ROPE_BASE = 10000.0
TOP_K = 4
SWIGLU_LIMIT = 7.0
SWIGLU_ALPHA = 1.702
NORM_EPS = 1e-6
SUBLN_EPS = 1e-5
NEG_INF = -1e30

LANE = 128
VMEM_LIMIT_BYTES = 56 * 1024 * 1024
S5_CHUNK = 32
MOE_LOAD_MARGIN = 1.75
MOE_SUB_ROWS = 256
ROW_TILE = 512


def _cparams(*sem):
    return pltpu.CompilerParams(dimension_semantics=sem, vmem_limit_bytes=VMEM_LIMIT_BYTES)


def _dot(a, b):
    return jnp.dot(a, b, preferred_element_type=F32)


def _dot_nt(a, b):
    return lax.dot_general(a, b, (((1,), (1,)), ((), ())), preferred_element_type=F32)


def _sigmoid(x):
    return 1.0 / (1.0 + jnp.exp(-x))


def _row_tile(n_rows, cap):
    best = 16
    for t in range(16, cap + 1, 16):
        if n_rows % t == 0:
            best = t
    return best


def _norm_mod_kernel(x_ref, g_ref, sc_ref, sh_ref, o_ref):
    x = x_ref[...]
    y = x * lax.rsqrt(jnp.mean(x * x, axis=-1, keepdims=True) + NORM_EPS) * g_ref[...]
    o_ref[...] = (y * (1.0 + sc_ref[0]) + sh_ref[0]).astype(o_ref.dtype)


def _norm_router_kernel(x_ref, g_ref, sc_ref, sh_ref, wr_ref, br_ref, o_ref, lg_ref):
    x = x_ref[...]
    y = x * lax.rsqrt(jnp.mean(x * x, axis=-1, keepdims=True) + NORM_EPS) * g_ref[...]
    h = y * (1.0 + sc_ref[0]) + sh_ref[0]
    o_ref[...] = _pack_halves(h)
    lg_ref[...] = jnp.dot(h, wr_ref[...], preferred_element_type=F32, precision=lax.Precision.HIGHEST) + br_ref[...]


def _stream_tile(n_rows, rows_lat, seq):
    tm = math.gcd(ROW_TILE, seq)
    return math.gcd(tm, n_rows - rows_lat) if n_rows > rows_lat else tm


def _stream_index(i, tm, rows_lat, seq, n_batch):
    r = i * tm
    return jnp.where(r >= rows_lat, n_batch, r // seq)


def norm_mod(x, gain, sc, sh, n_rows, rows_lat, seq, out_dtype, router=None):
    d = x.shape[1]
    tm = _stream_tile(n_rows, rows_lat, seq)
    n_batch = sc.shape[0] - 1
    idx = functools.partial(_stream_index, tm=tm, rows_lat=rows_lat, seq=seq, n_batch=n_batch)
    in_specs = [
        pl.BlockSpec((tm, d), lambda i: (i, 0)),
        pl.BlockSpec((1, d), lambda i: (0, 0)),
        pl.BlockSpec((1, 1, d), lambda i: (idx(i), 0, 0)),
        pl.BlockSpec((1, 1, d), lambda i: (idx(i), 0, 0)),
    ]
    if router is None:
        return pl.pallas_call(
            _norm_mod_kernel,
            grid=(n_rows // tm,),
            in_specs=in_specs,
            out_specs=pl.BlockSpec((tm, d), lambda i: (i, 0)),
            out_shape=jax.ShapeDtypeStruct((n_rows, d), out_dtype),
            compiler_params=_cparams("parallel"),
            name="norm_mod",
        )(x, gain.reshape(1, d), sc, sh)
    w_router, b_router = router
    n_exp = w_router.shape[1]
    return pl.pallas_call(
        _norm_router_kernel,
        grid=(n_rows // tm,),
        in_specs=in_specs + [
            pl.BlockSpec((d, n_exp), lambda i: (0, 0)),
            pl.BlockSpec((1, n_exp), lambda i: (0, 0)),
        ],
        out_specs=[pl.BlockSpec((tm, d // 2), lambda i: (i, 0)), pl.BlockSpec((tm, n_exp), lambda i: (i, 0))],
        out_shape=[jax.ShapeDtypeStruct((n_rows, d // 2), jnp.uint32), jax.ShapeDtypeStruct((n_rows, n_exp), F32)],
        compiler_params=_cparams("parallel"),
        name="norm_router",
    )(x, gain.reshape(1, d), sc, sh, w_router, b_router.reshape(1, n_exp))


def _mm_kernel(x_ref, w_ref, o_ref, *, pre):
    x = x_ref[...]
    if pre == "silu":
        x = x * _sigmoid(x)
    o_ref[...] = _dot(x.astype(BF16), w_ref[...].astype(BF16)).astype(o_ref.dtype)


def matmul(x, w_stack, layer, n_rows, tm, tn, out_dtype, pre=None, name="matmul"):
    k = x.shape[1]
    n = w_stack.shape[2]
    return pl.pallas_call(
        functools.partial(_mm_kernel, pre=pre),
        grid=(n_rows // tm, n // tn),
        in_specs=[
            pl.BlockSpec((tm, k), lambda i, j: (i, 0)),
            pl.BlockSpec((None, k, tn), lambda i, j: (layer, 0, j)),
        ],
        out_specs=pl.BlockSpec((tm, tn), lambda i, j: (i, j)),
        out_shape=jax.ShapeDtypeStruct((n_rows, n), out_dtype),
        compiler_params=_cparams("parallel", "arbitrary"),
        name=name,
    )(x, w_stack)


def _in_proj_kernel(x_ref, w_ref, *refs, mode, scale):
    o_ref = refs[-1]
    acc = _dot(x_ref[...], w_ref[...])
    if mode == "sigmoid":
        o_ref[...] = _sigmoid(acc).astype(o_ref.dtype)
        return
    if mode == "plain":
        o_ref[...] = (acc if scale == 1.0 else acc * scale).astype(o_ref.dtype)
        return
    cos_ref, sin_ref = refs[0], refs[1]
    lane = lax.broadcasted_iota(jnp.int32, (1, LANE), 1)
    first_half = (lane % 32) < 16
    cos = cos_ref[...] if scale == 1.0 else cos_ref[...] * scale
    sin = sin_ref[...] if scale == 1.0 else sin_ref[...] * scale
    for c in range(acc.shape[1] // LANE):
        x = acc[:, c * LANE:(c + 1) * LANE]
        partner = jnp.where(first_half, pltpu.roll(x, LANE - 16, axis=1), pltpu.roll(x, 16, axis=1))
        o_ref[:, c * LANE:(c + 1) * LANE] = (x * cos + partner * sin).astype(o_ref.dtype)


def in_proj(h, w, col0, width, mode, scale, out_dtype, tables, rows_lat, seq):
    n_rows, k = h.shape
    tm = _stream_tile(n_rows, rows_lat, seq)
    tn = math.gcd(1024, width)
    assert col0 % tn == 0
    in_specs = [pl.BlockSpec((tm, k), lambda i, j: (i, 0)),
                pl.BlockSpec((k, tn), lambda i, j: (0, col0 // tn + j))]
    args = [h, w]
    if mode == "rope":
        cos_t, sin_t = tables
        tiles_per_seq = seq // tm
        lat_tiles = rows_lat // tm
        cos_x = jnp.concatenate([cos_t, jnp.ones((tm, LANE), F32)], axis=0)
        sin_x = jnp.concatenate([sin_t, jnp.zeros((tm, LANE), F32)], axis=0)
        tab = pl.BlockSpec((tm, LANE), lambda i, j: (jnp.where(i < lat_tiles, i % tiles_per_seq, tiles_per_seq), 0))
        in_specs += [tab, tab]
        args += [cos_x, sin_x]
    return pl.pallas_call(
        functools.partial(_in_proj_kernel, mode=mode, scale=scale),
        grid=(n_rows // tm, width // tn),
        in_specs=in_specs,
        out_specs=pl.BlockSpec((tm, tn), lambda i, j: (i, j)),
        out_shape=jax.ShapeDtypeStruct((n_rows, width), out_dtype),
        compiler_params=_cparams("parallel", "arbitrary"),
        name="in_proj_" + mode,
    )(*args)


def rope_tables(seq, head_dim):
    assert head_dim == 64
    rows = seq // GRID_W
    r, col = jnp.meshgrid(jnp.arange(rows, dtype=F32), jnp.arange(GRID_W, dtype=F32), indexing="ij")
    half = head_dim // 2
    inv_freq = ROPE_BASE ** (-jnp.arange(0, half, 2, dtype=F32) / half)
    ang_r = r.reshape(-1)[:, None] * inv_freq[None, :]
    ang_c = col.reshape(-1)[:, None] * inv_freq[None, :]
    cos = jnp.concatenate([jnp.cos(ang_r), jnp.cos(ang_r), jnp.cos(ang_c), jnp.cos(ang_c)], axis=-1)
    sin = jnp.concatenate([-jnp.sin(ang_r), jnp.sin(ang_r), -jnp.sin(ang_c), jnp.sin(ang_c)], axis=-1)
    return jnp.tile(cos, (1, LANE // head_dim)), jnp.tile(sin, (1, LANE // head_dim))


def _win_attn_kernel(sink_ref, q_ref, *refs, n_heads, band, seq):
    if band:
        kp_ref, kc_ref, kn_ref, vp_ref, vc_ref, vn_ref, kx_ref, vx_ref, o_ref = refs
    else:
        kx_ref, vx_ref, o_ref = refs
    blk = q_ref.shape[0]
    dh = WIN_HEAD_DIM
    grp = n_heads // WIN_KV_HEADS
    lane = lax.broadcasted_iota(jnp.int32, (1, LANE), 1)
    rows = grp * blk
    if band:
        n = pl.program_id(1)
        k_all = jnp.concatenate([kp_ref[...], kc_ref[...], kn_ref[...], kx_ref[...]], axis=0)
        v_all = jnp.concatenate([vp_ref[...], vc_ref[...], vn_ref[...], vx_ref[...]], axis=0)
        n_keys = k_all.shape[0]
        col = lax.broadcasted_iota(jnp.int32, (rows, n_keys), 1)
        qpos = n * blk + (lax.broadcasted_iota(jnp.int32, (rows, n_keys), 0) & (blk - 1))
        kpos = (n - 1) * blk + col
        mask = (col >= 3 * blk) | ((jnp.abs(qpos - kpos) <= WINDOW) & (kpos >= 0) & (kpos < seq))
    else:
        k_all, v_all = kx_ref[...], vx_ref[...]
        n_keys = k_all.shape[0]
    ones_col = (lax.broadcasted_iota(jnp.int32, (n_keys, LANE), 1) == 0).astype(BF16)
    v_aug = jnp.concatenate([v_all, ones_col], axis=1)
    outs = [None] * n_heads
    for kvh in range(WIN_KV_HEADS):
        keep = (lane >= kvh * dh) & (lane < (kvh + 1) * dh)
        q_parts, sink_parts = [], []
        for g in range(grp):
            h = kvh * grp + g
            c = (h * dh) // LANE
            qc = q_ref[:, c * LANE:(c + 1) * LANE]
            if (h * dh) % LANE != kvh * dh:
                qc = jnp.concatenate([qc[:, dh:], qc[:, :dh]], axis=1)
            q_parts.append(jnp.where(keep, qc, jnp.zeros_like(qc)))
            sink_parts.append(jnp.full((blk, 1), sink_ref[h], F32))
        qs = jnp.concatenate(q_parts, axis=0)
        sink = jnp.concatenate(sink_parts, axis=0)
        s = _dot_nt(qs, k_all)
        if band:
            s = jnp.where(mask, s, NEG_INF)
        m = jnp.maximum(jnp.max(s, axis=-1, keepdims=True), sink)
        acc = _dot(jnp.exp(s - m).astype(BF16), v_aug)
        inv = 1.0 / (acc[:, LANE:LANE + 1] + jnp.exp(sink - m))
        o = acc[:, kvh * dh:(kvh + 1) * dh] * inv
        for g in range(grp):
            outs[kvh * grp + g] = o[g * blk:(g + 1) * blk, :]
    o_ref[...] = jnp.concatenate(outs, axis=1).astype(o_ref.dtype)


def window_attention(sink, q, k, v, n_batch, seq, n_ctx, n_heads, rows_lat, ctx_queries):
    blk = ATTN_BLOCK
    seq_q = n_ctx if ctx_queries else seq
    nb = seq_q // blk
    q0 = rows_lat // blk if ctx_queries else 0
    x0 = rows_lat // n_ctx
    qw = n_heads * WIN_HEAD_DIM
    kvw = WIN_KV_HEADS * WIN_HEAD_DIM
    in_specs = [
        pl.BlockSpec(memory_space=pltpu.SMEM),
        pl.BlockSpec((blk, qw), lambda b, n: (q0 + b * nb + n, 0)),
    ]
    args = [sink, q]
    if not ctx_queries:
        band_specs = [
            pl.BlockSpec((blk, kvw), lambda b, n: (b * nb + jnp.maximum(n - 1, 0), 0)),
            pl.BlockSpec((blk, kvw), lambda b, n: (b * nb + n, 0)),
            pl.BlockSpec((blk, kvw), lambda b, n: (b * nb + jnp.minimum(n + 1, nb - 1), 0)),
        ]
        in_specs += band_specs + band_specs
        args += [k, k, k, v, v, v]
    in_specs += [pl.BlockSpec((n_ctx, kvw), lambda b, n: (x0 + b, 0))] * 2
    args += [k, v]
    return pl.pallas_call(
        functools.partial(_win_attn_kernel, n_heads=n_heads, band=not ctx_queries, seq=seq),
        grid=(n_batch, nb),
        in_specs=in_specs,
        out_specs=pl.BlockSpec((blk, qw), lambda b, n: (b * nb + n, 0)),
        out_shape=jax.ShapeDtypeStruct((n_batch * seq_q, qw), BF16),
        compiler_params=_cparams("parallel", "parallel"),
        name="context_window_attention" if ctx_queries else "window_attention",
    )(*args)


def _diff_attn_kernel(lam_ref, q_ref, gain_ref, *refs, with_lat, post_scale):
    if with_lat:
        kl_ref, vl_ref, kx_ref, vx_ref, o_ref = refs
    else:
        kx_ref, vx_ref, o_ref = refs
    lam = lam_ref[0]
    q = q_ref[...]
    lane = lax.broadcasted_iota(jnp.int32, (1, LANE), 1)
    zero = jnp.zeros_like(q)
    kx = kx_ref[...]

    def softmax_parts(qm):
        s_x = _dot_nt(qm, kx)
        m = jnp.max(s_x, axis=-1, keepdims=True)
        if with_lat:
            s_l = _dot_nt(qm, kl_ref[...])
            m = jnp.maximum(m, jnp.max(s_l, axis=-1, keepdims=True))
            e_l = jnp.exp(s_l - m)
        else:
            e_l = None
        e_x = jnp.exp(s_x - m)
        den = jnp.sum(e_x, axis=-1, keepdims=True)
        if with_lat:
            den = den + jnp.sum(e_l, axis=-1, keepdims=True)
        return e_l, e_x, 1.0 / den

    e1l, e1x, inv1 = softmax_parts(jnp.where(lane < DIFF_QK_DIM, q, zero))
    e2l, e2x, inv2 = softmax_parts(jnp.where(lane >= DIFF_QK_DIM, q, zero))
    w2 = lam * inv2
    o = _dot((e1x * inv1 - e2x * w2).astype(BF16), vx_ref[...])
    if with_lat:
        o = o + _dot((e1l * inv1 - e2l * w2).astype(BF16), vl_ref[...])
    o = o * lax.rsqrt(jnp.mean(o * o, axis=-1, keepdims=True) + SUBLN_EPS) * gain_ref[...]
    o_ref[...] = (o * post_scale).astype(o_ref.dtype)


def diff_attention(lam, gain, q, k, v, n_batch, seq, n_ctx, n_heads, rows_lat, ctx_queries, post_scale):
    seq_q = n_ctx if ctx_queries else seq
    tq = min(256, seq_q)
    nq = seq_q // tq
    q0 = rows_lat // tq if ctx_queries else 0
    x0 = rows_lat // n_ctx
    in_specs = [
        pl.BlockSpec(memory_space=pltpu.SMEM),
        pl.BlockSpec((tq, LANE), lambda b, h, i: (q0 + b * nq + i, h)),
        pl.BlockSpec((1, LANE), lambda b, h, i: (0, 0)),
    ]
    args = [lam, q, gain]
    if not ctx_queries:
        in_specs += [pl.BlockSpec((seq, LANE), lambda b, h, i: (b, h))] * 2
        args += [k, v]
    in_specs += [pl.BlockSpec((n_ctx, LANE), lambda b, h, i: (x0 + b, h))] * 2
    args += [k, v]
    return pl.pallas_call(
        functools.partial(_diff_attn_kernel, with_lat=not ctx_queries, post_scale=post_scale),
        grid=(n_batch, n_heads, nq),
        in_specs=in_specs,
        out_specs=pl.BlockSpec((tq, LANE), lambda b, h, i: (b * nq + i, h)),
        out_shape=jax.ShapeDtypeStruct((n_batch * seq_q, n_heads * LANE), BF16),
        compiler_params=_cparams("parallel", "parallel", "arbitrary"),
        name="context_diff_attention" if ctx_queries else "diff_attention",
    )(*args)


def s5_matrices(lam_re, lam_im, log_dt, b_re, b_im, c_re, c_im, d_skip, t_chunk):
    n_dir, n_g, n_p = lam_re.shape
    n_h = b_re.shape[-1]
    lam_re = jnp.minimum(lam_re.astype(F32), -1e-4)
    lam_im = lam_im.astype(F32)
    dt = jnp.exp(log_dt.astype(F32))[..., None]
    mag = jnp.exp(lam_re * dt)
    a_re = mag * jnp.cos(lam_im * dt)
    a_im = mag * jnp.sin(lam_im * dt)
    den = lam_re * lam_re + lam_im * lam_im
    k_re = ((a_re - 1.0) * lam_re + a_im * lam_im) / den
    k_im = (a_im * lam_re - (a_re - 1.0) * lam_im) / den
    b_re = b_re.astype(F32)
    b_im = b_im.astype(F32)
    bb_re = k_re[..., None] * b_re - k_im[..., None] * b_im
    bb_im = k_re[..., None] * b_im + k_im[..., None] * b_re
    c_re = c_re.astype(F32)
    c_im = c_im.astype(F32)
    ldt_re = lam_re * dt
    ldt_im = lam_im * dt
    steps = jnp.arange(t_chunk, dtype=F32)

    def power(d, t, p_last):
        lr = ldt_re[d][:, None, :] if p_last else ldt_re[d][:, :, None]
        li = ldt_im[d][:, None, :] if p_last else ldt_im[d][:, :, None]
        tt = t[None, :, None] if p_last else t[None, None, :]
        mag_t = jnp.exp(lr * tt)
        return mag_t * jnp.cos(li * tt), mag_t * jnp.sin(li * tt)

    def c_times_power(d, t):
        pr, pi = power(d, t, False)
        cr = jnp.swapaxes(c_re[d], 1, 2)[:, :, None, :]
        ci = jnp.swapaxes(c_im[d], 1, 2)[:, :, None, :]
        re = cr * pr[..., None] - ci * pi[..., None]
        im = cr * pi[..., None] + ci * pr[..., None]
        return jnp.concatenate([re, -im], axis=1).reshape(n_g, 2 * n_p, t_chunk * n_h)

    def b_times_power(d, t):
        pr, pi = power(d, t, True)
        br = jnp.swapaxes(bb_re[d], 1, 2)[:, None, :, :]
        bi = jnp.swapaxes(bb_im[d], 1, 2)[:, None, :, :]
        re = pr[:, :, None, :] * br - pi[:, :, None, :] * bi
        im = pr[:, :, None, :] * bi + pi[:, :, None, :] * br
        return jnp.concatenate([re, im], axis=-1).reshape(n_g, t_chunk * n_h, 2 * n_p)

    ca_lag = jnp.stack([c_times_power(0, steps), c_times_power(1, t_chunk - 1 - steps)])
    bbt = jnp.stack([jnp.concatenate([jnp.swapaxes(bb_re[d], 1, 2), jnp.swapaxes(bb_im[d], 1, 2)], axis=-1)
                     for d in range(2)])
    p_mat = jnp.concatenate([b_times_power(0, t_chunk - 1 - steps), b_times_power(1, steps)], axis=-1)
    q_mat = jnp.concatenate([c_times_power(0, steps + 1.0), c_times_power(1, t_chunk - steps)], axis=1)
    d_tile = jnp.tile(d_skip.astype(F32).reshape(n_g, 1, n_h), (1, 1, t_chunk))
    mag_t = jnp.exp(ldt_re * t_chunk)
    at_re, at_im = mag_t * jnp.cos(ldt_im * t_chunk), mag_t * jnp.sin(ldt_im * t_chunk)
    at_mul = jnp.concatenate([at_re, at_re], axis=-1)
    at_swp = jnp.concatenate([-at_im, at_im], axis=-1)
    return ca_lag, bbt, p_mat.astype(BF16), q_mat.astype(BF16), d_tile, at_mul, at_swp


def _s5_local_kernel(x_ref, p_ref, e_ref):
    e_ref[...] = _dot(x_ref[...].astype(BF16), p_ref[...])


def _s5_out_kernel(x_ref, s_ref, ca_ref, bbt_ref, q_ref, d_ref, y_ref, m_ref):
    n_h = bbt_ref.shape[1]
    th = m_ref.shape[0]
    t_chunk = th // n_h
    hi = lax.Precision.HIGHEST
    row_f = jnp.dot(bbt_ref[0], ca_ref[0], preferred_element_type=F32, precision=hi)
    row_b = jnp.dot(bbt_ref[1], ca_ref[1], preferred_element_type=F32, precision=hi)
    lane = lax.broadcasted_iota(jnp.int32, (n_h, th), 1)
    for j in range(t_chunk):
        fwd = jnp.where(lane >= n_h * j, pltpu.roll(row_f, n_h * j, axis=1), 0.0) if j else row_f
        back = n_h * (t_chunk - 1 - j)
        bwd = jnp.where(lane < n_h * (j + 1), pltpu.roll(row_b, th - back, axis=1), 0.0) if back else row_b
        m_ref[n_h * j:n_h * (j + 1), :] = (fwd + bwd).astype(m_ref.dtype)
    x = x_ref[...]
    y_ref[...] = _dot(x.astype(BF16), m_ref[...]) + _dot(s_ref[...].astype(BF16), q_ref[...]) + x * d_ref[...]


def _s5_scan_kernel(ef_ref, eb_ref, mul_ref, swp_ref, sf_ref, sb_ref, *, n_batch, lat_chunks, ctx_chunks):
    half = ef_ref.shape[2] // 2
    mul_f, mul_b = mul_ref[0], mul_ref[1]
    swp_f, swp_b = swp_ref[0], swp_ref[1]
    zero = jnp.zeros(ef_ref.shape[1:], F32)

    def step(c_f, c_b, carry):
        s_f, s_b = carry
        sf_ref[c_f] = s_f
        sb_ref[c_b] = s_b
        s_f = s_f * mul_f + pltpu.roll(s_f, half, axis=1) * swp_f + ef_ref[c_f]
        s_b = s_b * mul_b + pltpu.roll(s_b, half, axis=1) * swp_b + eb_ref[c_b]
        return s_f, s_b

    for b in range(n_batch):
        ctx0 = n_batch * lat_chunks + b * ctx_chunks
        lat0 = b * lat_chunks
        carry = lax.fori_loop(0, ctx_chunks, lambda i, cr: step(ctx0 + i, ctx0 + ctx_chunks - 1 - i, cr), (zero, zero))
        lax.fori_loop(0, lat_chunks, lambda i, cr: step(lat0 + i, lat0 + lat_chunks - 1 - i, cr), carry)


def s5_mixer(proj, mats, n_rows, n_batch, seq, n_ctx):
    ca_lag, bbt, p_mat, q_mat, d_tile, at_mul, at_swp = mats
    n_g, th, _ = p_mat.shape
    n_h = SSM_GROUP
    t_chunk = th // n_h
    p4 = p_mat.shape[2]
    n_p2 = p4 // 2
    width = n_g * n_h
    n_chunks = n_rows // t_chunk
    u = proj[:, :width]
    x = u.reshape(n_chunks, t_chunk, n_g, n_h).transpose(2, 0, 1, 3).reshape(n_g, n_chunks, th)
    e = pl.pallas_call(
        _s5_local_kernel,
        grid=(n_g,),
        in_specs=[pl.BlockSpec((None, n_chunks, th), lambda g: (g, 0, 0)),
                  pl.BlockSpec((None, th, p4), lambda g: (g, 0, 0))],
        out_specs=pl.BlockSpec((n_chunks, p4), lambda g: (0, g)),
        out_shape=jax.ShapeDtypeStruct((n_chunks, n_g * p4), F32),
        compiler_params=_cparams("parallel"),
        name="s5_local_state",
    )(x, p_mat)
    e = e.reshape(n_chunks, n_g, 2, n_p2)
    gb = 16
    spec = pl.BlockSpec((n_chunks, gb, n_p2), lambda g: (0, g, 0))
    tab = pl.BlockSpec((2, gb, n_p2), lambda g: (0, g, 0))
    s_f, s_b = pl.pallas_call(
        functools.partial(_s5_scan_kernel, n_batch=n_batch, lat_chunks=seq // t_chunk, ctx_chunks=n_ctx // t_chunk),
        grid=(n_g // gb,),
        in_specs=[spec, spec, tab, tab],
        out_specs=[spec, spec],
        out_shape=[jax.ShapeDtypeStruct((n_chunks, n_g, n_p2), F32)] * 2,
        compiler_params=_cparams("parallel"),
        name="s5_chunk_scan",
    )(e[:, :, 0], e[:, :, 1], at_mul, at_swp)
    s_in = jnp.stack([s_f, s_b], axis=2).reshape(n_chunks, n_g * p4)
    y = pl.pallas_call(
        _s5_out_kernel,
        grid=(n_g,),
        in_specs=[pl.BlockSpec((None, n_chunks, th), lambda g: (g, 0, 0)),
                  pl.BlockSpec((n_chunks, p4), lambda g: (0, g)),
                  pl.BlockSpec((2, None, p4 // 2, th), lambda g: (0, g, 0, 0)),
                  pl.BlockSpec((2, None, n_h, p4 // 2), lambda g: (0, g, 0, 0)),
                  pl.BlockSpec((None, p4, th), lambda g: (g, 0, 0)),
                  pl.BlockSpec((None, 1, th), lambda g: (g, 0, 0))],
        out_specs=pl.BlockSpec((None, n_chunks, th), lambda g: (g, 0, 0)),
        out_shape=jax.ShapeDtypeStruct((n_g, n_chunks, th), F32),
        scratch_shapes=[pltpu.VMEM((th, th), BF16)],
        compiler_params=_cparams("parallel"),
        name="s5_output",
    )(x, s_in, ca_lag, bbt, q_mat, d_tile)
    return y.reshape(n_g, n_chunks, t_chunk, n_h).transpose(1, 2, 0, 3).reshape(n_rows, width)


def _glu_kernel(y_ref, w_ref, b_ref, o_ref):
    y = y_ref[...]
    gy = 0.5 * y * (1.0 + jnp.tanh(math.sqrt(2.0 / math.pi) * (y + 0.044715 * (y * y * y))))
    z = _dot(gy.astype(BF16), w_ref[...].astype(BF16)) + b_ref[...]
    o_ref[...] = (gy * _sigmoid(z)).astype(o_ref.dtype)


def s5_glu(y, w_glu, b_glu, layer, n_rows):
    width = y.shape[1]
    tm = _row_tile(n_rows, ROW_TILE)
    return pl.pallas_call(
        _glu_kernel,
        grid=(n_rows // tm,),
        in_specs=[pl.BlockSpec((tm, width), lambda i: (i, 0)),
                  pl.BlockSpec((None, width, width), lambda i: (layer, 0, 0)),
                  pl.BlockSpec((None, 1, width), lambda i: (layer, 0, 0))],
        out_specs=pl.BlockSpec((tm, width), lambda i: (i, 0)),
        out_shape=jax.ShapeDtypeStruct((n_rows, width), BF16),
        compiler_params=_cparams("parallel"),
        name="s5_glu",
    )(y, w_glu, b_glu.reshape(b_glu.shape[0], 1, width))


def _merge_kernel(os_ref, ow_ref, od_ref, ws_ref, ww_ref, wd_ref, gs_ref, gw_ref, gd_ref, o_ref):
    m = (gs_ref[...].astype(F32) * _dot(os_ref[...], ws_ref[...].astype(BF16))
         + gw_ref[...].astype(F32) * _dot(ow_ref[...], ww_ref[...].astype(BF16))
         + gd_ref[...].astype(F32) * _dot(od_ref[...], wd_ref[...].astype(BF16)))
    o_ref[...] = m.astype(o_ref.dtype)


def gated_merge(o_ssm, o_win, o_diff, w_s, w_w, w_d, proj, layer, n_rows, d):
    tm = _row_tile(n_rows, 1088)
    tn = 256
    assert d % tn == 0
    g0 = 0
    nd = d // tn

    def branch(arr):
        return pl.BlockSpec((tm, arr.shape[1]), lambda i, j: (i, 0))

    def weight(w):
        return pl.BlockSpec((None, w.shape[1], tn), lambda i, j: (layer, 0, j))

    def gate(k):
        return pl.BlockSpec((tm, tn), lambda i, j: (i, g0 + k * nd + j))

    return pl.pallas_call(
        _merge_kernel,
        grid=(n_rows // tm, nd),
        in_specs=[branch(o_ssm), branch(o_win), branch(o_diff), weight(w_s), weight(w_w), weight(w_d),
                  gate(0), gate(1), gate(2)],
        out_specs=pl.BlockSpec((tm, tn), lambda i, j: (i, j)),
        out_shape=jax.ShapeDtypeStruct((n_rows, d), BF16),
        compiler_params=_cparams("parallel", "arbitrary"),
        name="gated_merge",
    )(o_ssm, o_win, o_diff, w_s, w_w, w_d, proj, proj, proj)


def _out_proj_kernel(m_ref, w_ref, x_ref, g_ref, o_ref):
    o_ref[...] = x_ref[...] + g_ref[0] * _dot(m_ref[...], w_ref[...].astype(BF16))


def out_proj_residual(m, w_out, x, gate, layer, n_rows, rows_lat, seq):
    d = x.shape[1]
    tm = _stream_tile(n_rows, rows_lat, seq)
    tn = min(1024, d)
    n_batch = gate.shape[0] - 1
    idx = functools.partial(_stream_index, tm=tm, rows_lat=rows_lat, seq=seq, n_batch=n_batch)
    return pl.pallas_call(
        _out_proj_kernel,
        grid=(n_rows // tm, d // tn),
        in_specs=[pl.BlockSpec((tm, m.shape[1]), lambda i, j: (i, 0)),
                  pl.BlockSpec((None, m.shape[1], tn), lambda i, j: (layer, 0, j)),
                  pl.BlockSpec((tm, tn), lambda i, j: (i, j)),
                  pl.BlockSpec((1, 1, tn), lambda i, j: (idx(i), 0, j))],
        out_specs=pl.BlockSpec((tm, tn), lambda i, j: (i, j)),
        out_shape=jax.ShapeDtypeStruct((n_rows, d), F32),
        compiler_params=_cparams("parallel", "arbitrary"),
        name="out_proj_residual",
    )(m, w_out, x, gate)


def _pack_halves(h):
    half = h.shape[1] // 2
    lo = lax.bitcast_convert_type(h[:, :half].astype(BF16).astype(F32), jnp.uint32)
    hi = lax.bitcast_convert_type(h[:, half:].astype(BF16).astype(F32), jnp.uint32)
    return (lo >> 16) | (hi & jnp.uint32(0xFFFF0000))


def _unpack_halves(x):
    lo = lax.bitcast_convert_type(x << 16, F32).astype(BF16)
    hi = lax.bitcast_convert_type(x & jnp.uint32(0xFFFF0000), F32).astype(BF16)
    return lo, hi


def _dispatch_kernel(pos_ref, pad_lo_ref, pad_hi_ref, h_ref, xb_ref, zero_ref, sem):
    tb = h_ref.shape[0]
    base = pl.program_id(0) * (tb * TOP_K)

    @pl.when(pl.program_id(0) == 0)
    def _():
        zero_ref[...] = jnp.zeros_like(zero_ref)

        def fill(row, carry):
            pltpu.make_async_copy(zero_ref.at[pl.ds(0, 1)], xb_ref.at[pl.ds(row, 1)], sem).start()
            return carry

        def fill_done(row, carry):
            pltpu.make_async_copy(zero_ref.at[pl.ds(0, 1)], xb_ref.at[pl.ds(row, 1)], sem).wait()
            return carry

        def per_expert(body):
            lax.fori_loop(0, pad_lo_ref.shape[0], lambda e, c: lax.fori_loop(pad_lo_ref[e], pad_hi_ref[e], body, c), 0)

        per_expert(fill)
        per_expert(fill_done)

    def issue(j, carry):
        for k in range(TOP_K):
            dst = pos_ref[base + j * TOP_K + k]
            pltpu.make_async_copy(h_ref.at[pl.ds(j, 1)], xb_ref.at[pl.ds(dst, 1)], sem).start()
        return carry

    lax.fori_loop(0, tb, issue, 0, unroll=8)

    def drain(j, carry):
        for k in range(TOP_K):
            pltpu.make_async_copy(h_ref.at[pl.ds(j, 1)], xb_ref.at[pl.ds(0, 1)], sem).wait()
        return carry

    lax.fori_loop(0, tb, drain, 0, unroll=8)


def _combine_kernel(pos_ref, yb_ref, x_ref, g_ref, gates_ref, o_ref, buf, sem):
    tb = x_ref.shape[0]
    base = pl.program_id(0) * (tb * TOP_K)

    def issue(j, carry):
        for k in range(TOP_K):
            src = pos_ref[base + j * TOP_K + k]
            pltpu.make_async_copy(yb_ref.at[pl.ds(src, 1)], buf.at[k, pl.ds(j, 1)], sem).start()
        return carry

    lax.fori_loop(0, tb, issue, 0, unroll=8)

    def drain(j, carry):
        for k in range(TOP_K):
            pltpu.make_async_copy(yb_ref.at[pl.ds(0, 1)], buf.at[k, pl.ds(j, 1)], sem).wait()
        return carry

    lax.fori_loop(0, tb, drain, 0, unroll=8)
    gates = gates_ref[...]
    f = gates[:, 0:1] * buf[0]
    for k in range(1, TOP_K):
        f = f + gates[:, k:k + 1] * buf[k]
    o_ref[...] = x_ref[...] + g_ref[0] * f


def _expert_up_kernel(be_ref, nu_ref, nv_ref, x_ref, wg_ref, wl_ref, bg_ref, bl_ref, o_ref, wg_s, wl_s):
    r = pl.program_id(0)

    @pl.when(r < nu_ref[0])
    def _():
        wg_s[...] = wg_ref[...].astype(BF16)
        wl_s[...] = wl_ref[...].astype(BF16)
        half = x_ref.shape[1]
        for s in range(x_ref.shape[0] // MOE_SUB_ROWS):
            @pl.when(s * MOE_SUB_ROWS < nv_ref[r])
            def _(s=s):
                rows = slice(s * MOE_SUB_ROWS, (s + 1) * MOE_SUB_ROWS)
                lo, hi = _unpack_halves(x_ref[rows, :])
                glu = _dot(lo, wg_s[:half, :]) + _dot(hi, wg_s[half:, :]) + bg_ref[...]
                lin = _dot(lo, wl_s[:half, :]) + _dot(hi, wl_s[half:, :]) + bl_ref[...]
                glu = jnp.minimum(glu, SWIGLU_LIMIT)
                lin = jnp.clip(lin, -SWIGLU_LIMIT, SWIGLU_LIMIT)
                o_ref[rows, :] = (glu * _sigmoid(SWIGLU_ALPHA * glu) * (lin + 1.0)).astype(o_ref.dtype)


def _expert_down_kernel(be_ref, nu_ref, nv_ref, a_ref, w_ref, b_ref, o_ref, w_s):
    r = pl.program_id(0)

    @pl.when(r < nu_ref[0])
    def _():
        w_s[...] = w_ref[...].astype(BF16)
        for s in range(a_ref.shape[0] // MOE_SUB_ROWS):
            @pl.when(s * MOE_SUB_ROWS < nv_ref[r])
            def _(s=s):
                rows = slice(s * MOE_SUB_ROWS, (s + 1) * MOE_SUB_ROWS)
                o_ref[rows, :] = _dot(a_ref[rows, :], w_s[...]) + b_ref[...]


def expert_ffn(block_exp, n_used, n_valid, xb, w1, b1, w2, b2, layer, block_rows):
    n_rows, half = xb.shape
    d = 2 * half
    n_exp, _, f2 = w1.shape[1:]
    f = f2 // 2
    n_blocks = n_rows // block_rows
    tf = min(512, f)
    nf = f // tf
    b1r = b1.reshape(b1.shape[0], n_exp, 1, f2)
    b2r = b2.reshape(b2.shape[0], n_exp, 1, d)

    def rows(r, nu):
        return jnp.minimum(r, nu[0] - 1)

    def tile(r, j, nu, n_tiles):
        return jnp.where(r < nu[0], j, n_tiles - 1)

    act = pl.pallas_call(
        _expert_up_kernel,
        grid_spec=pltpu.PrefetchScalarGridSpec(
            num_scalar_prefetch=3,
            grid=(n_blocks, nf),
            in_specs=[
                pl.BlockSpec((block_rows, half), lambda r, j, be, nu, nv: (rows(r, nu), 0)),
                pl.BlockSpec((None, None, d, tf), lambda r, j, be, nu, nv: (layer, be[rows(r, nu)], 0, tile(r, j, nu, nf))),
                pl.BlockSpec((None, None, d, tf),
                             lambda r, j, be, nu, nv: (layer, be[rows(r, nu)], 0, nf + tile(r, j, nu, nf))),
                pl.BlockSpec((None, None, 1, tf), lambda r, j, be, nu, nv: (layer, be[rows(r, nu)], 0, tile(r, j, nu, nf))),
                pl.BlockSpec((None, None, 1, tf),
                             lambda r, j, be, nu, nv: (layer, be[rows(r, nu)], 0, nf + tile(r, j, nu, nf))),
            ],
            out_specs=pl.BlockSpec((block_rows, tf), lambda r, j, be, nu, nv: (rows(r, nu), tile(r, j, nu, nf))),
            scratch_shapes=[pltpu.VMEM((d, tf), BF16), pltpu.VMEM((d, tf), BF16)],
        ),
        out_shape=jax.ShapeDtypeStruct((n_rows, f), BF16),
        compiler_params=_cparams("arbitrary", "arbitrary"),
        name="expert_up",
    )(block_exp, n_used, n_valid, xb, w1, w1, b1r, b1r)
    tn = min(512, d)
    nd = d // tn
    return pl.pallas_call(
        _expert_down_kernel,
        grid_spec=pltpu.PrefetchScalarGridSpec(
            num_scalar_prefetch=3,
            grid=(n_blocks, nd),
            in_specs=[
                pl.BlockSpec((block_rows, f), lambda r, j, be, nu, nv: (rows(r, nu), 0)),
                pl.BlockSpec((None, None, f, tn), lambda r, j, be, nu, nv: (layer, be[rows(r, nu)], 0, tile(r, j, nu, nd))),
                pl.BlockSpec((None, None, 1, tn), lambda r, j, be, nu, nv: (layer, be[rows(r, nu)], 0, tile(r, j, nu, nd))),
            ],
            out_specs=pl.BlockSpec((block_rows, tn), lambda r, j, be, nu, nv: (rows(r, nu), tile(r, j, nu, nd))),
            scratch_shapes=[pltpu.VMEM((f, tn), BF16)],
        ),
        out_shape=jax.ShapeDtypeStruct((n_rows, d), F32),
        compiler_params=_cparams("arbitrary", "arbitrary"),
        name="expert_down",
    )(block_exp, n_used, n_valid, act, w2, b2r)


def _blocked_cumsum(onehot):
    n, e = onehot.shape
    blk = math.gcd(n, 512)
    x = onehot.reshape(n // blk, blk, e).astype(BF16)
    tril = jnp.tril(jnp.ones((blk, blk), BF16))
    within = jnp.einsum("ij,bje->bie", tril, x, preferred_element_type=F32).astype(jnp.int32)
    totals = within[:, -1, :]
    offsets = jnp.cumsum(totals, axis=0) - totals
    return (within + offsets[:, None, :]).reshape(n, e)


def moe_residual(x, hp, logits, gate, w1, b1, w2, b2, layer, rows_lat, seq):
    n_tok, half = hp.shape
    d = 2 * half
    n_exp = w1.shape[1]
    top_val, top_idx = lax.top_k(logits, TOP_K)
    gates = jax.nn.softmax(top_val, axis=-1)
    n_assign = n_tok * TOP_K
    flat_e = top_idx.reshape(-1)
    onehot = (flat_e[:, None] == jnp.arange(n_exp, dtype=flat_e.dtype)[None, :]).astype(jnp.int32)
    csum = _blocked_cumsum(onehot)
    counts = csum[-1]
    block_rows = MOE_SUB_ROWS * -(-int(MOE_LOAD_MARGIN * n_assign / n_exp) // MOE_SUB_ROWS)
    padded = (counts + block_rows - 1) // block_rows * block_rows
    pend = jnp.cumsum(padded)
    pstart = pend - padded
    pos = jnp.sum(onehot * (csum - 1 + pstart[None, :]), axis=1).astype(jnp.int32)
    n_blocks = -(-n_assign // block_rows) + n_exp
    n_rows = n_blocks * block_rows
    block_start = jnp.arange(n_blocks, dtype=jnp.int32) * block_rows
    block_exp = jnp.minimum(jnp.sum((block_start[:, None] >= pend[None, :]).astype(jnp.int32), axis=1), n_exp - 1)
    n_used = (pend[-1] // block_rows).astype(jnp.int32).reshape(1)
    last_row = (pstart + counts)[block_exp]
    n_valid = jnp.clip(last_row - block_start, 0, block_rows).astype(jnp.int32)
    pad_lo = (pstart + counts).astype(jnp.int32)
    pad_hi = (pstart + (counts + MOE_SUB_ROWS - 1) // MOE_SUB_ROWS * MOE_SUB_ROWS).astype(jnp.int32)

    tb = math.gcd(128, _stream_tile(n_tok, rows_lat, seq))
    xb = pl.pallas_call(
        _dispatch_kernel,
        grid_spec=pltpu.PrefetchScalarGridSpec(
            num_scalar_prefetch=3,
            grid=(n_tok // tb,),
            in_specs=[pl.BlockSpec((tb, half), lambda i, pos, lo, hi: (i, 0))],
            out_specs=pl.BlockSpec(memory_space=pl.ANY),
            scratch_shapes=[pltpu.VMEM((8, half), jnp.uint32), pltpu.SemaphoreType.DMA(())],
        ),
        out_shape=jax.ShapeDtypeStruct((n_rows, half), jnp.uint32),
        compiler_params=_cparams("arbitrary"),
        name="expert_dispatch",
    )(pos, pad_lo, pad_hi, hp)
    yb = expert_ffn(block_exp.astype(jnp.int32), n_used, n_valid, xb, w1, b1, w2, b2, layer, block_rows)
    n_batch = gate.shape[0] - 1
    idx = functools.partial(_stream_index, tm=tb, rows_lat=rows_lat, seq=seq, n_batch=n_batch)
    return pl.pallas_call(
        _combine_kernel,
        grid_spec=pltpu.PrefetchScalarGridSpec(
            num_scalar_prefetch=1,
            grid=(n_tok // tb,),
            in_specs=[pl.BlockSpec(memory_space=pl.ANY),
                      pl.BlockSpec((tb, d), lambda i, pos: (i, 0)),
                      pl.BlockSpec((1, 1, d), lambda i, pos: (idx(i), 0, 0)),
                      pl.BlockSpec((tb, TOP_K), lambda i, pos: (i, 0))],
            out_specs=pl.BlockSpec((tb, d), lambda i, pos: (i, 0)),
            scratch_shapes=[pltpu.VMEM((TOP_K, tb, d), F32), pltpu.SemaphoreType.DMA(())],
        ),
        out_shape=jax.ShapeDtypeStruct((n_tok, d), F32),
        compiler_params=_cparams("arbitrary"),
        name="expert_combine",
    )(pos, yb, x, gate, gates)


def _layer(i, x, c_all, p, n_batch, seq, n_ctx, tables, with_ctx):
    d = x.shape[1]
    rows_lat = n_batch * seq
    rows_all = rows_lat + n_batch * n_ctx
    n_out = rows_all if with_ctx else rows_lat
    n_stream = n_batch + 1

    mod = matmul(c_all, p["w_mod"], i, c_all.shape[0], c_all.shape[0], 512, F32, pre="silu", name="modulation")
    mod = (mod[:n_stream] + p["b_mod"][i]).reshape(n_stream, 6, 1, d)
    sh1, sc1, g1, sh2, sc2, g2 = (mod[:, k] for k in range(6))

    h1 = norm_mod(x, p["g_mix"][i], sc1, sh1, rows_all, rows_lat, seq, BF16)
    in_width = p["w_in"].shape[2]

    n_g = p["ssm_lambda_re"].shape[2]
    ssm_w = n_g * SSM_GROUP
    n_wh = p["win_sink"].shape[1]
    win_q = n_wh * WIN_HEAD_DIM
    win_kv = WIN_KV_HEADS * WIN_HEAD_DIM
    n_dh = (in_width - ssm_w - win_q - 2 * win_kv - 3 * d) // (2 * 2 * DIFF_QK_DIM + DIFF_V_DIM)
    diff_w = n_dh * 2 * DIFF_QK_DIM
    widths = dict(u=ssm_w, qw=win_q, kw=win_kv, vw=win_kv, qd=diff_w, kd=diff_w, vd=diff_w, gates=3 * d)
    src, c0 = {}, 0
    for name in ("u", "qw", "kw", "vw", "qd", "kd", "vd", "gates"):
        src[name] = c0
        c0 += widths[name]
    modes = dict(u=("plain", 1.0, F32), qw=("rope", WIN_HEAD_DIM ** -0.5, BF16), kw=("rope", 1.0, BF16),
                 vw=("plain", 1.0, BF16), qd=("rope", DIFF_QK_DIM ** -0.5, BF16), kd=("rope", 1.0, BF16),
                 vd=("plain", 1.0, BF16), gates=("sigmoid", 1.0, BF16))
    seg = {}
    for name, (mode, scale, dtype) in modes.items():
        w_seg = p["w_in"][i][:, src[name]:src[name] + widths[name]].astype(BF16)
        seg[name] = in_proj(h1, w_seg, 0, widths[name], mode, scale, dtype, tables, rows_lat, seq)

    mats = s5_matrices(p["ssm_lambda_re"][i], p["ssm_lambda_im"][i], p["ssm_log_dt"][i], p["ssm_b_re"][i],
                       p["ssm_b_im"][i], p["ssm_c_re"][i], p["ssm_c_im"][i], p["ssm_d"][i], S5_CHUNK)
    y = s5_mixer(seg["u"], mats, rows_all, n_batch, seq, n_ctx)
    o_ssm = s5_glu(y, p["w_glu"], p["b_glu"], i, n_out)

    sink = p["win_sink"][i].astype(F32)
    o_win = window_attention(sink, seg["qw"], seg["kw"], seg["vw"], n_batch, seq, n_ctx, n_wh, rows_lat, False)

    lam_p = p["diff_lambda"][i].astype(F32)
    lambda_init = 0.8 - 0.6 * math.exp(-0.3 * i)
    lam = (jnp.exp(jnp.sum(lam_p[0] * lam_p[1])) - jnp.exp(jnp.sum(lam_p[2] * lam_p[3])) + lambda_init).reshape(1)
    gain = p["diff_subln"][i].astype(F32).reshape(1, DIFF_V_DIM)
    o_diff = diff_attention(lam, gain, seg["qd"], seg["kd"], seg["vd"], n_batch, seq, n_ctx, n_dh, rows_lat, False,
                            1.0 - lambda_init)
    if with_ctx:
        o_win_c = window_attention(sink, seg["qw"], seg["kw"], seg["vw"], n_batch, seq, n_ctx, n_wh, rows_lat, True)
        o_diff_c = diff_attention(lam, gain, seg["qd"], seg["kd"], seg["vd"], n_batch, seq, n_ctx, n_dh, rows_lat, True,
                                  1.0 - lambda_init)
        o_win = jnp.concatenate([o_win, o_win_c], axis=0)
        o_diff = jnp.concatenate([o_diff, o_diff_c], axis=0)

    merged = gated_merge(o_ssm, o_win, o_diff, p["w_branch_ssm"], p["w_branch_win"], p["w_branch_diff"], seg["gates"],
                         i, n_out, d)
    x = out_proj_residual(merged, p["w_out"], x, g1, i, n_out, rows_lat, seq)

    hp, logits = norm_mod(x, p["g_ffn"][i], sc2, sh2, n_out, rows_lat, seq, BF16,
                          router=(p["w_router"][i], p["b_router"][i]))
    return moe_residual(x, hp, logits, g2, p["w_exp1"], p["b_exp1"], p["w_exp2"], p["b_exp2"], i, rows_lat, seq)


def kernel(x, c, ctx, c_ctx, w_mod, b_mod, g_mix, g_ffn, w_in, ssm_lambda_re, ssm_lambda_im, ssm_log_dt, ssm_b_re, ssm_b_im, ssm_c_re, ssm_c_im, ssm_d, w_glu, b_glu, win_sink, diff_lambda, diff_subln, w_branch_ssm, w_branch_win, w_branch_diff, w_out, w_router, b_router, w_exp1, b_exp1, w_exp2, b_exp2, g_final):
    n_batch, seq, d = x.shape
    n_ctx = ctx.shape[1]
    depth = w_mod.shape[0]
    p = dict(w_mod=w_mod, b_mod=b_mod, g_mix=g_mix, g_ffn=g_ffn, w_in=w_in, ssm_lambda_re=ssm_lambda_re,
             ssm_lambda_im=ssm_lambda_im, ssm_log_dt=ssm_log_dt, ssm_b_re=ssm_b_re, ssm_b_im=ssm_b_im,
             ssm_c_re=ssm_c_re, ssm_c_im=ssm_c_im, ssm_d=ssm_d, w_glu=w_glu, b_glu=b_glu, win_sink=win_sink,
             diff_lambda=diff_lambda, diff_subln=diff_subln, w_branch_ssm=w_branch_ssm, w_branch_win=w_branch_win,
             w_branch_diff=w_branch_diff, w_out=w_out, w_router=w_router, b_router=b_router, w_exp1=w_exp1,
             b_exp1=b_exp1, w_exp2=w_exp2, b_exp2=b_exp2)
    rows_lat = n_batch * seq
    rows = jnp.concatenate([x.reshape(rows_lat, d), ctx.reshape(n_batch * n_ctx, d)], axis=0)
    c_all = jnp.concatenate([c, c_ctx[None, :], jnp.zeros((8 - (n_batch + 1) % 8, d), F32)], axis=0)
    tables = rope_tables(seq, WIN_HEAD_DIM)
    for i in range(depth):
        with_ctx = i < depth - 1
        rows = _layer(i, rows, c_all, p, n_batch, seq, n_ctx, tables, with_ctx)
    no_mod = jnp.zeros((n_batch + 1, 1, d), F32)
    out = norm_mod(rows, g_final, no_mod, no_mod, rows_lat, rows_lat, seq, F32)
    return out.reshape(n_batch, seq, d)
```

```python
import functools
import math

import jax
import jax.numpy as jnp
import numpy as np
from jax import lax
from jax.experimental import pallas as pl
from jax.experimental.pallas import tpu as pltpu

F32 = jnp.float32
BF16 = jnp.bfloat16

GRID_W = 64
SSM_GROUP = 16
WIN_KV_HEADS = 2
WIN_HEAD_DIM = 64
WINDOW = 128
DIFF_QK_DIM = 64
DIFF_V_DIM = 2 * DIFF_QK_DIM
ATTN_BLOCK = 128
ROPE_BASE = 10000.0
TOP_K = 4
SWIGLU_LIMIT = 7.0
SWIGLU_ALPHA = 1.702
NORM_EPS = 1e-6
SUBLN_EPS = 1e-5
NEG_INF = -1e30

LANE = 128
VMEM_LIMIT_BYTES = 56 * 1024 * 1024
S5_CHUNK = 32
MOE_LOAD_MARGIN = 1.4
MOE_SUB_ROWS = 256
ROW_TILE = 512


def _cparams(*sem):
    return pltpu.CompilerParams(dimension_semantics=sem, vmem_limit_bytes=VMEM_LIMIT_BYTES)


def _dot(a, b):
    return jnp.dot(a, b, preferred_element_type=F32)


def _dot_nt(a, b):
    return lax.dot_general(a, b, (((1,), (1,)), ((), ())), preferred_element_type=F32)


def _sigmoid(x):
    return 1.0 / (1.0 + jnp.exp(-x))


def _row_tile(n_rows, cap):
    best = 16
    for t in range(16, cap + 1, 16):
        if n_rows % t == 0:
            best = t
    return best


def _norm_mod_kernel(x_ref, g_ref, sc_ref, sh_ref, o_ref):
    x = x_ref[...]
    y = x * lax.rsqrt(jnp.mean(x * x, axis=-1, keepdims=True) + NORM_EPS) * g_ref[...]
    o_ref[...] = (y * (1.0 + sc_ref[0]) + sh_ref[0]).astype(o_ref.dtype)


def _norm_router_kernel(x_ref, g_ref, sc_ref, sh_ref, wr_ref, br_ref, o_ref, lg_ref):
    x = x_ref[...]
    y = x * lax.rsqrt(jnp.mean(x * x, axis=-1, keepdims=True) + NORM_EPS) * g_ref[...]
    h = y * (1.0 + sc_ref[0]) + sh_ref[0]
    o_ref[...] = _pack_halves(h)
    lg_ref[...] = jnp.dot(h, wr_ref[...], preferred_element_type=F32, precision=lax.Precision.HIGHEST) + br_ref[...]


def _stream_tile(n_rows, rows_lat, seq):
    tm = math.gcd(ROW_TILE, seq)
    return math.gcd(tm, n_rows - rows_lat) if n_rows > rows_lat else tm


def _stream_index(i, tm, rows_lat, seq, n_batch):
    r = i * tm
    return jnp.where(r >= rows_lat, n_batch, r // seq)


def norm_mod(x, gain, sc, sh, n_rows, rows_lat, seq, out_dtype, router=None):
    d = x.shape[1]
    tm = _stream_tile(n_rows, rows_lat, seq)
    n_batch = sc.shape[0] - 1
    idx = functools.partial(_stream_index, tm=tm, rows_lat=rows_lat, seq=seq, n_batch=n_batch)
    in_specs = [
        pl.BlockSpec((tm, d), lambda i: (i, 0)),
        pl.BlockSpec((1, d), lambda i: (0, 0)),
        pl.BlockSpec((1, 1, d), lambda i: (idx(i), 0, 0)),
        pl.BlockSpec((1, 1, d), lambda i: (idx(i), 0, 0)),
    ]
    if router is None:
        return pl.pallas_call(
            _norm_mod_kernel,
            grid=(n_rows // tm,),
            in_specs=in_specs,
            out_specs=pl.BlockSpec((tm, d), lambda i: (i, 0)),
            out_shape=jax.ShapeDtypeStruct((n_rows, d), out_dtype),
            compiler_params=_cparams("parallel"),
            name="norm_mod",
        )(x, gain.reshape(1, d), sc, sh)
    w_router, b_router = router
    n_exp = w_router.shape[1]
    return pl.pallas_call(
        _norm_router_kernel,
        grid=(n_rows // tm,),
        in_specs=in_specs + [
            pl.BlockSpec((d, n_exp), lambda i: (0, 0)),
            pl.BlockSpec((1, n_exp), lambda i: (0, 0)),
        ],
        out_specs=[pl.BlockSpec((tm, d // 2), lambda i: (i, 0)), pl.BlockSpec((tm, n_exp), lambda i: (i, 0))],
        out_shape=[jax.ShapeDtypeStruct((n_rows, d // 2), jnp.uint32), jax.ShapeDtypeStruct((n_rows, n_exp), F32)],
        compiler_params=_cparams("parallel"),
        name="norm_router",
    )(x, gain.reshape(1, d), sc, sh, w_router, b_router.reshape(1, n_exp))


def _mm_kernel(x_ref, w_ref, o_ref, *, pre):
    x = x_ref[...]
    if pre == "silu":
        x = x * _sigmoid(x)
    o_ref[...] = _dot(x.astype(BF16), w_ref[...].astype(BF16)).astype(o_ref.dtype)


def matmul(x, w_stack, layer, n_rows, tm, tn, out_dtype, pre=None, name="matmul"):
    k = x.shape[1]
    n = w_stack.shape[2]
    return pl.pallas_call(
        functools.partial(_mm_kernel, pre=pre),
        grid=(n_rows // tm, n // tn),
        in_specs=[
            pl.BlockSpec((tm, k), lambda i, j: (i, 0)),
            pl.BlockSpec((None, k, tn), lambda i, j: (layer, 0, j)),
        ],
        out_specs=pl.BlockSpec((tm, tn), lambda i, j: (i, j)),
        out_shape=jax.ShapeDtypeStruct((n_rows, n), out_dtype),
        compiler_params=_cparams("parallel", "arbitrary"),
        name=name,
    )(x, w_stack)


def _in_proj_kernel(x_ref, w_ref, *refs, mode, scale):
    o_ref = refs[-1]
    acc = _dot(x_ref[...], w_ref[...])
    if mode == "sigmoid":
        o_ref[...] = _sigmoid(acc).astype(o_ref.dtype)
        return
    if mode == "plain":
        o_ref[...] = (acc if scale == 1.0 else acc * scale).astype(o_ref.dtype)
        return
    cos_ref, sin_ref = refs[0], refs[1]
    lane = lax.broadcasted_iota(jnp.int32, (1, LANE), 1)
    first_half = (lane % 32) < 16
    cos = cos_ref[...] if scale == 1.0 else cos_ref[...] * scale
    sin = sin_ref[...] if scale == 1.0 else sin_ref[...] * scale
    for c in range(acc.shape[1] // LANE):
        x = acc[:, c * LANE:(c + 1) * LANE]
        partner = jnp.where(first_half, pltpu.roll(x, LANE - 16, axis=1), pltpu.roll(x, 16, axis=1))
        o_ref[:, c * LANE:(c + 1) * LANE] = (x * cos + partner * sin).astype(o_ref.dtype)


def in_proj(h, w, col0, width, mode, scale, out_dtype, tables, rows_lat, seq):
    n_rows, k = h.shape
    tm = _stream_tile(n_rows, rows_lat, seq)
    tn = math.gcd(1024, width)
    assert col0 % tn == 0
    in_specs = [pl.BlockSpec((tm, k), lambda i, j: (i, 0)),
                pl.BlockSpec((k, tn), lambda i, j: (0, col0 // tn + j))]
    args = [h, w]
    if mode == "rope":
        cos_t, sin_t = tables
        tiles_per_seq = seq // tm
        lat_tiles = rows_lat // tm
        cos_x = jnp.concatenate([cos_t, jnp.ones((tm, LANE), F32)], axis=0)
        sin_x = jnp.concatenate([sin_t, jnp.zeros((tm, LANE), F32)], axis=0)
        tab = pl.BlockSpec((tm, LANE), lambda i, j: (jnp.where(i < lat_tiles, i % tiles_per_seq, tiles_per_seq), 0))
        in_specs += [tab, tab]
        args += [cos_x, sin_x]
    return pl.pallas_call(
        functools.partial(_in_proj_kernel, mode=mode, scale=scale),
        grid=(n_rows // tm, width // tn),
        in_specs=in_specs,
        out_specs=pl.BlockSpec((tm, tn), lambda i, j: (i, j)),
        out_shape=jax.ShapeDtypeStruct((n_rows, width), out_dtype),
        compiler_params=_cparams("parallel", "arbitrary"),
        name="in_proj_" + mode,
    )(*args)


def rope_tables(seq, head_dim):
    assert head_dim == 64
    rows = seq // GRID_W
    r, col = jnp.meshgrid(jnp.arange(rows, dtype=F32), jnp.arange(GRID_W, dtype=F32), indexing="ij")
    half = head_dim // 2
    inv_freq = ROPE_BASE ** (-jnp.arange(0, half, 2, dtype=F32) / half)
    ang_r = r.reshape(-1)[:, None] * inv_freq[None, :]
    ang_c = col.reshape(-1)[:, None] * inv_freq[None, :]
    cos = jnp.concatenate([jnp.cos(ang_r), jnp.cos(ang_r), jnp.cos(ang_c), jnp.cos(ang_c)], axis=-1)
    sin = jnp.concatenate([-jnp.sin(ang_r), jnp.sin(ang_r), -jnp.sin(ang_c), jnp.sin(ang_c)], axis=-1)
    return jnp.tile(cos, (1, LANE // head_dim)), jnp.tile(sin, (1, LANE // head_dim))


def _win_attn_kernel(sink_ref, q_ref, *refs, n_heads, band, seq):
    if band:
        kp_ref, kc_ref, kn_ref, vp_ref, vc_ref, vn_ref, kx_ref, vx_ref, o_ref = refs
    else:
        kx_ref, vx_ref, o_ref = refs
    blk = q_ref.shape[0]
    dh = WIN_HEAD_DIM
    grp = n_heads // WIN_KV_HEADS
    lane = lax.broadcasted_iota(jnp.int32, (1, LANE), 1)
    rows = grp * blk
    if band:
        n = pl.program_id(1)
        k_all = jnp.concatenate([kp_ref[...], kc_ref[...], kn_ref[...], kx_ref[...]], axis=0)
        v_all = jnp.concatenate([vp_ref[...], vc_ref[...], vn_ref[...], vx_ref[...]], axis=0)
        n_keys = k_all.shape[0]
        col = lax.broadcasted_iota(jnp.int32, (rows, n_keys), 1)
        qpos = n * blk + (lax.broadcasted_iota(jnp.int32, (rows, n_keys), 0) & (blk - 1))
        kpos = (n - 1) * blk + col
        mask = (col >= 3 * blk) | ((jnp.abs(qpos - kpos) <= WINDOW) & (kpos >= 0) & (kpos < seq))
    else:
        k_all, v_all = kx_ref[...], vx_ref[...]
        n_keys = k_all.shape[0]
    ones_col = (lax.broadcasted_iota(jnp.int32, (n_keys, LANE), 1) == 0).astype(BF16)
    v_aug = jnp.concatenate([v_all, ones_col], axis=1)
    outs = [None] * n_heads
    for kvh in range(WIN_KV_HEADS):
        keep = (lane >= kvh * dh) & (lane < (kvh + 1) * dh)
        q_parts, sink_parts = [], []
        for g in range(grp):
            h = kvh * grp + g
            c = (h * dh) // LANE
            qc = q_ref[:, c * LANE:(c + 1) * LANE]
            if (h * dh) % LANE != kvh * dh:
                qc = jnp.concatenate([qc[:, dh:], qc[:, :dh]], axis=1)
            q_parts.append(jnp.where(keep, qc, jnp.zeros_like(qc)))
            sink_parts.append(jnp.full((blk, 1), sink_ref[h], F32))
        qs = jnp.concatenate(q_parts, axis=0)
        sink = jnp.concatenate(sink_parts, axis=0)
        s = _dot_nt(qs, k_all)
        if band:
            s = jnp.where(mask, s, NEG_INF)
        m = jnp.maximum(jnp.max(s, axis=-1, keepdims=True), sink)
        acc = _dot(jnp.exp(s - m).astype(BF16), v_aug)
        inv = 1.0 / (acc[:, LANE:LANE + 1] + jnp.exp(sink - m))
        o = acc[:, kvh * dh:(kvh + 1) * dh] * inv
        for g in range(grp):
            outs[kvh * grp + g] = o[g * blk:(g + 1) * blk, :]
    o_ref[...] = jnp.concatenate(outs, axis=1).astype(o_ref.dtype)


def window_attention(sink, q, k, v, n_batch, seq, n_ctx, n_heads, rows_lat, ctx_queries):
    blk = ATTN_BLOCK
    seq_q = n_ctx if ctx_queries else seq
    nb = seq_q // blk
    q0 = rows_lat // blk if ctx_queries else 0
    x0 = rows_lat // n_ctx
    qw = n_heads * WIN_HEAD_DIM
    kvw = WIN_KV_HEADS * WIN_HEAD_DIM
    in_specs = [
        pl.BlockSpec(memory_space=pltpu.SMEM),
        pl.BlockSpec((blk, qw), lambda b, n: (q0 + b * nb + n, 0)),
    ]
    args = [sink, q]
    if not ctx_queries:
        band_specs = [
            pl.BlockSpec((blk, kvw), lambda b, n: (b * nb + jnp.maximum(n - 1, 0), 0)),
            pl.BlockSpec((blk, kvw), lambda b, n: (b * nb + n, 0)),
            pl.BlockSpec((blk, kvw), lambda b, n: (b * nb + jnp.minimum(n + 1, nb - 1), 0)),
        ]
        in_specs += band_specs + band_specs
        args += [k, k, k, v, v, v]
    in_specs += [pl.BlockSpec((n_ctx, kvw), lambda b, n: (x0 + b, 0))] * 2
    args += [k, v]
    return pl.pallas_call(
        functools.partial(_win_attn_kernel, n_heads=n_heads, band=not ctx_queries, seq=seq),
        grid=(n_batch, nb),
        in_specs=in_specs,
        out_specs=pl.BlockSpec((blk, qw), lambda b, n: (b * nb + n, 0)),
        out_shape=jax.ShapeDtypeStruct((n_batch * seq_q, qw), BF16),
        compiler_params=_cparams("parallel", "parallel"),
        name="context_window_attention" if ctx_queries else "window_attention",
    )(*args)


def _diff_attn_kernel(lam_ref, q_ref, gain_ref, *refs, with_lat, post_scale):
    if with_lat:
        kl_ref, vl_ref, kx_ref, vx_ref, o_ref = refs
    else:
        kx_ref, vx_ref, o_ref = refs
    lam = lam_ref[0]
    q = q_ref[...]
    lane = lax.broadcasted_iota(jnp.int32, (1, LANE), 1)
    zero = jnp.zeros_like(q)
    kx = kx_ref[...]

    def softmax_parts(qm):
        s_x = _dot_nt(qm, kx)
        m = jnp.max(s_x, axis=-1, keepdims=True)
        if with_lat:
            s_l = _dot_nt(qm, kl_ref[...])
            m = jnp.maximum(m, jnp.max(s_l, axis=-1, keepdims=True))
            e_l = jnp.exp(s_l - m)
        else:
            e_l = None
        e_x = jnp.exp(s_x - m)
        den = jnp.sum(e_x, axis=-1, keepdims=True)
        if with_lat:
            den = den + jnp.sum(e_l, axis=-1, keepdims=True)
        return e_l, e_x, 1.0 / den

    e1l, e1x, inv1 = softmax_parts(jnp.where(lane < DIFF_QK_DIM, q, zero))
    e2l, e2x, inv2 = softmax_parts(jnp.where(lane >= DIFF_QK_DIM, q, zero))
    w2 = lam * inv2
    o = _dot((e1x * inv1 - e2x * w2).astype(BF16), vx_ref[...])
    if with_lat:
        o = o + _dot((e1l * inv1 - e2l * w2).astype(BF16), vl_ref[...])
    o = o * lax.rsqrt(jnp.mean(o * o, axis=-1, keepdims=True) + SUBLN_EPS) * gain_ref[...]
    o_ref[...] = (o * post_scale).astype(o_ref.dtype)


def diff_attention(lam, gain, q, k, v, n_batch, seq, n_ctx, n_heads, rows_lat, ctx_queries, post_scale):
    seq_q = n_ctx if ctx_queries else seq
    tq = min(256, seq_q)
    nq = seq_q // tq
    q0 = rows_lat // tq if ctx_queries else 0
    x0 = rows_lat // n_ctx
    in_specs = [
        pl.BlockSpec(memory_space=pltpu.SMEM),
        pl.BlockSpec((tq, LANE), lambda b, h, i: (q0 + b * nq + i, h)),
        pl.BlockSpec((1, LANE), lambda b, h, i: (0, 0)),
    ]
    args = [lam, q, gain]
    if not ctx_queries:
        in_specs += [pl.BlockSpec((seq, LANE), lambda b, h, i: (b, h))] * 2
        args += [k, v]
    in_specs += [pl.BlockSpec((n_ctx, LANE), lambda b, h, i: (x0 + b, h))] * 2
    args += [k, v]
    return pl.pallas_call(
        functools.partial(_diff_attn_kernel, with_lat=not ctx_queries, post_scale=post_scale),
        grid=(n_batch, n_heads, nq),
        in_specs=in_specs,
        out_specs=pl.BlockSpec((tq, LANE), lambda b, h, i: (b * nq + i, h)),
        out_shape=jax.ShapeDtypeStruct((n_batch * seq_q, n_heads * LANE), BF16),
        compiler_params=_cparams("parallel", "parallel", "arbitrary"),
        name="context_diff_attention" if ctx_queries else "diff_attention",
    )(*args)


def s5_matrices(lam_re, lam_im, log_dt, b_re, b_im, c_re, c_im, d_skip, t_chunk):
    n_dir, n_g, n_p = lam_re.shape
    n_h = b_re.shape[-1]
    lam_re = jnp.minimum(lam_re.astype(F32), -1e-4)
    lam_im = lam_im.astype(F32)
    dt = jnp.exp(log_dt.astype(F32))[..., None]
    mag = jnp.exp(lam_re * dt)
    a_re = mag * jnp.cos(lam_im * dt)
    a_im = mag * jnp.sin(lam_im * dt)
    den = lam_re * lam_re + lam_im * lam_im
    k_re = ((a_re - 1.0) * lam_re + a_im * lam_im) / den
    k_im = (a_im * lam_re - (a_re - 1.0) * lam_im) / den
    b_re = b_re.astype(F32)
    b_im = b_im.astype(F32)
    bb_re = k_re[..., None] * b_re - k_im[..., None] * b_im
    bb_im = k_re[..., None] * b_im + k_im[..., None] * b_re
    c_re = c_re.astype(F32)
    c_im = c_im.astype(F32)
    ldt_re = lam_re * dt
    ldt_im = lam_im * dt
    steps = jnp.arange(t_chunk, dtype=F32)

    def power(d, t, p_last):
        lr = ldt_re[d][:, None, :] if p_last else ldt_re[d][:, :, None]
        li = ldt_im[d][:, None, :] if p_last else ldt_im[d][:, :, None]
        tt = t[None, :, None] if p_last else t[None, None, :]
        mag_t = jnp.exp(lr * tt)
        return mag_t * jnp.cos(li * tt), mag_t * jnp.sin(li * tt)

    def c_times_power(d, t):
        pr, pi = power(d, t, False)
        cr = jnp.swapaxes(c_re[d], 1, 2)[:, :, None, :]
        ci = jnp.swapaxes(c_im[d], 1, 2)[:, :, None, :]
        re = cr * pr[..., None] - ci * pi[..., None]
        im = cr * pi[..., None] + ci * pr[..., None]
        return jnp.concatenate([re, -im], axis=1).reshape(n_g, 2 * n_p, t_chunk * n_h)

    def b_times_power(d, t):
        pr, pi = power(d, t, True)
        br = jnp.swapaxes(bb_re[d], 1, 2)[:, None, :, :]
        bi = jnp.swapaxes(bb_im[d], 1, 2)[:, None, :, :]
        re = pr[:, :, None, :] * br - pi[:, :, None, :] * bi
        im = pr[:, :, None, :] * bi + pi[:, :, None, :] * br
        return jnp.concatenate([re, im], axis=-1).reshape(n_g, t_chunk * n_h, 2 * n_p)

    ca_lag = jnp.stack([c_times_power(0, steps), c_times_power(1, t_chunk - 1 - steps)])
    bbt = jnp.stack([jnp.concatenate([jnp.swapaxes(bb_re[d], 1, 2), jnp.swapaxes(bb_im[d], 1, 2)], axis=-1)
                     for d in range(2)])
    p_mat = jnp.concatenate([b_times_power(0, t_chunk - 1 - steps), b_times_power(1, steps)], axis=-1)
    q_mat = jnp.concatenate([c_times_power(0, steps + 1.0), c_times_power(1, t_chunk - steps)], axis=1)
    d_tile = jnp.tile(d_skip.astype(F32).reshape(n_g, 1, n_h), (1, 1, t_chunk))
    mag_t = jnp.exp(ldt_re * t_chunk)
    at_re, at_im = mag_t * jnp.cos(ldt_im * t_chunk), mag_t * jnp.sin(ldt_im * t_chunk)
    at_mul = jnp.concatenate([at_re, at_re], axis=-1)
    at_swp = jnp.concatenate([-at_im, at_im], axis=-1)
    return ca_lag, bbt, p_mat.astype(BF16), q_mat.astype(BF16), d_tile, at_mul, at_swp


def _s5_local_kernel(x_ref, p_ref, e_ref):
    e_ref[...] = _dot(x_ref[...].astype(BF16), p_ref[...])


def _s5_out_kernel(x_ref, s_ref, ca_ref, bbt_ref, q_ref, d_ref, y_ref, m_ref):
    n_h = bbt_ref.shape[1]
    th = m_ref.shape[0]
    t_chunk = th // n_h
    hi = lax.Precision.HIGHEST
    row_f = jnp.dot(bbt_ref[0], ca_ref[0], preferred_element_type=F32, precision=hi)
    row_b = jnp.dot(bbt_ref[1], ca_ref[1], preferred_element_type=F32, precision=hi)
    lane = lax.broadcasted_iota(jnp.int32, (n_h, th), 1)
    for j in range(t_chunk):
        fwd = jnp.where(lane >= n_h * j, pltpu.roll(row_f, n_h * j, axis=1), 0.0) if j else row_f
        back = n_h * (t_chunk - 1 - j)
        bwd = jnp.where(lane < n_h * (j + 1), pltpu.roll(row_b, th - back, axis=1), 0.0) if back else row_b
        m_ref[n_h * j:n_h * (j + 1), :] = (fwd + bwd).astype(m_ref.dtype)
    x = x_ref[...]
    y_ref[...] = _dot(x.astype(BF16), m_ref[...]) + _dot(s_ref[...].astype(BF16), q_ref[...]) + x * d_ref[...]


def _s5_scan_kernel(ef_ref, eb_ref, mul_ref, swp_ref, sf_ref, sb_ref, *, n_batch, lat_chunks, ctx_chunks):
    half = ef_ref.shape[2] // 2
    mul_f, mul_b = mul_ref[0], mul_ref[1]
    swp_f, swp_b = swp_ref[0], swp_ref[1]
    zero = jnp.zeros(ef_ref.shape[1:], F32)

    def step(c_f, c_b, carry):
        s_f, s_b = carry
        sf_ref[c_f] = s_f
        sb_ref[c_b] = s_b
        s_f = s_f * mul_f + pltpu.roll(s_f, half, axis=1) * swp_f + ef_ref[c_f]
        s_b = s_b * mul_b + pltpu.roll(s_b, half, axis=1) * swp_b + eb_ref[c_b]
        return s_f, s_b

    for b in range(n_batch):
        ctx0 = n_batch * lat_chunks + b * ctx_chunks
        lat0 = b * lat_chunks
        carry = lax.fori_loop(0, ctx_chunks, lambda i, cr: step(ctx0 + i, ctx0 + ctx_chunks - 1 - i, cr), (zero, zero))
        lax.fori_loop(0, lat_chunks, lambda i, cr: step(lat0 + i, lat0 + lat_chunks - 1 - i, cr), carry)


def s5_mixer(proj, mats, n_rows, n_batch, seq, n_ctx):
    ca_lag, bbt, p_mat, q_mat, d_tile, at_mul, at_swp = mats
    n_g, th, _ = p_mat.shape
    n_h = SSM_GROUP
    t_chunk = th // n_h
    p4 = p_mat.shape[2]
    n_p2 = p4 // 2
    width = n_g * n_h
    n_chunks = n_rows // t_chunk
    u = proj[:, :width]
    x = u.reshape(n_chunks, t_chunk, n_g, n_h).transpose(2, 0, 1, 3).reshape(n_g, n_chunks, th)
    e = pl.pallas_call(
        _s5_local_kernel,
        grid=(n_g,),
        in_specs=[pl.BlockSpec((None, n_chunks, th), lambda g: (g, 0, 0)),
                  pl.BlockSpec((None, th, p4), lambda g: (g, 0, 0))],
        out_specs=pl.BlockSpec((n_chunks, p4), lambda g: (0, g)),
        out_shape=jax.ShapeDtypeStruct((n_chunks, n_g * p4), F32),
        compiler_params=_cparams("parallel"),
        name="s5_local_state",
    )(x, p_mat)
    e = e.reshape(n_chunks, n_g, 2, n_p2)
    gb = 16
    spec = pl.BlockSpec((n_chunks, gb, n_p2), lambda g: (0, g, 0))
    tab = pl.BlockSpec((2, gb, n_p2), lambda g: (0, g, 0))
    s_f, s_b = pl.pallas_call(
        functools.partial(_s5_scan_kernel, n_batch=n_batch, lat_chunks=seq // t_chunk, ctx_chunks=n_ctx // t_chunk),
        grid=(n_g // gb,),
        in_specs=[spec, spec, tab, tab],
        out_specs=[spec, spec],
        out_shape=[jax.ShapeDtypeStruct((n_chunks, n_g, n_p2), F32)] * 2,
        compiler_params=_cparams("parallel"),
        name="s5_chunk_scan",
    )(e[:, :, 0], e[:, :, 1], at_mul, at_swp)
    s_in = jnp.stack([s_f, s_b], axis=2).reshape(n_chunks, n_g * p4)
    y = pl.pallas_call(
        _s5_out_kernel,
        grid=(n_g,),
        in_specs=[pl.BlockSpec((None, n_chunks, th), lambda g: (g, 0, 0)),
                  pl.BlockSpec((n_chunks, p4), lambda g: (0, g)),
                  pl.BlockSpec((2, None, p4 // 2, th), lambda g: (0, g, 0, 0)),
                  pl.BlockSpec((2, None, n_h, p4 // 2), lambda g: (0, g, 0, 0)),
                  pl.BlockSpec((None, p4, th), lambda g: (g, 0, 0)),
                  pl.BlockSpec((None, 1, th), lambda g: (g, 0, 0))],
        out_specs=pl.BlockSpec((None, n_chunks, th), lambda g: (g, 0, 0)),
        out_shape=jax.ShapeDtypeStruct((n_g, n_chunks, th), F32),
        scratch_shapes=[pltpu.VMEM((th, th), BF16)],
        compiler_params=_cparams("parallel"),
        name="s5_output",
    )(x, s_in, ca_lag, bbt, q_mat, d_tile)
    return y.reshape(n_g, n_chunks, t_chunk, n_h).transpose(1, 2, 0, 3).reshape(n_rows, width)


def _glu_kernel(y_ref, w_ref, b_ref, o_ref):
    y = y_ref[...]
    gy = 0.5 * y * (1.0 + jnp.tanh(math.sqrt(2.0 / math.pi) * (y + 0.044715 * (y * y * y))))
    z = _dot(gy.astype(BF16), w_ref[...].astype(BF16)) + b_ref[...]
    o_ref[...] = (gy * _sigmoid(z)).astype(o_ref.dtype)


def s5_glu(y, w_glu, b_glu, layer, n_rows):
    width = y.shape[1]
    tm = _row_tile(n_rows, ROW_TILE)
    return pl.pallas_call(
        _glu_kernel,
        grid=(n_rows // tm,),
        in_specs=[pl.BlockSpec((tm, width), lambda i: (i, 0)),
                  pl.BlockSpec((None, width, width), lambda i: (layer, 0, 0)),
                  pl.BlockSpec((None, 1, width), lambda i: (layer, 0, 0))],
        out_specs=pl.BlockSpec((tm, width), lambda i: (i, 0)),
        out_shape=jax.ShapeDtypeStruct((n_rows, width), BF16),
        compiler_params=_cparams("parallel"),
        name="s5_glu",
    )(y, w_glu, b_glu.reshape(b_glu.shape[0], 1, width))


def _merge_kernel(os_ref, ow_ref, od_ref, ws_ref, ww_ref, wd_ref, gs_ref, gw_ref, gd_ref, o_ref):
    m = (gs_ref[...].astype(F32) * _dot(os_ref[...], ws_ref[...].astype(BF16))
         + gw_ref[...].astype(F32) * _dot(ow_ref[...], ww_ref[...].astype(BF16))
         + gd_ref[...].astype(F32) * _dot(od_ref[...], wd_ref[...].astype(BF16)))
    o_ref[...] = m.astype(o_ref.dtype)


def gated_merge(o_ssm, o_win, o_diff, w_s, w_w, w_d, proj, layer, n_rows, d):
    tm = _row_tile(n_rows, 1088)
    tn = 256
    assert d % tn == 0
    g0 = 0
    nd = d // tn

    def branch(arr):
        return pl.BlockSpec((tm, arr.shape[1]), lambda i, j: (i, 0))

    def weight(w):
        return pl.BlockSpec((None, w.shape[1], tn), lambda i, j: (layer, 0, j))

    def gate(k):
        return pl.BlockSpec((tm, tn), lambda i, j: (i, g0 + k * nd + j))

    return pl.pallas_call(
        _merge_kernel,
        grid=(n_rows // tm, nd),
        in_specs=[branch(o_ssm), branch(o_win), branch(o_diff), weight(w_s), weight(w_w), weight(w_d),
                  gate(0), gate(1), gate(2)],
        out_specs=pl.BlockSpec((tm, tn), lambda i, j: (i, j)),
        out_shape=jax.ShapeDtypeStruct((n_rows, d), BF16),
        compiler_params=_cparams("parallel", "arbitrary"),
        name="gated_merge",
    )(o_ssm, o_win, o_diff, w_s, w_w, w_d, proj, proj, proj)


def _out_proj_kernel(m_ref, w_ref, x_ref, g_ref, o_ref):
    o_ref[...] = x_ref[...] + g_ref[0] * _dot(m_ref[...], w_ref[...].astype(BF16))


def out_proj_residual(m, w_out, x, gate, layer, n_rows, rows_lat, seq):
    d = x.shape[1]
    tm = _stream_tile(n_rows, rows_lat, seq)
    tn = min(1024, d)
    n_batch = gate.shape[0] - 1
    idx = functools.partial(_stream_index, tm=tm, rows_lat=rows_lat, seq=seq, n_batch=n_batch)
    return pl.pallas_call(
        _out_proj_kernel,
        grid=(n_rows // tm, d // tn),
        in_specs=[pl.BlockSpec((tm, m.shape[1]), lambda i, j: (i, 0)),
                  pl.BlockSpec((None, m.shape[1], tn), lambda i, j: (layer, 0, j)),
                  pl.BlockSpec((tm, tn), lambda i, j: (i, j)),
                  pl.BlockSpec((1, 1, tn), lambda i, j: (idx(i), 0, j))],
        out_specs=pl.BlockSpec((tm, tn), lambda i, j: (i, j)),
        out_shape=jax.ShapeDtypeStruct((n_rows, d), F32),
        compiler_params=_cparams("parallel", "arbitrary"),
        name="out_proj_residual",
    )(m, w_out, x, gate)


def _pack_halves(h):
    half = h.shape[1] // 2
    lo = lax.bitcast_convert_type(h[:, :half].astype(BF16).astype(F32), jnp.uint32)
    hi = lax.bitcast_convert_type(h[:, half:].astype(BF16).astype(F32), jnp.uint32)
    return (lo >> 16) | (hi & jnp.uint32(0xFFFF0000))


def _unpack_halves(x):
    lo = lax.bitcast_convert_type(x << 16, F32).astype(BF16)
    hi = lax.bitcast_convert_type(x & jnp.uint32(0xFFFF0000), F32).astype(BF16)
    return lo, hi


def _dispatch_kernel(pos_ref, pad_lo_ref, pad_hi_ref, h_ref, xb_ref, zero_ref, sem):
    tb = h_ref.shape[0]
    base = pl.program_id(0) * (tb * TOP_K)

    @pl.when(pl.program_id(0) == 0)
    def _():
        zero_ref[...] = jnp.zeros_like(zero_ref)

        def fill(row, carry):
            pltpu.make_async_copy(zero_ref.at[pl.ds(0, 1)], xb_ref.at[pl.ds(row, 1)], sem).start()
            return carry

        def fill_done(row, carry):
            pltpu.make_async_copy(zero_ref.at[pl.ds(0, 1)], xb_ref.at[pl.ds(row, 1)], sem).wait()
            return carry

        def per_expert(body):
            lax.fori_loop(0, pad_lo_ref.shape[0], lambda e, c: lax.fori_loop(pad_lo_ref[e], pad_hi_ref[e], body, c), 0)

        per_expert(fill)
        per_expert(fill_done)

    def issue(j, carry):
        for k in range(TOP_K):
            dst = pos_ref[base + j * TOP_K + k]
            pltpu.make_async_copy(h_ref.at[pl.ds(j, 1)], xb_ref.at[pl.ds(dst, 1)], sem).start()
        return carry

    lax.fori_loop(0, tb, issue, 0, unroll=8)

    def drain(j, carry):
        for k in range(TOP_K):
            pltpu.make_async_copy(h_ref.at[pl.ds(j, 1)], xb_ref.at[pl.ds(0, 1)], sem).wait()
        return carry

    lax.fori_loop(0, tb, drain, 0, unroll=8)


def _combine_kernel(pos_ref, yb_ref, x_ref, g_ref, gates_ref, o_ref, buf, sem):
    tb = x_ref.shape[0]
    base = pl.program_id(0) * (tb * TOP_K)

    def issue(j, carry):
        for k in range(TOP_K):
            src = pos_ref[base + j * TOP_K + k]
            pltpu.make_async_copy(yb_ref.at[pl.ds(src, 1)], buf.at[k, pl.ds(j, 1)], sem).start()
        return carry

    lax.fori_loop(0, tb, issue, 0, unroll=8)

    def drain(j, carry):
        for k in range(TOP_K):
            pltpu.make_async_copy(yb_ref.at[pl.ds(0, 1)], buf.at[k, pl.ds(j, 1)], sem).wait()
        return carry

    lax.fori_loop(0, tb, drain, 0, unroll=8)
    gates = gates_ref[...]
    f = gates[:, 0:1] * buf[0]
    for k in range(1, TOP_K):
        f = f + gates[:, k:k + 1] * buf[k]
    o_ref[...] = x_ref[...] + g_ref[0] * f


def _expert_up_kernel(be_ref, nu_ref, nv_ref, x_ref, wg_ref, wl_ref, bg_ref, bl_ref, o_ref, wg_s, wl_s):
    r = pl.program_id(0)

    @pl.when(r < nu_ref[0])
    def _():
        wg_s[...] = wg_ref[...].astype(BF16)
        wl_s[...] = wl_ref[...].astype(BF16)
        half = x_ref.shape[1]
        for s in range(x_ref.shape[0] // MOE_SUB_ROWS):
            @pl.when(s * MOE_SUB_ROWS < nv_ref[r])
            def _(s=s):
                rows = slice(s * MOE_SUB_ROWS, (s + 1) * MOE_SUB_ROWS)
                lo, hi = _unpack_halves(x_ref[rows, :])
                glu = _dot(lo, wg_s[:half, :]) + _dot(hi, wg_s[half:, :]) + bg_ref[...]
                lin = _dot(lo, wl_s[:half, :]) + _dot(hi, wl_s[half:, :]) + bl_ref[...]
                glu = jnp.minimum(glu, SWIGLU_LIMIT)
                lin = jnp.clip(lin, -SWIGLU_LIMIT, SWIGLU_LIMIT)
                o_ref[rows, :] = (glu * _sigmoid(SWIGLU_ALPHA * glu) * (lin + 1.0)).astype(o_ref.dtype)


def _expert_down_kernel(be_ref, nu_ref, nv_ref, a_ref, w_ref, b_ref, o_ref, w_s):
    r = pl.program_id(0)

    @pl.when(r < nu_ref[0])
    def _():
        w_s[...] = w_ref[...].astype(BF16)
        for s in range(a_ref.shape[0] // MOE_SUB_ROWS):
            @pl.when(s * MOE_SUB_ROWS < nv_ref[r])
            def _(s=s):
                rows = slice(s * MOE_SUB_ROWS, (s + 1) * MOE_SUB_ROWS)
                o_ref[rows, :] = _dot(a_ref[rows, :], w_s[...]) + b_ref[...]


def expert_ffn(block_exp, n_used, n_valid, xb, w1, b1, w2, b2, layer, block_rows):
    n_rows, half = xb.shape
    d = 2 * half
    n_exp, _, f2 = w1.shape[1:]
    f = f2 // 2
    n_blocks = n_rows // block_rows
    tf = min(512, f)
    nf = f // tf
    b1r = b1.reshape(b1.shape[0], n_exp, 1, f2)
    b2r = b2.reshape(b2.shape[0], n_exp, 1, d)

    def rows(r, nu):
        return jnp.minimum(r, nu[0] - 1)

    def tile(r, j, nu, n_tiles):
        return jnp.where(r < nu[0], j, n_tiles - 1)

    act = pl.pallas_call(
        _expert_up_kernel,
        grid_spec=pltpu.PrefetchScalarGridSpec(
            num_scalar_prefetch=3,
            grid=(n_blocks, nf),
            in_specs=[
                pl.BlockSpec((block_rows, half), lambda r, j, be, nu, nv: (rows(r, nu), 0)),
                pl.BlockSpec((None, None, d, tf), lambda r, j, be, nu, nv: (layer, be[rows(r, nu)], 0, tile(r, j, nu, nf))),
                pl.BlockSpec((None, None, d, tf),
                             lambda r, j, be, nu, nv: (layer, be[rows(r, nu)], 0, nf + tile(r, j, nu, nf))),
                pl.BlockSpec((None, None, 1, tf), lambda r, j, be, nu, nv: (layer, be[rows(r, nu)], 0, tile(r, j, nu, nf))),
                pl.BlockSpec((None, None, 1, tf),
                             lambda r, j, be, nu, nv: (layer, be[rows(r, nu)], 0, nf + tile(r, j, nu, nf))),
            ],
            out_specs=pl.BlockSpec((block_rows, tf), lambda r, j, be, nu, nv: (rows(r, nu), tile(r, j, nu, nf))),
            scratch_shapes=[pltpu.VMEM((d, tf), BF16), pltpu.VMEM((d, tf), BF16)],
        ),
        out_shape=jax.ShapeDtypeStruct((n_rows, f), BF16),
        compiler_params=_cparams("arbitrary", "arbitrary"),
        name="expert_up",
    )(block_exp, n_used, n_valid, xb, w1, w1, b1r, b1r)
    tn = min(1024, d)
    nd = d // tn
    return pl.pallas_call(
        _expert_down_kernel,
        grid_spec=pltpu.PrefetchScalarGridSpec(
            num_scalar_prefetch=3,
            grid=(n_blocks, nd),
            in_specs=[
                pl.BlockSpec((block_rows, f), lambda r, j, be, nu, nv: (rows(r, nu), 0)),
                pl.BlockSpec((None, None, f, tn), lambda r, j, be, nu, nv: (layer, be[rows(r, nu)], 0, tile(r, j, nu, nd))),
                pl.BlockSpec((None, None, 1, tn), lambda r, j, be, nu, nv: (layer, be[rows(r, nu)], 0, tile(r, j, nu, nd))),
            ],
            out_specs=pl.BlockSpec((block_rows, tn), lambda r, j, be, nu, nv: (rows(r, nu), tile(r, j, nu, nd))),
            scratch_shapes=[pltpu.VMEM((f, tn), BF16)],
        ),
        out_shape=jax.ShapeDtypeStruct((n_rows, d), F32),
        compiler_params=_cparams("arbitrary", "arbitrary"),
        name="expert_down",
    )(block_exp, n_used, n_valid, act, w2, b2r)


def _blocked_cumsum(onehot):
    n, e = onehot.shape
    blk = math.gcd(n, 512)
    x = onehot.reshape(n // blk, blk, e).astype(BF16)
    tril = jnp.tril(jnp.ones((blk, blk), BF16))
    within = jnp.einsum("ij,bje->bie", tril, x, preferred_element_type=F32).astype(jnp.int32)
    totals = within[:, -1, :]
    offsets = jnp.cumsum(totals, axis=0) - totals
    return (within + offsets[:, None, :]).reshape(n, e)


def moe_residual(x, hp, logits, gate, w1, b1, w2, b2, layer, rows_lat, seq):
    n_tok, half = hp.shape
    d = 2 * half
    n_exp = w1.shape[1]
    top_val, top_idx = lax.top_k(logits, TOP_K)
    gates = jax.nn.softmax(top_val, axis=-1)
    n_assign = n_tok * TOP_K
    flat_e = top_idx.reshape(-1)
    onehot = (flat_e[:, None] == jnp.arange(n_exp, dtype=flat_e.dtype)[None, :]).astype(jnp.int32)
    csum = _blocked_cumsum(onehot)
    counts = csum[-1]
    block_rows = MOE_SUB_ROWS * -(-int(MOE_LOAD_MARGIN * n_assign / n_exp) // MOE_SUB_ROWS)
    padded = (counts + block_rows - 1) // block_rows * block_rows
    pend = jnp.cumsum(padded)
    pstart = pend - padded
    pos = jnp.sum(onehot * (csum - 1 + pstart[None, :]), axis=1).astype(jnp.int32)
    n_blocks = -(-n_assign // block_rows) + n_exp
    n_rows = n_blocks * block_rows
    block_start = jnp.arange(n_blocks, dtype=jnp.int32) * block_rows
    block_exp = jnp.minimum(jnp.sum((block_start[:, None] >= pend[None, :]).astype(jnp.int32), axis=1), n_exp - 1)
    n_used = (pend[-1] // block_rows).astype(jnp.int32).reshape(1)
    last_row = (pstart + counts)[block_exp]
    n_valid = jnp.clip(last_row - block_start, 0, block_rows).astype(jnp.int32)
    pad_lo = (pstart + counts).astype(jnp.int32)
    pad_hi = (pstart + (counts + MOE_SUB_ROWS - 1) // MOE_SUB_ROWS * MOE_SUB_ROWS).astype(jnp.int32)

    tb = math.gcd(128, _stream_tile(n_tok, rows_lat, seq))
    xb = pl.pallas_call(
        _dispatch_kernel,
        grid_spec=pltpu.PrefetchScalarGridSpec(
            num_scalar_prefetch=3,
            grid=(n_tok // tb,),
            in_specs=[pl.BlockSpec((tb, half), lambda i, pos, lo, hi: (i, 0))],
            out_specs=pl.BlockSpec(memory_space=pl.ANY),
            scratch_shapes=[pltpu.VMEM((8, half), jnp.uint32), pltpu.SemaphoreType.DMA(())],
        ),
        out_shape=jax.ShapeDtypeStruct((n_rows, half), jnp.uint32),
        compiler_params=_cparams("arbitrary"),
        name="expert_dispatch",
    )(pos, pad_lo, pad_hi, hp)
    yb = expert_ffn(block_exp.astype(jnp.int32), n_used, n_valid, xb, w1, b1, w2, b2, layer, block_rows)
    n_batch = gate.shape[0] - 1
    idx = functools.partial(_stream_index, tm=tb, rows_lat=rows_lat, seq=seq, n_batch=n_batch)
    return pl.pallas_call(
        _combine_kernel,
        grid_spec=pltpu.PrefetchScalarGridSpec(
            num_scalar_prefetch=1,
            grid=(n_tok // tb,),
            in_specs=[pl.BlockSpec(memory_space=pl.ANY),
                      pl.BlockSpec((tb, d), lambda i, pos: (i, 0)),
                      pl.BlockSpec((1, 1, d), lambda i, pos: (idx(i), 0, 0)),
                      pl.BlockSpec((tb, TOP_K), lambda i, pos: (i, 0))],
            out_specs=pl.BlockSpec((tb, d), lambda i, pos: (i, 0)),
            scratch_shapes=[pltpu.VMEM((TOP_K, tb, d), F32), pltpu.SemaphoreType.DMA(())],
        ),
        out_shape=jax.ShapeDtypeStruct((n_tok, d), F32),
        compiler_params=_cparams("arbitrary"),
        name="expert_combine",
    )(pos, yb, x, gate, gates)


def _layer(i, x, c_all, p, n_batch, seq, n_ctx, tables, with_ctx):
    d = x.shape[1]
    rows_lat = n_batch * seq
    rows_all = rows_lat + n_batch * n_ctx
    n_out = rows_all if with_ctx else rows_lat
    n_stream = n_batch + 1

    mod = matmul(c_all, p["w_mod"], i, c_all.shape[0], c_all.shape[0], 512, F32, pre="silu", name="modulation")
    mod = (mod[:n_stream] + p["b_mod"][i]).reshape(n_stream, 6, 1, d)
    sh1, sc1, g1, sh2, sc2, g2 = (mod[:, k] for k in range(6))

    h1 = norm_mod(x, p["g_mix"][i], sc1, sh1, rows_all, rows_lat, seq, BF16)
    in_width = p["w_in"].shape[2]

    n_g = p["ssm_lambda_re"].shape[2]
    ssm_w = n_g * SSM_GROUP
    n_wh = p["win_sink"].shape[1]
    win_q = n_wh * WIN_HEAD_DIM
    win_kv = WIN_KV_HEADS * WIN_HEAD_DIM
    n_dh = (in_width - ssm_w - win_q - 2 * win_kv - 3 * d) // (2 * 2 * DIFF_QK_DIM + DIFF_V_DIM)
    diff_w = n_dh * 2 * DIFF_QK_DIM
    widths = dict(u=ssm_w, qw=win_q, kw=win_kv, vw=win_kv, qd=diff_w, kd=diff_w, vd=diff_w, gates=3 * d)
    src, c0 = {}, 0
    for name in ("u", "qw", "kw", "vw", "qd", "kd", "vd", "gates"):
        src[name] = c0
        c0 += widths[name]
    modes = dict(u=("plain", 1.0, F32), qw=("rope", WIN_HEAD_DIM ** -0.5, BF16), kw=("rope", 1.0, BF16),
                 vw=("plain", 1.0, BF16), qd=("rope", DIFF_QK_DIM ** -0.5, BF16), kd=("rope", 1.0, BF16),
                 vd=("plain", 1.0, BF16), gates=("sigmoid", 1.0, BF16))
    seg = {}
    for name, (mode, scale, dtype) in modes.items():
        w_seg = p["w_in"][i][:, src[name]:src[name] + widths[name]].astype(BF16)
        seg[name] = in_proj(h1, w_seg, 0, widths[name], mode, scale, dtype, tables, rows_lat, seq)

    mats = s5_matrices(p["ssm_lambda_re"][i], p["ssm_lambda_im"][i], p["ssm_log_dt"][i], p["ssm_b_re"][i],
                       p["ssm_b_im"][i], p["ssm_c_re"][i], p["ssm_c_im"][i], p["ssm_d"][i], S5_CHUNK)
    y = s5_mixer(seg["u"], mats, rows_all, n_batch, seq, n_ctx)
    o_ssm = s5_glu(y, p["w_glu"], p["b_glu"], i, n_out)

    sink = p["win_sink"][i].astype(F32)
    o_win = window_attention(sink, seg["qw"], seg["kw"], seg["vw"], n_batch, seq, n_ctx, n_wh, rows_lat, False)

    lam_p = p["diff_lambda"][i].astype(F32)
    lambda_init = 0.8 - 0.6 * math.exp(-0.3 * i)
    lam = (jnp.exp(jnp.sum(lam_p[0] * lam_p[1])) - jnp.exp(jnp.sum(lam_p[2] * lam_p[3])) + lambda_init).reshape(1)
    gain = p["diff_subln"][i].astype(F32).reshape(1, DIFF_V_DIM)
    o_diff = diff_attention(lam, gain, seg["qd"], seg["kd"], seg["vd"], n_batch, seq, n_ctx, n_dh, rows_lat, False,
                            1.0 - lambda_init)
    if with_ctx:
        o_win_c = window_attention(sink, seg["qw"], seg["kw"], seg["vw"], n_batch, seq, n_ctx, n_wh, rows_lat, True)
        o_diff_c = diff_attention(lam, gain, seg["qd"], seg["kd"], seg["vd"], n_batch, seq, n_ctx, n_dh, rows_lat, True,
                                  1.0 - lambda_init)
        o_win = jnp.concatenate([o_win, o_win_c], axis=0)
        o_diff = jnp.concatenate([o_diff, o_diff_c], axis=0)

    merged = gated_merge(o_ssm, o_win, o_diff, p["w_branch_ssm"], p["w_branch_win"], p["w_branch_diff"], seg["gates"],
                         i, n_out, d)
    x = out_proj_residual(merged, p["w_out"], x, g1, i, n_out, rows_lat, seq)

    hp, logits = norm_mod(x, p["g_ffn"][i], sc2, sh2, n_out, rows_lat, seq, BF16,
                          router=(p["w_router"][i], p["b_router"][i]))
    return moe_residual(x, hp, logits, g2, p["w_exp1"], p["b_exp1"], p["w_exp2"], p["b_exp2"], i, rows_lat, seq)


def kernel(x, c, ctx, c_ctx, w_mod, b_mod, g_mix, g_ffn, w_in, ssm_lambda_re, ssm_lambda_im, ssm_log_dt, ssm_b_re, ssm_b_im, ssm_c_re, ssm_c_im, ssm_d, w_glu, b_glu, win_sink, diff_lambda, diff_subln, w_branch_ssm, w_branch_win, w_branch_diff, w_out, w_router, b_router, w_exp1, b_exp1, w_exp2, b_exp2, g_final):
    n_batch, seq, d = x.shape
    n_ctx = ctx.shape[1]
    depth = w_mod.shape[0]
    p = dict(w_mod=w_mod, b_mod=b_mod, g_mix=g_mix, g_ffn=g_ffn, w_in=w_in, ssm_lambda_re=ssm_lambda_re,
             ssm_lambda_im=ssm_lambda_im, ssm_log_dt=ssm_log_dt, ssm_b_re=ssm_b_re, ssm_b_im=ssm_b_im,
             ssm_c_re=ssm_c_re, ssm_c_im=ssm_c_im, ssm_d=ssm_d, w_glu=w_glu, b_glu=b_glu, win_sink=win_sink,
             diff_lambda=diff_lambda, diff_subln=diff_subln, w_branch_ssm=w_branch_ssm, w_branch_win=w_branch_win,
             w_branch_diff=w_branch_diff, w_out=w_out, w_router=w_router, b_router=b_router, w_exp1=w_exp1,
             b_exp1=b_exp1, w_exp2=w_exp2, b_exp2=b_exp2)
    rows_lat = n_batch * seq
    rows = jnp.concatenate([x.reshape(rows_lat, d), ctx.reshape(n_batch * n_ctx, d)], axis=0)
    c_all = jnp.concatenate([c, c_ctx[None, :], jnp.zeros((8 - (n_batch + 1) % 8, d), F32)], axis=0)
    tables = rope_tables(seq, WIN_HEAD_DIM)
    for i in range(depth):
        with_ctx = i < depth - 1
        rows = _layer(i, rows, c_all, p, n_batch, seq, n_ctx, tables, with_ctx)
    no_mod = jnp.zeros((n_batch + 1, 1, d), F32)
    out = norm_mod(rows, g_final, no_mod, no_mod, rows_lat, rows_lat, seq, F32)
    return out.reshape(n_batch, seq, d)
```

```python
import functools
import math

import jax
import jax.numpy as jnp
import numpy as np
from jax import lax
from jax.experimental import pallas as pl
from jax.experimental.pallas import tpu as pltpu

F32 = jnp.float32
BF16 = jnp.bfloat16

GRID_W = 64
SSM_GROUP = 16
WIN_KV_HEADS = 2
WIN_HEAD_DIM = 64
WINDOW = 128
DIFF_QK_DIM = 64
DIFF_V_DIM = 2 * DIFF_QK_DIM
ATTN_BLOCK = 128
ROPE_BASE = 10000.0
TOP_K = 4
SWIGLU_LIMIT = 7.0
SWIGLU_ALPHA = 1.702
NORM_EPS = 1e-6
SUBLN_EPS = 1e-5
NEG_INF = -1e30

LANE = 128
VMEM_LIMIT_BYTES = 56 * 1024 * 1024
S5_CHUNK = 32
MOE_LOAD_MARGIN = 1.4
MOE_SUB_ROWS = 256
ROW_TILE = 512


def _cparams(*sem):
    return pltpu.CompilerParams(dimension_semantics=sem, vmem_limit_bytes=VMEM_LIMIT_BYTES)


def _dot(a, b):
    return jnp.dot(a, b, preferred_element_type=F32)


def _dot_nt(a, b):
    return lax.dot_general(a, b, (((1,), (1,)), ((), ())), preferred_element_type=F32)


def _sigmoid(x):
    return 1.0 / (1.0 + jnp.exp(-x))


def _row_tile(n_rows, cap):
    best = 16
    for t in range(16, cap + 1, 16):
        if n_rows % t == 0:
            best = t
    return best


def _norm_mod_kernel(x_ref, g_ref, sc_ref, sh_ref, o_ref):
    x = x_ref[...]
    y = x * lax.rsqrt(jnp.mean(x * x, axis=-1, keepdims=True) + NORM_EPS) * g_ref[...]
    o_ref[...] = (y * (1.0 + sc_ref[0]) + sh_ref[0]).astype(o_ref.dtype)


def _norm_router_kernel(x_ref, g_ref, sc_ref, sh_ref, wr_ref, br_ref, o_ref, lg_ref):
    x = x_ref[...]
    y = x * lax.rsqrt(jnp.mean(x * x, axis=-1, keepdims=True) + NORM_EPS) * g_ref[...]
    h = y * (1.0 + sc_ref[0]) + sh_ref[0]
    o_ref[...] = _pack_halves(h)
    lg_ref[...] = jnp.dot(h, wr_ref[...], preferred_element_type=F32, precision=lax.Precision.HIGHEST) + br_ref[...]


def _stream_tile(n_rows, rows_lat, seq):
    tm = math.gcd(ROW_TILE, seq)
    return math.gcd(tm, n_rows - rows_lat) if n_rows > rows_lat else tm


def _stream_index(i, tm, rows_lat, seq, n_batch):
    r = i * tm
    return jnp.where(r >= rows_lat, n_batch, r // seq)


def norm_mod(x, gain, sc, sh, n_rows, rows_lat, seq, out_dtype, router=None):
    d = x.shape[1]
    tm = _stream_tile(n_rows, rows_lat, seq)
    n_batch = sc.shape[0] - 1
    idx = functools.partial(_stream_index, tm=tm, rows_lat=rows_lat, seq=seq, n_batch=n_batch)
    in_specs = [
        pl.BlockSpec((tm, d), lambda i: (i, 0)),
        pl.BlockSpec((1, d), lambda i: (0, 0)),
        pl.BlockSpec((1, 1, d), lambda i: (idx(i), 0, 0)),
        pl.BlockSpec((1, 1, d), lambda i: (idx(i), 0, 0)),
    ]
    if router is None:
        return pl.pallas_call(
            _norm_mod_kernel,
            grid=(n_rows // tm,),
            in_specs=in_specs,
            out_specs=pl.BlockSpec((tm, d), lambda i: (i, 0)),
            out_shape=jax.ShapeDtypeStruct((n_rows, d), out_dtype),
            compiler_params=_cparams("parallel"),
            name="norm_mod",
        )(x, gain.reshape(1, d), sc, sh)
    w_router, b_router = router
    n_exp = w_router.shape[1]
    return pl.pallas_call(
        _norm_router_kernel,
        grid=(n_rows // tm,),
        in_specs=in_specs + [
            pl.BlockSpec((d, n_exp), lambda i: (0, 0)),
            pl.BlockSpec((1, n_exp), lambda i: (0, 0)),
        ],
        out_specs=[pl.BlockSpec((tm, d // 2), lambda i: (i, 0)), pl.BlockSpec((tm, n_exp), lambda i: (i, 0))],
        out_shape=[jax.ShapeDtypeStruct((n_rows, d // 2), jnp.uint32), jax.ShapeDtypeStruct((n_rows, n_exp), F32)],
        compiler_params=_cparams("parallel"),
        name="norm_router",
    )(x, gain.reshape(1, d), sc, sh, w_router, b_router.reshape(1, n_exp))


def _mm_kernel(x_ref, w_ref, o_ref, *, pre):
    x = x_ref[...]
    if pre == "silu":
        x = x * _sigmoid(x)
    o_ref[...] = _dot(x.astype(BF16), w_ref[...].astype(BF16)).astype(o_ref.dtype)


def matmul(x, w_stack, layer, n_rows, tm, tn, out_dtype, pre=None, name="matmul"):
    k = x.shape[1]
    n = w_stack.shape[2]
    return pl.pallas_call(
        functools.partial(_mm_kernel, pre=pre),
        grid=(n_rows // tm, n // tn),
        in_specs=[
            pl.BlockSpec((tm, k), lambda i, j: (i, 0)),
            pl.BlockSpec((None, k, tn), lambda i, j: (layer, 0, j)),
        ],
        out_specs=pl.BlockSpec((tm, tn), lambda i, j: (i, j)),
        out_shape=jax.ShapeDtypeStruct((n_rows, n), out_dtype),
        compiler_params=_cparams("parallel", "arbitrary"),
        name=name,
    )(x, w_stack)


def _in_proj_kernel(x_ref, w_ref, *refs, mode, scale):
    o_ref = refs[-1]
    acc = _dot(x_ref[...], w_ref[...])
    if mode == "sigmoid":
        o_ref[...] = _sigmoid(acc).astype(o_ref.dtype)
        return
    if mode == "plain":
        o_ref[...] = (acc if scale == 1.0 else acc * scale).astype(o_ref.dtype)
        return
    cos_ref, sin_ref = refs[0], refs[1]
    lane = lax.broadcasted_iota(jnp.int32, (1, LANE), 1)
    first_half = (lane % 32) < 16
    cos = cos_ref[...] if scale == 1.0 else cos_ref[...] * scale
    sin = sin_ref[...] if scale == 1.0 else sin_ref[...] * scale
    for c in range(acc.shape[1] // LANE):
        x = acc[:, c * LANE:(c + 1) * LANE]
        partner = jnp.where(first_half, pltpu.roll(x, LANE - 16, axis=1), pltpu.roll(x, 16, axis=1))
        o_ref[:, c * LANE:(c + 1) * LANE] = (x * cos + partner * sin).astype(o_ref.dtype)


def in_proj(h, w, col0, width, mode, scale, out_dtype, tables, rows_lat, seq):
    n_rows, k = h.shape
    tm = _stream_tile(n_rows, rows_lat, seq)
    tn = math.gcd(1024, width)
    assert col0 % tn == 0
    in_specs = [pl.BlockSpec((tm, k), lambda i, j: (i, 0)),
                pl.BlockSpec((k, tn), lambda i, j: (0, col0 // tn + j))]
    args = [h, w]
    if mode == "rope":
        cos_t, sin_t = tables
        tiles_per_seq = seq // tm
        lat_tiles = rows_lat // tm
        cos_x = jnp.concatenate([cos_t, jnp.ones((tm, LANE), F32)], axis=0)
        sin_x = jnp.concatenate([sin_t, jnp.zeros((tm, LANE), F32)], axis=0)
        tab = pl.BlockSpec((tm, LANE), lambda i, j: (jnp.where(i < lat_tiles, i % tiles_per_seq, tiles_per_seq), 0))
        in_specs += [tab, tab]
        args += [cos_x, sin_x]
    return pl.pallas_call(
        functools.partial(_in_proj_kernel, mode=mode, scale=scale),
        grid=(n_rows // tm, width // tn),
        in_specs=in_specs,
        out_specs=pl.BlockSpec((tm, tn), lambda i, j: (i, j)),
        out_shape=jax.ShapeDtypeStruct((n_rows, width), out_dtype),
        compiler_params=_cparams("parallel", "arbitrary"),
        name="in_proj_" + mode,
    )(*args)


def rope_tables(seq, head_dim):
    assert head_dim == 64
    rows = seq // GRID_W
    r, col = jnp.meshgrid(jnp.arange(rows, dtype=F32), jnp.arange(GRID_W, dtype=F32), indexing="ij")
    half = head_dim // 2
    inv_freq = ROPE_BASE ** (-jnp.arange(0, half, 2, dtype=F32) / half)
    ang_r = r.reshape(-1)[:, None] * inv_freq[None, :]
    ang_c = col.reshape(-1)[:, None] * inv_freq[None, :]
    cos = jnp.concatenate([jnp.cos(ang_r), jnp.cos(ang_r), jnp.cos(ang_c), jnp.cos(ang_c)], axis=-1)
    sin = jnp.concatenate([-jnp.sin(ang_r), jnp.sin(ang_r), -jnp.sin(ang_c), jnp.sin(ang_c)], axis=-1)
    return jnp.tile(cos, (1, LANE // head_dim)), jnp.tile(sin, (1, LANE // head_dim))


def _win_attn_kernel(sink_ref, q_ref, *refs, n_heads, band, seq):
    if band:
        kp_ref, kc_ref, kn_ref, vp_ref, vc_ref, vn_ref, kx_ref, vx_ref, o_ref = refs
    else:
        kx_ref, vx_ref, o_ref = refs
    blk = q_ref.shape[0]
    dh = WIN_HEAD_DIM
    grp = n_heads // WIN_KV_HEADS
    lane = lax.broadcasted_iota(jnp.int32, (1, LANE), 1)
    rows = grp * blk
    if band:
        n = pl.program_id(1)
        k_all = jnp.concatenate([kp_ref[...], kc_ref[...], kn_ref[...], kx_ref[...]], axis=0)
        v_all = jnp.concatenate([vp_ref[...], vc_ref[...], vn_ref[...], vx_ref[...]], axis=0)
        n_keys = k_all.shape[0]
        col = lax.broadcasted_iota(jnp.int32, (rows, n_keys), 1)
        qpos = n * blk + (lax.broadcasted_iota(jnp.int32, (rows, n_keys), 0) & (blk - 1))
        kpos = (n - 1) * blk + col
        mask = (col >= 3 * blk) | ((jnp.abs(qpos - kpos) <= WINDOW) & (kpos >= 0) & (kpos < seq))
    else:
        k_all, v_all = kx_ref[...], vx_ref[...]
        n_keys = k_all.shape[0]
    ones_col = (lax.broadcasted_iota(jnp.int32, (n_keys, LANE), 1) == 0).astype(BF16)
    v_aug = jnp.concatenate([v_all, ones_col], axis=1)
    outs = [None] * n_heads
    for kvh in range(WIN_KV_HEADS):
        keep = (lane >= kvh * dh) & (lane < (kvh + 1) * dh)
        q_parts, sink_parts = [], []
        for g in range(grp):
            h = kvh * grp + g
            c = (h * dh) // LANE
            qc = q_ref[:, c * LANE:(c + 1) * LANE]
            if (h * dh) % LANE != kvh * dh:
                qc = jnp.concatenate([qc[:, dh:], qc[:, :dh]], axis=1)
            q_parts.append(jnp.where(keep, qc, jnp.zeros_like(qc)))
            sink_parts.append(jnp.full((blk, 1), sink_ref[h], F32))
        qs = jnp.concatenate(q_parts, axis=0)
        sink = jnp.concatenate(sink_parts, axis=0)
        s = _dot_nt(qs, k_all)
        if band:
            s = jnp.where(mask, s, NEG_INF)
        m = jnp.maximum(jnp.max(s, axis=-1, keepdims=True), sink)
        acc = _dot(jnp.exp(s - m).astype(BF16), v_aug)
        inv = 1.0 / (acc[:, LANE:LANE + 1] + jnp.exp(sink - m))
        o = acc[:, kvh * dh:(kvh + 1) * dh] * inv
        for g in range(grp):
            outs[kvh * grp + g] = o[g * blk:(g + 1) * blk, :]
    o_ref[...] = jnp.concatenate(outs, axis=1).astype(o_ref.dtype)


def window_attention(sink, q, k, v, n_batch, seq, n_ctx, n_heads, rows_lat, ctx_queries):
    blk = ATTN_BLOCK
    seq_q = n_ctx if ctx_queries else seq
    nb = seq_q // blk
    q0 = rows_lat // blk if ctx_queries else 0
    x0 = rows_lat // n_ctx
    qw = n_heads * WIN_HEAD_DIM
    kvw = WIN_KV_HEADS * WIN_HEAD_DIM
    in_specs = [
        pl.BlockSpec(memory_space=pltpu.SMEM),
        pl.BlockSpec((blk, qw), lambda b, n: (q0 + b * nb + n, 0)),
    ]
    args = [sink, q]
    if not ctx_queries:
        band_specs = [
            pl.BlockSpec((blk, kvw), lambda b, n: (b * nb + jnp.maximum(n - 1, 0), 0)),
            pl.BlockSpec((blk, kvw), lambda b, n: (b * nb + n, 0)),
            pl.BlockSpec((blk, kvw), lambda b, n: (b * nb + jnp.minimum(n + 1, nb - 1), 0)),
        ]
        in_specs += band_specs + band_specs
        args += [k, k, k, v, v, v]
    in_specs += [pl.BlockSpec((n_ctx, kvw), lambda b, n: (x0 + b, 0))] * 2
    args += [k, v]
    return pl.pallas_call(
        functools.partial(_win_attn_kernel, n_heads=n_heads, band=not ctx_queries, seq=seq),
        grid=(n_batch, nb),
        in_specs=in_specs,
        out_specs=pl.BlockSpec((blk, qw), lambda b, n: (b * nb + n, 0)),
        out_shape=jax.ShapeDtypeStruct((n_batch * seq_q, qw), BF16),
        compiler_params=_cparams("parallel", "parallel"),
        name="context_window_attention" if ctx_queries else "window_attention",
    )(*args)


def _diff_attn_kernel(lam_ref, q_ref, gain_ref, *refs, with_lat, post_scale):
    if with_lat:
        kl_ref, vl_ref, kx_ref, vx_ref, o_ref = refs
    else:
        kx_ref, vx_ref, o_ref = refs
    lam = lam_ref[0]
    q = q_ref[...]
    lane = lax.broadcasted_iota(jnp.int32, (1, LANE), 1)
    zero = jnp.zeros_like(q)
    kx = kx_ref[...]

    def softmax_parts(qm):
        s_x = _dot_nt(qm, kx)
        m = jnp.max(s_x, axis=-1, keepdims=True)
        if with_lat:
            s_l = _dot_nt(qm, kl_ref[...])
            m = jnp.maximum(m, jnp.max(s_l, axis=-1, keepdims=True))
            e_l = jnp.exp(s_l - m)
        else:
            e_l = None
        e_x = jnp.exp(s_x - m)
        den = jnp.sum(e_x, axis=-1, keepdims=True)
        if with_lat:
            den = den + jnp.sum(e_l, axis=-1, keepdims=True)
        return e_l, e_x, 1.0 / den

    e1l, e1x, inv1 = softmax_parts(jnp.where(lane < DIFF_QK_DIM, q, zero))
    e2l, e2x, inv2 = softmax_parts(jnp.where(lane >= DIFF_QK_DIM, q, zero))
    w2 = lam * inv2
    o = _dot((e1x * inv1 - e2x * w2).astype(BF16), vx_ref[...])
    if with_lat:
        o = o + _dot((e1l * inv1 - e2l * w2).astype(BF16), vl_ref[...])
    o = o * lax.rsqrt(jnp.mean(o * o, axis=-1, keepdims=True) + SUBLN_EPS) * gain_ref[...]
    o_ref[...] = (o * post_scale).astype(o_ref.dtype)


def diff_attention(lam, gain, q, k, v, n_batch, seq, n_ctx, n_heads, rows_lat, ctx_queries, post_scale):
    seq_q = n_ctx if ctx_queries else seq
    tq = min(256, seq_q)
    nq = seq_q // tq
    q0 = rows_lat // tq if ctx_queries else 0
    x0 = rows_lat // n_ctx
    in_specs = [
        pl.BlockSpec(memory_space=pltpu.SMEM),
        pl.BlockSpec((tq, LANE), lambda b, h, i: (q0 + b * nq + i, h)),
        pl.BlockSpec((1, LANE), lambda b, h, i: (0, 0)),
    ]
    args = [lam, q, gain]
    if not ctx_queries:
        in_specs += [pl.BlockSpec((seq, LANE), lambda b, h, i: (b, h))] * 2
        args += [k, v]
    in_specs += [pl.BlockSpec((n_ctx, LANE), lambda b, h, i: (x0 + b, h))] * 2
    args += [k, v]
    return pl.pallas_call(
        functools.partial(_diff_attn_kernel, with_lat=not ctx_queries, post_scale=post_scale),
        grid=(n_batch, n_heads, nq),
        in_specs=in_specs,
        out_specs=pl.BlockSpec((tq, LANE), lambda b, h, i: (b * nq + i, h)),
        out_shape=jax.ShapeDtypeStruct((n_batch * seq_q, n_heads * LANE), BF16),
        compiler_params=_cparams("parallel", "parallel", "arbitrary"),
        name="context_diff_attention" if ctx_queries else "diff_attention",
    )(*args)


def s5_matrices(lam_re, lam_im, log_dt, b_re, b_im, c_re, c_im, d_skip, t_chunk):
    n_dir, n_g, n_p = lam_re.shape
    n_h = b_re.shape[-1]
    lam_re = jnp.minimum(lam_re.astype(F32), -1e-4)
    lam_im = lam_im.astype(F32)
    dt = jnp.exp(log_dt.astype(F32))[..., None]
    mag = jnp.exp(lam_re * dt)
    a_re = mag * jnp.cos(lam_im * dt)
    a_im = mag * jnp.sin(lam_im * dt)
    den = lam_re * lam_re + lam_im * lam_im
    k_re = ((a_re - 1.0) * lam_re + a_im * lam_im) / den
    k_im = (a_im * lam_re - (a_re - 1.0) * lam_im) / den
    b_re = b_re.astype(F32)
    b_im = b_im.astype(F32)
    bb_re = k_re[..., None] * b_re - k_im[..., None] * b_im
    bb_im = k_re[..., None] * b_im + k_im[..., None] * b_re
    c_re = c_re.astype(F32)
    c_im = c_im.astype(F32)
    ldt_re = lam_re * dt
    ldt_im = lam_im * dt
    steps = jnp.arange(t_chunk, dtype=F32)

    def power(d, t, p_last):
        lr = ldt_re[d][:, None, :] if p_last else ldt_re[d][:, :, None]
        li = ldt_im[d][:, None, :] if p_last else ldt_im[d][:, :, None]
        tt = t[None, :, None] if p_last else t[None, None, :]
        mag_t = jnp.exp(lr * tt)
        return mag_t * jnp.cos(li * tt), mag_t * jnp.sin(li * tt)

    def c_times_power(d, t):
        pr, pi = power(d, t, False)
        cr = jnp.swapaxes(c_re[d], 1, 2)[:, :, None, :]
        ci = jnp.swapaxes(c_im[d], 1, 2)[:, :, None, :]
        re = cr * pr[..., None] - ci * pi[..., None]
        im = cr * pi[..., None] + ci * pr[..., None]
        return jnp.concatenate([re, -im], axis=1).reshape(n_g, 2 * n_p, t_chunk * n_h)

    def b_times_power(d, t):
        pr, pi = power(d, t, True)
        br = jnp.swapaxes(bb_re[d], 1, 2)[:, None, :, :]
        bi = jnp.swapaxes(bb_im[d], 1, 2)[:, None, :, :]
        re = pr[:, :, None, :] * br - pi[:, :, None, :] * bi
        im = pr[:, :, None, :] * bi + pi[:, :, None, :] * br
        return jnp.concatenate([re, im], axis=-1).reshape(n_g, t_chunk * n_h, 2 * n_p)

    ca_lag = jnp.stack([c_times_power(0, steps), c_times_power(1, t_chunk - 1 - steps)])
    bbt = jnp.stack([jnp.concatenate([jnp.swapaxes(bb_re[d], 1, 2), jnp.swapaxes(bb_im[d], 1, 2)], axis=-1)
                     for d in range(2)])
    p_mat = jnp.concatenate([b_times_power(0, t_chunk - 1 - steps), b_times_power(1, steps)], axis=-1)
    q_mat = jnp.concatenate([c_times_power(0, steps + 1.0), c_times_power(1, t_chunk - steps)], axis=1)
    d_tile = jnp.tile(d_skip.astype(F32).reshape(n_g, 1, n_h), (1, 1, t_chunk))
    mag_t = jnp.exp(ldt_re * t_chunk)
    at_re, at_im = mag_t * jnp.cos(ldt_im * t_chunk), mag_t * jnp.sin(ldt_im * t_chunk)
    at_mul = jnp.concatenate([at_re, at_re], axis=-1)
    at_swp = jnp.concatenate([-at_im, at_im], axis=-1)
    return ca_lag, bbt, p_mat.astype(BF16), q_mat.astype(BF16), d_tile, at_mul, at_swp


def _s5_local_kernel(x_ref, p_ref, e_ref):
    e_ref[...] = _dot(x_ref[...].astype(BF16), p_ref[...])


def _s5_out_kernel(x_ref, s_ref, ca_ref, bbt_ref, q_ref, d_ref, y_ref, m_ref):
    n_h = bbt_ref.shape[1]
    th = m_ref.shape[0]
    t_chunk = th // n_h
    hi = lax.Precision.HIGHEST
    row_f = jnp.dot(bbt_ref[0], ca_ref[0], preferred_element_type=F32, precision=hi)
    row_b = jnp.dot(bbt_ref[1], ca_ref[1], preferred_element_type=F32, precision=hi)
    lane = lax.broadcasted_iota(jnp.int32, (n_h, th), 1)
    for j in range(t_chunk):
        fwd = jnp.where(lane >= n_h * j, pltpu.roll(row_f, n_h * j, axis=1), 0.0) if j else row_f
        back = n_h * (t_chunk - 1 - j)
        bwd = jnp.where(lane < n_h * (j + 1), pltpu.roll(row_b, th - back, axis=1), 0.0) if back else row_b
        m_ref[n_h * j:n_h * (j + 1), :] = (fwd + bwd).astype(m_ref.dtype)
    x = x_ref[...]
    y_ref[...] = _dot(x.astype(BF16), m_ref[...]) + _dot(s_ref[...].astype(BF16), q_ref[...]) + x * d_ref[...]


def _s5_scan_kernel(ef_ref, eb_ref, mul_ref, swp_ref, sf_ref, sb_ref, *, n_batch, lat_chunks, ctx_chunks):
    half = ef_ref.shape[2] // 2
    mul_f, mul_b = mul_ref[0], mul_ref[1]
    swp_f, swp_b = swp_ref[0], swp_ref[1]
    zero = jnp.zeros(ef_ref.shape[1:], F32)

    def step(c_f, c_b, carry):
        s_f, s_b = carry
        sf_ref[c_f] = s_f
        sb_ref[c_b] = s_b
        s_f = s_f * mul_f + pltpu.roll(s_f, half, axis=1) * swp_f + ef_ref[c_f]
        s_b = s_b * mul_b + pltpu.roll(s_b, half, axis=1) * swp_b + eb_ref[c_b]
        return s_f, s_b

    for b in range(n_batch):
        ctx0 = n_batch * lat_chunks + b * ctx_chunks
        lat0 = b * lat_chunks
        carry = lax.fori_loop(0, ctx_chunks, lambda i, cr: step(ctx0 + i, ctx0 + ctx_chunks - 1 - i, cr), (zero, zero))
        lax.fori_loop(0, lat_chunks, lambda i, cr: step(lat0 + i, lat0 + lat_chunks - 1 - i, cr), carry)


def s5_mixer(proj, mats, n_rows, n_batch, seq, n_ctx):
    ca_lag, bbt, p_mat, q_mat, d_tile, at_mul, at_swp = mats
    n_g, th, _ = p_mat.shape
    n_h = SSM_GROUP
    t_chunk = th // n_h
    p4 = p_mat.shape[2]
    n_p2 = p4 // 2
    width = n_g * n_h
    n_chunks = n_rows // t_chunk
    u = proj[:, :width]
    x = u.reshape(n_chunks, t_chunk, n_g, n_h).transpose(2, 0, 1, 3).reshape(n_g, n_chunks, th)
    e = pl.pallas_call(
        _s5_local_kernel,
        grid=(n_g,),
        in_specs=[pl.BlockSpec((None, n_chunks, th), lambda g: (g, 0, 0)),
                  pl.BlockSpec((None, th, p4), lambda g: (g, 0, 0))],
        out_specs=pl.BlockSpec((n_chunks, p4), lambda g: (0, g)),
        out_shape=jax.ShapeDtypeStruct((n_chunks, n_g * p4), F32),
        compiler_params=_cparams("parallel"),
        name="s5_local_state",
    )(x, p_mat)
    e = e.reshape(n_chunks, n_g, 2, n_p2)
    gb = 16
    spec = pl.BlockSpec((n_chunks, gb, n_p2), lambda g: (0, g, 0))
    tab = pl.BlockSpec((2, gb, n_p2), lambda g: (0, g, 0))
    s_f, s_b = pl.pallas_call(
        functools.partial(_s5_scan_kernel, n_batch=n_batch, lat_chunks=seq // t_chunk, ctx_chunks=n_ctx // t_chunk),
        grid=(n_g // gb,),
        in_specs=[spec, spec, tab, tab],
        out_specs=[spec, spec],
        out_shape=[jax.ShapeDtypeStruct((n_chunks, n_g, n_p2), F32)] * 2,
        compiler_params=_cparams("parallel"),
        name="s5_chunk_scan",
    )(e[:, :, 0], e[:, :, 1], at_mul, at_swp)
    s_in = jnp.stack([s_f, s_b], axis=2).reshape(n_chunks, n_g * p4)
    y = pl.pallas_call(
        _s5_out_kernel,
        grid=(n_g,),
        in_specs=[pl.BlockSpec((None, n_chunks, th), lambda g: (g, 0, 0)),
                  pl.BlockSpec((n_chunks, p4), lambda g: (0, g)),
                  pl.BlockSpec((2, None, p4 // 2, th), lambda g: (0, g, 0, 0)),
                  pl.BlockSpec((2, None, n_h, p4 // 2), lambda g: (0, g, 0, 0)),
                  pl.BlockSpec((None, p4, th), lambda g: (g, 0, 0)),
                  pl.BlockSpec((None, 1, th), lambda g: (g, 0, 0))],
        out_specs=pl.BlockSpec((None, n_chunks, th), lambda g: (g, 0, 0)),
        out_shape=jax.ShapeDtypeStruct((n_g, n_chunks, th), F32),
        scratch_shapes=[pltpu.VMEM((th, th), BF16)],
        compiler_params=_cparams("parallel"),
        name="s5_output",
    )(x, s_in, ca_lag, bbt, q_mat, d_tile)
    return y.reshape(n_g, n_chunks, t_chunk, n_h).transpose(1, 2, 0, 3).reshape(n_rows, width)


def _glu_kernel(y_ref, w_ref, b_ref, o_ref):
    y = y_ref[...]
    gy = 0.5 * y * (1.0 + jnp.tanh(math.sqrt(2.0 / math.pi) * (y + 0.044715 * (y * y * y))))
    z = _dot(gy.astype(BF16), w_ref[...].astype(BF16)) + b_ref[...]
    o_ref[...] = (gy * _sigmoid(z)).astype(o_ref.dtype)


def s5_glu(y, w_glu, b_glu, layer, n_rows):
    width = y.shape[1]
    tm = _row_tile(n_rows, ROW_TILE)
    return pl.pallas_call(
        _glu_kernel,
        grid=(n_rows // tm,),
        in_specs=[pl.BlockSpec((tm, width), lambda i: (i, 0)),
                  pl.BlockSpec((None, width, width), lambda i: (layer, 0, 0)),
                  pl.BlockSpec((None, 1, width), lambda i: (layer, 0, 0))],
        out_specs=pl.BlockSpec((tm, width), lambda i: (i, 0)),
        out_shape=jax.ShapeDtypeStruct((n_rows, width), BF16),
        compiler_params=_cparams("parallel"),
        name="s5_glu",
    )(y, w_glu, b_glu.reshape(b_glu.shape[0], 1, width))


def _merge_kernel(os_ref, ow_ref, od_ref, ws_ref, ww_ref, wd_ref, gs_ref, gw_ref, gd_ref, o_ref):
    m = (gs_ref[...].astype(F32) * _dot(os_ref[...], ws_ref[...].astype(BF16))
         + gw_ref[...].astype(F32) * _dot(ow_ref[...], ww_ref[...].astype(BF16))
         + gd_ref[...].astype(F32) * _dot(od_ref[...], wd_ref[...].astype(BF16)))
    o_ref[...] = m.astype(o_ref.dtype)


def gated_merge(o_ssm, o_win, o_diff, w_s, w_w, w_d, proj, layer, n_rows, d):
    tm = _row_tile(n_rows, 1088)
    tn = 256
    assert d % tn == 0
    g0 = 0
    nd = d // tn

    def branch(arr):
        return pl.BlockSpec((tm, arr.shape[1]), lambda i, j: (i, 0))

    def weight(w):
        return pl.BlockSpec((None, w.shape[1], tn), lambda i, j: (layer, 0, j))

    def gate(k):
        return pl.BlockSpec((tm, tn), lambda i, j: (i, g0 + k * nd + j))

    return pl.pallas_call(
        _merge_kernel,
        grid=(n_rows // tm, nd),
        in_specs=[branch(o_ssm), branch(o_win), branch(o_diff), weight(w_s), weight(w_w), weight(w_d),
                  gate(0), gate(1), gate(2)],
        out_specs=pl.BlockSpec((tm, tn), lambda i, j: (i, j)),
        out_shape=jax.ShapeDtypeStruct((n_rows, d), BF16),
        compiler_params=_cparams("parallel", "arbitrary"),
        name="gated_merge",
    )(o_ssm, o_win, o_diff, w_s, w_w, w_d, proj, proj, proj)


def _out_proj_kernel(m_ref, w_ref, x_ref, g_ref, o_ref):
    o_ref[...] = x_ref[...] + g_ref[0] * _dot(m_ref[...], w_ref[...].astype(BF16))


def out_proj_residual(m, w_out, x, gate, layer, n_rows, rows_lat, seq):
    d = x.shape[1]
    tm = _stream_tile(n_rows, rows_lat, seq)
    tn = min(1024, d)
    n_batch = gate.shape[0] - 1
    idx = functools.partial(_stream_index, tm=tm, rows_lat=rows_lat, seq=seq, n_batch=n_batch)
    return pl.pallas_call(
        _out_proj_kernel,
        grid=(n_rows // tm, d // tn),
        in_specs=[pl.BlockSpec((tm, m.shape[1]), lambda i, j: (i, 0)),
                  pl.BlockSpec((None, m.shape[1], tn), lambda i, j: (layer, 0, j)),
                  pl.BlockSpec((tm, tn), lambda i, j: (i, j)),
                  pl.BlockSpec((1, 1, tn), lambda i, j: (idx(i), 0, j))],
        out_specs=pl.BlockSpec((tm, tn), lambda i, j: (i, j)),
        out_shape=jax.ShapeDtypeStruct((n_rows, d), F32),
        compiler_params=_cparams("parallel", "arbitrary"),
        name="out_proj_residual",
    )(m, w_out, x, gate)


def _pack_halves(h):
    half = h.shape[1] // 2
    lo = lax.bitcast_convert_type(h[:, :half].astype(BF16).astype(F32), jnp.uint32)
    hi = lax.bitcast_convert_type(h[:, half:].astype(BF16).astype(F32), jnp.uint32)
    return (lo >> 16) | (hi & jnp.uint32(0xFFFF0000))


def _unpack_halves(x):
    lo = lax.bitcast_convert_type(x << 16, F32).astype(BF16)
    hi = lax.bitcast_convert_type(x & jnp.uint32(0xFFFF0000), F32).astype(BF16)
    return lo, hi


def _dispatch_kernel(pos_ref, pad_lo_ref, pad_hi_ref, h_ref, xb_ref, zero_ref, sem):
    tb = h_ref.shape[0]
    base = pl.program_id(0) * (tb * TOP_K)

    @pl.when(pl.program_id(0) == 0)
    def _():
        zero_ref[...] = jnp.zeros_like(zero_ref)

        def fill(row, carry):
            pltpu.make_async_copy(zero_ref.at[pl.ds(0, 1)], xb_ref.at[pl.ds(row, 1)], sem).start()
            return carry

        def fill_done(row, carry):
            pltpu.make_async_copy(zero_ref.at[pl.ds(0, 1)], xb_ref.at[pl.ds(row, 1)], sem).wait()
            return carry

        def per_expert(body):
            lax.fori_loop(0, pad_lo_ref.shape[0], lambda e, c: lax.fori_loop(pad_lo_ref[e], pad_hi_ref[e], body, c), 0)

        per_expert(fill)
        per_expert(fill_done)

    def issue(j, carry):
        for k in range(TOP_K):
            dst = pos_ref[base + j * TOP_K + k]
            pltpu.make_async_copy(h_ref.at[pl.ds(j, 1)], xb_ref.at[pl.ds(dst, 1)], sem).start(priority=k % 2)
        return carry

    lax.fori_loop(0, tb, issue, 0, unroll=8)

    def drain(j, carry):
        for k in range(TOP_K):
            pltpu.make_async_copy(h_ref.at[pl.ds(j, 1)], xb_ref.at[pl.ds(0, 1)], sem).wait()
        return carry

    lax.fori_loop(0, tb, drain, 0, unroll=8)


def _combine_kernel(pos_ref, yb_ref, x_ref, g_ref, gates_ref, o_ref, buf, sem):
    tb = x_ref.shape[0]
    base = pl.program_id(0) * (tb * TOP_K)

    def issue(j, carry):
        for k in range(TOP_K):
            src = pos_ref[base + j * TOP_K + k]
            pltpu.make_async_copy(yb_ref.at[pl.ds(src, 1)], buf.at[k, pl.ds(j, 1)], sem).start(priority=k % 2)
        return carry

    lax.fori_loop(0, tb, issue, 0, unroll=8)

    def drain(j, carry):
        for k in range(TOP_K):
            pltpu.make_async_copy(yb_ref.at[pl.ds(0, 1)], buf.at[k, pl.ds(j, 1)], sem).wait()
        return carry

    lax.fori_loop(0, tb, drain, 0, unroll=8)
    gates = gates_ref[...]
    f = gates[:, 0:1] * buf[0]
    for k in range(1, TOP_K):
        f = f + gates[:, k:k + 1] * buf[k]
    o_ref[...] = x_ref[...] + g_ref[0] * f


def _expert_up_kernel(be_ref, nu_ref, nv_ref, x_ref, wg_ref, wl_ref, bg_ref, bl_ref, o_ref, wg_s, wl_s):
    r = pl.program_id(0)

    @pl.when(r < nu_ref[0])
    def _():
        wg_s[...] = wg_ref[...].astype(BF16)
        wl_s[...] = wl_ref[...].astype(BF16)
        half = x_ref.shape[1]
        for s in range(x_ref.shape[0] // MOE_SUB_ROWS):
            @pl.when(s * MOE_SUB_ROWS < nv_ref[r])
            def _(s=s):
                rows = slice(s * MOE_SUB_ROWS, (s + 1) * MOE_SUB_ROWS)
                lo, hi = _unpack_halves(x_ref[rows, :])
                glu = _dot(lo, wg_s[:half, :]) + _dot(hi, wg_s[half:, :]) + bg_ref[...]
                lin = _dot(lo, wl_s[:half, :]) + _dot(hi, wl_s[half:, :]) + bl_ref[...]
                glu = jnp.minimum(glu, SWIGLU_LIMIT)
                lin = jnp.clip(lin, -SWIGLU_LIMIT, SWIGLU_LIMIT)
                o_ref[rows, :] = (glu * _sigmoid(SWIGLU_ALPHA * glu) * (lin + 1.0)).astype(o_ref.dtype)


def _expert_down_kernel(be_ref, nu_ref, nv_ref, a_ref, w_ref, b_ref, o_ref, w_s):
    r = pl.program_id(0)

    @pl.when(r < nu_ref[0])
    def _():
        w_s[...] = w_ref[...].astype(BF16)
        for s in range(a_ref.shape[0] // MOE_SUB_ROWS):
            @pl.when(s * MOE_SUB_ROWS < nv_ref[r])
            def _(s=s):
                rows = slice(s * MOE_SUB_ROWS, (s + 1) * MOE_SUB_ROWS)
                o_ref[rows, :] = _dot(a_ref[rows, :], w_s[...]) + b_ref[...]


def expert_ffn(block_exp, n_used, n_valid, xb, w1, b1, w2, b2, layer, block_rows):
    n_rows, half = xb.shape
    d = 2 * half
    n_exp, _, f2 = w1.shape[1:]
    f = f2 // 2
    n_blocks = n_rows // block_rows
    tf = min(512, f)
    nf = f // tf
    b1r = b1.reshape(b1.shape[0], n_exp, 1, f2)
    b2r = b2.reshape(b2.shape[0], n_exp, 1, d)

    def rows(r, nu):
        return jnp.minimum(r, nu[0] - 1)

    def tile(r, j, nu, n_tiles):
        return jnp.where(r < nu[0], j, n_tiles - 1)

    act = pl.pallas_call(
        _expert_up_kernel,
        grid_spec=pltpu.PrefetchScalarGridSpec(
            num_scalar_prefetch=3,
            grid=(n_blocks, nf),
            in_specs=[
                pl.BlockSpec((block_rows, half), lambda r, j, be, nu, nv: (rows(r, nu), 0)),
                pl.BlockSpec((None, None, d, tf), lambda r, j, be, nu, nv: (layer, be[rows(r, nu)], 0, tile(r, j, nu, nf))),
                pl.BlockSpec((None, None, d, tf),
                             lambda r, j, be, nu, nv: (layer, be[rows(r, nu)], 0, nf + tile(r, j, nu, nf))),
                pl.BlockSpec((None, None, 1, tf), lambda r, j, be, nu, nv: (layer, be[rows(r, nu)], 0, tile(r, j, nu, nf))),
                pl.BlockSpec((None, None, 1, tf),
                             lambda r, j, be, nu, nv: (layer, be[rows(r, nu)], 0, nf + tile(r, j, nu, nf))),
            ],
            out_specs=pl.BlockSpec((block_rows, tf), lambda r, j, be, nu, nv: (rows(r, nu), tile(r, j, nu, nf))),
            scratch_shapes=[pltpu.VMEM((d, tf), BF16), pltpu.VMEM((d, tf), BF16)],
        ),
        out_shape=jax.ShapeDtypeStruct((n_rows, f), BF16),
        compiler_params=_cparams("arbitrary", "arbitrary"),
        name="expert_up",
    )(block_exp, n_used, n_valid, xb, w1, w1, b1r, b1r)
    tn = min(1024, d)
    nd = d // tn
    return pl.pallas_call(
        _expert_down_kernel,
        grid_spec=pltpu.PrefetchScalarGridSpec(
            num_scalar_prefetch=3,
            grid=(n_blocks, nd),
            in_specs=[
                pl.BlockSpec((block_rows, f), lambda r, j, be, nu, nv: (rows(r, nu), 0)),
                pl.BlockSpec((None, None, f, tn), lambda r, j, be, nu, nv: (layer, be[rows(r, nu)], 0, tile(r, j, nu, nd))),
                pl.BlockSpec((None, None, 1, tn), lambda r, j, be, nu, nv: (layer, be[rows(r, nu)], 0, tile(r, j, nu, nd))),
            ],
            out_specs=pl.BlockSpec((block_rows, tn), lambda r, j, be, nu, nv: (rows(r, nu), tile(r, j, nu, nd))),
            scratch_shapes=[pltpu.VMEM((f, tn), BF16)],
        ),
        out_shape=jax.ShapeDtypeStruct((n_rows, d), F32),
        compiler_params=_cparams("arbitrary", "arbitrary"),
        name="expert_down",
    )(block_exp, n_used, n_valid, act, w2, b2r)


def _blocked_cumsum(onehot):
    n, e = onehot.shape
    blk = math.gcd(n, 512)
    x = onehot.reshape(n // blk, blk, e).astype(BF16)
    tril = jnp.tril(jnp.ones((blk, blk), BF16))
    within = jnp.einsum("ij,bje->bie", tril, x, preferred_element_type=F32).astype(jnp.int32)
    totals = within[:, -1, :]
    offsets = jnp.cumsum(totals, axis=0) - totals
    return (within + offsets[:, None, :]).reshape(n, e)


def moe_residual(x, hp, logits, gate, w1, b1, w2, b2, layer, rows_lat, seq):
    n_tok, half = hp.shape
    d = 2 * half
    n_exp = w1.shape[1]
    top_val, top_idx = lax.top_k(logits, TOP_K)
    gates = jax.nn.softmax(top_val, axis=-1)
    n_assign = n_tok * TOP_K
    flat_e = top_idx.reshape(-1)
    onehot = (flat_e[:, None] == jnp.arange(n_exp, dtype=flat_e.dtype)[None, :]).astype(jnp.int32)
    csum = _blocked_cumsum(onehot)
    counts = csum[-1]
    block_rows = MOE_SUB_ROWS * -(-int(MOE_LOAD_MARGIN * n_assign / n_exp) // MOE_SUB_ROWS)
    padded = (counts + block_rows - 1) // block_rows * block_rows
    pend = jnp.cumsum(padded)
    pstart = pend - padded
    pos = jnp.sum(onehot * (csum - 1 + pstart[None, :]), axis=1).astype(jnp.int32)
    n_blocks = -(-n_assign // block_rows) + n_exp
    n_rows = n_blocks * block_rows
    block_start = jnp.arange(n_blocks, dtype=jnp.int32) * block_rows
    block_exp = jnp.minimum(jnp.sum((block_start[:, None] >= pend[None, :]).astype(jnp.int32), axis=1), n_exp - 1)
    n_used = (pend[-1] // block_rows).astype(jnp.int32).reshape(1)
    last_row = (pstart + counts)[block_exp]
    n_valid = jnp.clip(last_row - block_start, 0, block_rows).astype(jnp.int32)
    pad_lo = (pstart + counts).astype(jnp.int32)
    pad_hi = (pstart + (counts + MOE_SUB_ROWS - 1) // MOE_SUB_ROWS * MOE_SUB_ROWS).astype(jnp.int32)

    tb = math.gcd(128, _stream_tile(n_tok, rows_lat, seq))
    xb = pl.pallas_call(
        _dispatch_kernel,
        grid_spec=pltpu.PrefetchScalarGridSpec(
            num_scalar_prefetch=3,
            grid=(n_tok // tb,),
            in_specs=[pl.BlockSpec((tb, half), lambda i, pos, lo, hi: (i, 0))],
            out_specs=pl.BlockSpec(memory_space=pl.ANY),
            scratch_shapes=[pltpu.VMEM((8, half), jnp.uint32), pltpu.SemaphoreType.DMA(())],
        ),
        out_shape=jax.ShapeDtypeStruct((n_rows, half), jnp.uint32),
        compiler_params=_cparams("arbitrary"),
        name="expert_dispatch",
    )(pos, pad_lo, pad_hi, hp)
    yb = expert_ffn(block_exp.astype(jnp.int32), n_used, n_valid, xb, w1, b1, w2, b2, layer, block_rows)
    n_batch = gate.shape[0] - 1
    idx = functools.partial(_stream_index, tm=tb, rows_lat=rows_lat, seq=seq, n_batch=n_batch)
    return pl.pallas_call(
        _combine_kernel,
        grid_spec=pltpu.PrefetchScalarGridSpec(
            num_scalar_prefetch=1,
            grid=(n_tok // tb,),
            in_specs=[pl.BlockSpec(memory_space=pl.ANY),
                      pl.BlockSpec((tb, d), lambda i, pos: (i, 0)),
                      pl.BlockSpec((1, 1, d), lambda i, pos: (idx(i), 0, 0)),
                      pl.BlockSpec((tb, TOP_K), lambda i, pos: (i, 0))],
            out_specs=pl.BlockSpec((tb, d), lambda i, pos: (i, 0)),
            scratch_shapes=[pltpu.VMEM((TOP_K, tb, d), F32), pltpu.SemaphoreType.DMA(())],
        ),
        out_shape=jax.ShapeDtypeStruct((n_tok, d), F32),
        compiler_params=_cparams("arbitrary"),
        name="expert_combine",
    )(pos, yb, x, gate, gates)


def _layer(i, x, c_all, p, n_batch, seq, n_ctx, tables, with_ctx):
    d = x.shape[1]
    rows_lat = n_batch * seq
    rows_all = rows_lat + n_batch * n_ctx
    n_out = rows_all if with_ctx else rows_lat
    n_stream = n_batch + 1

    mod = matmul(c_all, p["w_mod"], i, c_all.shape[0], c_all.shape[0], 512, F32, pre="silu", name="modulation")
    mod = (mod[:n_stream] + p["b_mod"][i]).reshape(n_stream, 6, 1, d)
    sh1, sc1, g1, sh2, sc2, g2 = (mod[:, k] for k in range(6))

    h1 = norm_mod(x, p["g_mix"][i], sc1, sh1, rows_all, rows_lat, seq, BF16)
    in_width = p["w_in"].shape[2]

    n_g = p["ssm_lambda_re"].shape[2]
    ssm_w = n_g * SSM_GROUP
    n_wh = p["win_sink"].shape[1]
    win_q = n_wh * WIN_HEAD_DIM
    win_kv = WIN_KV_HEADS * WIN_HEAD_DIM
    n_dh = (in_width - ssm_w - win_q - 2 * win_kv - 3 * d) // (2 * 2 * DIFF_QK_DIM + DIFF_V_DIM)
    diff_w = n_dh * 2 * DIFF_QK_DIM
    widths = dict(u=ssm_w, qw=win_q, kw=win_kv, vw=win_kv, qd=diff_w, kd=diff_w, vd=diff_w, gates=3 * d)
    src, c0 = {}, 0
    for name in ("u", "qw", "kw", "vw", "qd", "kd", "vd", "gates"):
        src[name] = c0
        c0 += widths[name]
    modes = dict(u=("plain", 1.0, F32), qw=("rope", WIN_HEAD_DIM ** -0.5, BF16), kw=("rope", 1.0, BF16),
                 vw=("plain", 1.0, BF16), qd=("rope", DIFF_QK_DIM ** -0.5, BF16), kd=("rope", 1.0, BF16),
                 vd=("plain", 1.0, BF16), gates=("sigmoid", 1.0, BF16))
    seg = {}
    for name, (mode, scale, dtype) in modes.items():
        w_seg = p["w_in"][i][:, src[name]:src[name] + widths[name]].astype(BF16)
        seg[name] = in_proj(h1, w_seg, 0, widths[name], mode, scale, dtype, tables, rows_lat, seq)

    mats = s5_matrices(p["ssm_lambda_re"][i], p["ssm_lambda_im"][i], p["ssm_log_dt"][i], p["ssm_b_re"][i],
                       p["ssm_b_im"][i], p["ssm_c_re"][i], p["ssm_c_im"][i], p["ssm_d"][i], S5_CHUNK)
    y = s5_mixer(seg["u"], mats, rows_all, n_batch, seq, n_ctx)
    o_ssm = s5_glu(y, p["w_glu"], p["b_glu"], i, n_out)

    sink = p["win_sink"][i].astype(F32)
    o_win = window_attention(sink, seg["qw"], seg["kw"], seg["vw"], n_batch, seq, n_ctx, n_wh, rows_lat, False)

    lam_p = p["diff_lambda"][i].astype(F32)
    lambda_init = 0.8 - 0.6 * math.exp(-0.3 * i)
    lam = (jnp.exp(jnp.sum(lam_p[0] * lam_p[1])) - jnp.exp(jnp.sum(lam_p[2] * lam_p[3])) + lambda_init).reshape(1)
    gain = p["diff_subln"][i].astype(F32).reshape(1, DIFF_V_DIM)
    o_diff = diff_attention(lam, gain, seg["qd"], seg["kd"], seg["vd"], n_batch, seq, n_ctx, n_dh, rows_lat, False,
                            1.0 - lambda_init)
    if with_ctx:
        o_win_c = window_attention(sink, seg["qw"], seg["kw"], seg["vw"], n_batch, seq, n_ctx, n_wh, rows_lat, True)
        o_diff_c = diff_attention(lam, gain, seg["qd"], seg["kd"], seg["vd"], n_batch, seq, n_ctx, n_dh, rows_lat, True,
                                  1.0 - lambda_init)
        o_win = jnp.concatenate([o_win, o_win_c], axis=0)
        o_diff = jnp.concatenate([o_diff, o_diff_c], axis=0)

    merged = gated_merge(o_ssm, o_win, o_diff, p["w_branch_ssm"], p["w_branch_win"], p["w_branch_diff"], seg["gates"],
                         i, n_out, d)
    x = out_proj_residual(merged, p["w_out"], x, g1, i, n_out, rows_lat, seq)

    hp, logits = norm_mod(x, p["g_ffn"][i], sc2, sh2, n_out, rows_lat, seq, BF16,
                          router=(p["w_router"][i], p["b_router"][i]))
    return moe_residual(x, hp, logits, g2, p["w_exp1"], p["b_exp1"], p["w_exp2"], p["b_exp2"], i, rows_lat, seq)


def kernel(x, c, ctx, c_ctx, w_mod, b_mod, g_mix, g_ffn, w_in, ssm_lambda_re, ssm_lambda_im, ssm_log_dt, ssm_b_re, ssm_b_im, ssm_c_re, ssm_c_im, ssm_d, w_glu, b_glu, win_sink, diff_lambda, diff_subln, w_branch_ssm, w_branch_win, w_branch_diff, w_out, w_router, b_router, w_exp1, b_exp1, w_exp2, b_exp2, g_final):
    n_batch, seq, d = x.shape
    n_ctx = ctx.shape[1]
    depth = w_mod.shape[0]
    p = dict(w_mod=w_mod, b_mod=b_mod, g_mix=g_mix, g_ffn=g_ffn, w_in=w_in, ssm_lambda_re=ssm_lambda_re,
             ssm_lambda_im=ssm_lambda_im, ssm_log_dt=ssm_log_dt, ssm_b_re=ssm_b_re, ssm_b_im=ssm_b_im,
             ssm_c_re=ssm_c_re, ssm_c_im=ssm_c_im, ssm_d=ssm_d, w_glu=w_glu, b_glu=b_glu, win_sink=win_sink,
             diff_lambda=diff_lambda, diff_subln=diff_subln, w_branch_ssm=w_branch_ssm, w_branch_win=w_branch_win,
             w_branch_diff=w_branch_diff, w_out=w_out, w_router=w_router, b_router=b_router, w_exp1=w_exp1,
             b_exp1=b_exp1, w_exp2=w_exp2, b_exp2=b_exp2)
    rows_lat = n_batch * seq
    rows = jnp.concatenate([x.reshape(rows_lat, d), ctx.reshape(n_batch * n_ctx, d)], axis=0)
    c_all = jnp.concatenate([c, c_ctx[None, :], jnp.zeros((8 - (n_batch + 1) % 8, d), F32)], axis=0)
    tables = rope_tables(seq, WIN_HEAD_DIM)
    for i in range(depth):
        with_ctx = i < depth - 1
        rows = _layer(i, rows, c_all, p, n_batch, seq, n_ctx, tables, with_ctx)
    no_mod = jnp.zeros((n_batch + 1, 1, d), F32)
    out = norm_mod(rows, g_final, no_mod, no_mod, rows_lat, rows_lat, seq, F32)
    return out.reshape(n_batch, seq, d)
```
